```python
import jax
import jax.numpy as jnp
from jax import lax
import numpy as np

D_MODEL = 2048
BATCH = 2
SEQ = 4096
DEPTH = 2

GRID_W = 64
CTX_LEN = 256
N_MOD = 6
ATT_HEADS = 8
ATT_KV_HEADS = 2
HEAD_DIM = 128
ATT_W = ATT_HEADS * HEAD_DIM
KV_W = ATT_KV_HEADS * HEAD_DIM
ROPE_AXIS_DIM = HEAD_DIM // 2
ROPE_BASE = 10000.0
Q_BLOCK = 128
ML_HEADS = 4
ML_DK = 256
ML_DV = 256
ML_W = ML_HEADS * ML_DV
CONV_W = 3
HG_HEADS = 8
HG_DK = 128
HG_DV = 128
HG_W = HG_HEADS * HG_DV
CHUNK = 64
N_BRANCH = 3
BRANCH_W = 1024
D_FF = 5632
N_EXPERTS = 8
TOP_K = 2
EPS = 1e-6
NEG_INIT = -1e30
IN_SPLITS = (ATT_W, KV_W, KV_W, ML_W, ML_W, ML_W, ML_W, 4 * ML_HEADS, HG_W, HG_W, HG_W, HG_W, HG_W, N_BRANCH * D_MODEL)
N_IN = sum(IN_SPLITS)

kernel_name = 'hybrid_gqa_mlstm_hgrn2_moe_prefix_block'


def rms_norm(x, g):
    x32 = x.astype(jnp.float32)
    y = x32 * lax.rsqrt(jnp.mean(x32 * x32, axis=-1, keepdims=True) + EPS)
    return (y * g.astype(jnp.float32)).astype(x.dtype)


def split_cols(p):
    return jnp.split(p, np.cumsum(IN_SPLITS)[:-1].tolist(), axis=-1)


def to_heads(t, n):
    b, s, _ = t.shape
    return t.reshape(b, s, n, -1).transpose(0, 2, 1, 3)


def head_rms(h, g):
    b, n, s, d = h.shape
    y = rms_norm(h.transpose(0, 2, 1, 3), g.reshape(n, d))
    return y.reshape(b, s, n * d)


def to_chunks(t):
    b, n, s = t.shape[:3]
    return jnp.moveaxis(t.reshape((b, n, s // CHUNK, CHUNK) + t.shape[3:]), 2, 0)


def from_chunks(t):
    t = jnp.moveaxis(t, 0, 2)
    return t.reshape(t.shape[:2] + (-1,) + t.shape[4:])


FLIPS = (lambda t: t, lambda t: jnp.flip(t, axis=2))


def axial_rope_tables(n, dtype):
    rows = n // GRID_W
    t = jnp.arange(rows * GRID_W)
    pos = jnp.stack([t // GRID_W, t % GRID_W], axis=-1).astype(jnp.float32)
    inv_freq = jnp.exp(-jnp.log(ROPE_BASE) * jnp.arange(0, ROPE_AXIS_DIM, 2, dtype=jnp.float32) / ROPE_AXIS_DIM)
    ang = pos[..., None] * inv_freq
    return jnp.cos(ang).astype(dtype), jnp.sin(ang).astype(dtype)


def apply_rope(x, cos, sin):
    b, s, h, _ = x.shape
    xr = x.reshape(b, s, h, 2, 2, ROPE_AXIS_DIM // 2)
    x1, x2 = xr[..., 0, :], xr[..., 1, :]
    co, si = cos[None, :, None], sin[None, :, None]
    return jnp.stack([x1 * co - x2 * si, x2 * co + x1 * si], axis=-2).reshape(x.shape)


def centred_conv(x, w):
    pad = CONV_W // 2
    s = x.shape[1]
    xp = jnp.pad(x, ((0, 0), (pad, pad), (0, 0)))
    y = xp[:, 0:s] * w[0]
    for j in range(1, CONV_W):
        y = y + xp[:, j:j + s] * w[j]
    return y


def attention_branch(pc, pl, qg, kg, cos, sin, need_ctx):
    scale = HEAD_DIM ** -0.5
    grp = ATT_HEADS // ATT_KV_HEADS
    kc = rms_norm(pc[1].reshape(pc[1].shape[:2] + (ATT_KV_HEADS, HEAD_DIM)), kg)
    vc = pc[2].reshape(pc[2].shape[:2] + (ATT_KV_HEADS, HEAD_DIM))
    b, s, _ = pl[0].shape
    kl = apply_rope(rms_norm(pl[1].reshape(b, s, ATT_KV_HEADS, HEAD_DIM), kg), cos, sin)
    vl = pl[2].reshape(b, s, ATT_KV_HEADS, HEAD_DIM)
    keys = jnp.concatenate([kc, kl], axis=1).transpose(0, 2, 1, 3)
    vals = jnp.concatenate([vc, vl], axis=1).transpose(0, 2, 1, 3)
    ql = apply_rope(rms_norm(pl[0].reshape(b, s, ATT_HEADS, HEAD_DIM), qg), cos, sin) * scale
    qb = ql.reshape(b, s // Q_BLOCK, Q_BLOCK, ATT_KV_HEADS, grp, HEAD_DIM).transpose(1, 0, 3, 4, 2, 5)

    def block(qblk):
        sc = jnp.einsum('bkgqd,bktd->bkgqt', qblk, keys).astype(jnp.float32)
        p = jax.nn.softmax(sc, axis=-1).astype(vals.dtype)
        return jnp.einsum('bkgqt,bktd->bkgqd', p, vals)

    ob = lax.map(block, qb)
    y_l = ob.transpose(1, 0, 4, 2, 3, 5).reshape(b, s, ATT_W)
    y_c = None
    if need_ctx:
        bc, tc, _ = pc[0].shape
        qc = rms_norm(pc[0].reshape(bc, tc, ATT_HEADS, HEAD_DIM), qg) * scale
        qc = qc.reshape(bc, tc, ATT_KV_HEADS, grp, HEAD_DIM)
        sc = jnp.einsum('bqkgd,btkd->bkgqt', qc, kc).astype(jnp.float32)
        p = jax.nn.softmax(sc, axis=-1).astype(vc.dtype)
        y_c = jnp.einsum('bkgqt,btkd->bqkgd', p, vc).reshape(bc, tc, ATT_W)
    return y_c, y_l


def mlstm_zero_state(b):
    return (jnp.zeros((b, ML_HEADS, ML_DK, ML_DV), jnp.float32), jnp.zeros((b, ML_HEADS, ML_DK), jnp.float32),
            jnp.full((b, ML_HEADS), NEG_INIT, jnp.float32))


def mlstm_final_state(k, v, log_i, log_f):
    f_cum = jnp.cumsum(log_f, axis=-1)
    g = f_cum[..., -1:] - f_cum + log_i
    m = jnp.max(g, axis=-1)
    w = jnp.exp(g - m[..., None])
    return (jnp.einsum('bhs,bhsd,bhse->bhde', w, k, v), jnp.einsum('bhs,bhsd->bhd', w, k), m)


def mlstm_chunkwise(q, k, v, log_i, log_f, state):
    tril = jnp.tril(jnp.ones((CHUNK, CHUNK), bool))

    def step(carry, inp):
        c_st, n_st, m_st = carry
        qc, kc, vc, li, lf = inp
        f_cum = jnp.cumsum(lf, axis=-1)
        dmat = jnp.where(tril, f_cum[..., :, None] - f_cum[..., None, :] + li[..., None, :], -jnp.inf)
        inter = f_cum + m_st[..., None]
        m_t = jnp.maximum(inter, jnp.max(dmat, axis=-1))
        w = jnp.exp(dmat - m_t[..., None])
        a = jnp.exp(inter - m_t)
        sc = jnp.einsum('bhtd,bhsd->bhts', qc, kc) * w
        num = jnp.einsum('bhts,bhse->bhte', sc, vc) + a[..., None] * jnp.einsum('bhtd,bhde->bhte', qc, c_st)
        den = jnp.sum(sc, axis=-1) + a * jnp.einsum('bhtd,bhd->bht', qc, n_st)
        h = num / jnp.maximum(jnp.abs(den), jnp.exp(-m_t))[..., None]
        f_last = f_cum[..., -1]
        g = f_last[..., None] - f_cum + li
        m_new = jnp.maximum(f_last + m_st, jnp.max(g, axis=-1))
        ws = jnp.exp(g - m_new[..., None])
        decay = jnp.exp(f_last + m_st - m_new)
        c_new = decay[..., None, None] * c_st + jnp.einsum('bhs,bhsd,bhse->bhde', ws, kc, vc)
        n_new = decay[..., None] * n_st + jnp.einsum('bhs,bhsd->bhd', ws, kc)
        return (c_new, n_new, m_new), h

    _, h = lax.scan(step, state, tuple(to_chunks(t) for t in (q, k, v, log_i, log_f)))
    return from_chunks(h)


def mlstm_branch(pc, pl, conv_w, gate_b, norm_g, need_ctx):
    def prep(q, k, v, gates):
        b, s, _ = q.shape
        qk = centred_conv(jnp.concatenate([q, k], axis=-1), conv_w).astype(jnp.float32)
        qh = to_heads(qk[..., :ML_W], ML_HEADS)
        kh = to_heads(qk[..., ML_W:], ML_HEADS) * ML_DK ** -0.5
        vh = to_heads(v.astype(jnp.float32), ML_HEADS)
        g = (gates + gate_b).astype(jnp.float32).reshape(b, s, 4, ML_HEADS).transpose(2, 0, 3, 1)
        return qh, kh, vh, g[:2], jax.nn.log_sigmoid(g[2:])

    def finish(h, o):
        return (head_rms(h, norm_g) * jax.nn.sigmoid(o.astype(jnp.float32))).astype(o.dtype)

    qc, kc, vc, lic, lfc = prep(pc[0], pc[1], pc[2], pc[4])
    ql, kl, vl, lil, lfl = prep(pl[0], pl[1], pl[2], pl[4])
    outs_l, outs_c = [], []
    for d in range(2):
        fl = FLIPS[d]
        state = mlstm_final_state(fl(kc), fl(vc), fl(lic[d]), fl(lfc[d]))
        outs_l.append(fl(mlstm_chunkwise(fl(ql), fl(kl), fl(vl), fl(lil[d]), fl(lfl[d]), state)))
        if need_ctx:
            outs_c.append(fl(mlstm_chunkwise(fl(qc), fl(kc), fl(vc), fl(lic[d]), fl(lfc[d]), mlstm_zero_state(qc.shape[0]))))
    y_l = finish(outs_l[0] + outs_l[1], pl[3])
    y_c = finish(outs_c[0] + outs_c[1], pc[3]) if need_ctx else None
    return y_c, y_l


def hgrn2_final_state(k, v, log_f):
    g_cum = jnp.cumsum(log_f, axis=2)
    return jnp.einsum('bhsd,bhse->bhde', k * jnp.exp(g_cum[:, :, -1:] - g_cum), v)


def hgrn2_chunkwise(q, k, v, log_f, state):
    tril = jnp.tril(jnp.ones((CHUNK, CHUNK), bool))[:, :, None]

    def step(s_st, inp):
        qc, kc, vc, gc = inp
        g_cum = jnp.cumsum(gc, axis=2)
        diff = g_cum[:, :, :, None, :] - g_cum[:, :, None, :, :]
        dec = jnp.exp(jnp.where(tril, diff, -jnp.inf))
        att = jnp.einsum('bhtd,bhsd,bhtsd->bhts', qc, kc, dec)
        o = jnp.einsum('bhts,bhse->bhte', att, vc) + jnp.einsum('bhtd,bhde->bhte', qc * jnp.exp(g_cum), s_st)
        g_last = g_cum[:, :, -1:]
        s_new = jnp.exp(g_last[:, :, 0])[..., None] * s_st + jnp.einsum('bhsd,bhse->bhde', kc * jnp.exp(g_last - g_cum), vc)
        return s_new, o

    _, o = lax.scan(step, state, tuple(to_chunks(t) for t in (q, k, v, log_f)))
    return from_chunks(o)


def hgrn2_branch(pc, pl, lb, norm_g, need_ctx):
    lbh = lb.reshape(HG_HEADS, 1, HG_DK)

    def prep(q, f_fwd, f_bwd, i):
        qh = to_heads(jax.nn.silu(q.astype(jnp.float32)), HG_HEADS)
        vh = to_heads(i.astype(jnp.float32), HG_HEADS)
        log_f = [jnp.logaddexp(jnp.log(lbh), jnp.log1p(-lbh) + jax.nn.log_sigmoid(to_heads(z.astype(jnp.float32), HG_HEADS)))
                 for z in (f_fwd, f_bwd)]
        return qh, vh, log_f, [-jnp.expm1(lf) for lf in log_f]

    def finish(o, g):
        return (head_rms(o, norm_g) * jax.nn.silu(g.astype(jnp.float32))).astype(g.dtype)

    qc, vc, lfc, kc = prep(pc[0], pc[1], pc[2], pc[3])
    ql, vl, lfl, kl = prep(pl[0], pl[1], pl[2], pl[3])
    outs_l, outs_c = [], []
    for d in range(2):
        fl = FLIPS[d]
        state = hgrn2_final_state(fl(kc[d]), fl(vc), fl(lfc[d]))
        outs_l.append(fl(hgrn2_chunkwise(fl(ql), fl(kl[d]), fl(vl), fl(lfl[d]), state)))
        if need_ctx:
            zero = jnp.zeros((qc.shape[0], HG_HEADS, HG_DK, HG_DV), jnp.float32)
            outs_c.append(fl(hgrn2_chunkwise(fl(qc), fl(kc[d]), fl(vc), fl(lfc[d]), zero)))
    y_l = finish(outs_l[0] + outs_l[1], pl[4])
    y_c = finish(outs_c[0] + outs_c[1], pc[4]) if need_ctx else None
    return y_c, y_l


def token_mixing(hc, hl, w_in, qg, kg, conv_w, gate_b, ml_g, lb, hg_g, w_branch, w_out, cos, sin, need_ctx):
    pc = split_cols(hc @ w_in)
    pl = split_cols(hl @ w_in)
    ya_c, ya_l = attention_branch(pc[0:3], pl[0:3], qg, kg, cos, sin, need_ctx)
    ym_c, ym_l = mlstm_branch(pc[3:8], pl[3:8], conv_w, gate_b, ml_g, need_ctx)
    yh_c, yh_l = hgrn2_branch(pc[8:13], pl[8:13], lb, hg_g, need_ctx)

    def merge(ys, gate_cols):
        proj = jnp.einsum('btnc,ncd->btnd', jnp.stack(ys, axis=-2), w_branch)
        gates = jax.nn.sigmoid(gate_cols).reshape(proj.shape)
        return jnp.sum(gates * proj, axis=-2) @ w_out

    y_l = merge((ya_l, ym_l, yh_l), pl[13])
    y_c = merge((ya_c, ym_c, yh_c), pc[13]) if need_ctx else None
    return y_c, y_l


def swiglu(h, wg, wu, wd):
    return (jax.nn.silu(h @ wg) * (h @ wu)) @ wd


def moe_ffn(h, w_router, wg, wu, wd):
    logits = (h @ w_router).astype(jnp.float32)
    top_v, top_i = lax.top_k(logits, TOP_K)
    top_w = jax.nn.softmax(top_v, axis=-1)
    gate = jnp.sum(top_w[..., None] * jax.nn.one_hot(top_i, N_EXPERTS, dtype=jnp.float32), axis=-2).astype(h.dtype)
    out = jnp.zeros_like(h)
    for e in range(N_EXPERTS):
        out = out + gate[..., e:e + 1] * swiglu(h, wg[e], wu[e], wd[e])
    return out


def modulation(cvec, w, b):
    return [m[:, None, :] for m in jnp.split(jax.nn.silu(cvec) @ w + b, N_MOD, axis=-1)]


def setup_inputs(seed: int = 0) -> dict:
    key = jax.random.key(seed)
    ks = iter(jax.random.split(key, 32))
    f32 = jnp.float32
    n_dense = (DEPTH + 1) // 2
    n_moe = DEPTH // 2

    def nrm(shape, fan_in, s=1.0):
        return jax.random.normal(next(ks), shape, f32) * (s * fan_in ** -0.5)

    def gain(shape):
        return 1.0 + 0.05 * jax.random.normal(next(ks), shape, f32)

    ml_gate_b = jnp.concatenate([0.1 * jax.random.normal(next(ks), (DEPTH, 2 * ML_HEADS), f32),
                                 3.0 + 3.0 * jax.random.uniform(next(ks), (DEPTH, 2 * ML_HEADS), f32)], axis=-1)
    return {
        'x': jax.random.normal(next(ks), (BATCH, SEQ, D_MODEL), f32),
        'c': jax.random.normal(next(ks), (BATCH, D_MODEL), f32),
        'ctx': jax.random.normal(next(ks), (BATCH, CTX_LEN, D_MODEL), f32),
        'c_ctx': jax.random.normal(next(ks), (D_MODEL,), f32),
        'w_mod': nrm((DEPTH, D_MODEL, N_MOD * D_MODEL), D_MODEL, 0.5),
        'b_mod': 0.02 * jax.random.normal(next(ks), (DEPTH, N_MOD * D_MODEL), f32),
        'norm1_g': gain((DEPTH, D_MODEL)),
        'norm2_g': gain((DEPTH, D_MODEL)),
        'w_in': nrm((DEPTH, D_MODEL, N_IN), D_MODEL),
        'attn_q_norm_g': gain((DEPTH, HEAD_DIM)),
        'attn_k_norm_g': gain((DEPTH, HEAD_DIM)),
        'ml_conv_w': nrm((DEPTH, CONV_W, 2 * ML_W), CONV_W),
        'ml_gate_b': ml_gate_b,
        'ml_norm_g': gain((DEPTH, ML_W)),
        'hg_lb_logits': 0.5 * jax.random.normal(next(ks), (DEPTH, HG_HEADS * HG_DK), f32),
        'hg_norm_g': gain((DEPTH, HG_W)),
        'w_branch': nrm((DEPTH, N_BRANCH, BRANCH_W, D_MODEL), BRANCH_W),
        'w_out': nrm((DEPTH, D_MODEL, D_MODEL), D_MODEL),
        'ffn_w_gate': nrm((n_dense, D_MODEL, D_FF), D_MODEL),
        'ffn_w_up': nrm((n_dense, D_MODEL, D_FF), D_MODEL),
        'ffn_w_down': nrm((n_dense, D_FF, D_MODEL), D_FF),
        'moe_w_router': nrm((n_moe, D_MODEL, N_EXPERTS), D_MODEL),
        'moe_w_gate': nrm((n_moe, N_EXPERTS, D_MODEL, D_FF), D_MODEL),
        'moe_w_up': nrm((n_moe, N_EXPERTS, D_MODEL, D_FF), D_MODEL),
        'moe_w_down': nrm((n_moe, N_EXPERTS, D_FF, D_MODEL), D_FF),
    }


def reference(x, c, ctx, c_ctx, w_mod, b_mod, norm1_g, norm2_g, w_in, attn_q_norm_g, attn_k_norm_g, ml_conv_w,
              ml_gate_b, ml_norm_g, hg_lb_logits, hg_norm_g, w_branch, w_out, ffn_w_gate, ffn_w_up, ffn_w_down,
              moe_w_router, moe_w_gate, moe_w_up, moe_w_down):
    cos, sin = axial_rope_tables(x.shape[1], x.dtype)
    lb_all = jnp.cumsum(jax.nn.softmax(hg_lb_logits.astype(jnp.float32), axis=0), axis=0)
    lb_all = lb_all - lb_all[:1]
    for l in range(DEPTH):
        need_ctx = l < DEPTH - 1
        ml = modulation(c, w_mod[l], b_mod[l])
        mc = modulation(c_ctx[None], w_mod[l], b_mod[l])
        hl = rms_norm(x, norm1_g[l]) * (1 + ml[1]) + ml[0]
        hc = rms_norm(ctx, norm1_g[l]) * (1 + mc[1]) + mc[0]
        y_c, y_l = token_mixing(hc, hl, w_in[l], attn_q_norm_g[l], attn_k_norm_g[l], ml_conv_w[l], ml_gate_b[l],
                                ml_norm_g[l], lb_all[l], hg_norm_g[l], w_branch[l], w_out[l], cos, sin, need_ctx)
        x = x + ml[2] * y_l
        if l % 2 == 0:
            ffn = lambda h: swiglu(h, ffn_w_gate[l // 2], ffn_w_up[l // 2], ffn_w_down[l // 2])
        else:
            ffn = lambda h: moe_ffn(h, moe_w_router[l // 2], moe_w_gate[l // 2], moe_w_up[l // 2], moe_w_down[l // 2])
        x = x + ml[5] * ffn(rms_norm(x, norm2_g[l]) * (1 + ml[4]) + ml[3])
        if need_ctx:
            ctx = ctx + mc[2] * y_c
            ctx = ctx + mc[5] * ffn(rms_norm(ctx, norm2_g[l]) * (1 + mc[4]) + mc[3])
    return x
```

```python
import functools

import jax
import jax.numpy as jnp
from jax import lax
from jax.experimental import pallas as pl
from jax.experimental.pallas import tpu as pltpu

F32 = jnp.float32
BF16 = jnp.bfloat16

D_MODEL = 2048
GRID_W = 64
N_MOD = 6
ATT_HEADS = 8
ATT_KV_HEADS = 2
HEAD_DIM = 128
ATT_W = ATT_HEADS * HEAD_DIM
KV_W = ATT_KV_HEADS * HEAD_DIM
ROPE_AXIS_DIM = HEAD_DIM // 2
ROPE_BASE = 10000.0
ML_HEADS = 4
ML_DK = 256
ML_W = ML_HEADS * ML_DK
HG_HEADS = 8
HG_DK = 128
HG_W = HG_HEADS * HG_DK
D_FF = 5632
N_EXPERTS = 8
EPS = 1e-6
NEG_INIT = -1e30

_O_ATT_Q = 0
_O_ML_GATES = ATT_W + 2 * KV_W + 4 * ML_W
_O_HG_Q = _O_ML_GATES + 4 * ML_HEADS
_O_HG_FF = _O_HG_Q + HG_W
_O_HG_I = _O_HG_FF + 2 * HG_W
_O_MERGE = _O_HG_I + 2 * HG_W
_N_IN = _O_MERGE + 3 * D_MODEL

H_ATT_Q = 0
H_ATT_K = 1024
H_ATT_V = 1280
H_ML_Q = 1536
H_ML_K = 2560
H_ML_V = 3584
H_ML_O = 4608
H_HG_Q = 5632
H_HG_I = 6656
H_HG_G = 7680
H_MERGE = 8704
H_WIDTH = 14848
F_HG_FF = 0
F_HG_FB = 1024
F_ML_G = 2048
F_WIDTH = 2304

LANE = 128
ROW_TILE = 512
COL_TILE = 512
VMEM_LIMIT = 56 * 1024 * 1024

ML_CHUNK = 256
HG_CHUNK = 64
HG_LEAF = 8
ATT_TQ = 256
MOE_TILE = 512

_NT = (((1,), (1,)), ((), ()))
_TN = (((0,), (0,)), ((), ()))


def _params(n_axes):
    return pltpu.CompilerParams(dimension_semantics=("arbitrary",) * n_axes,
                                vmem_limit_bytes=VMEM_LIMIT)


def _silu(x):
    return x * jax.nn.sigmoid(x)


def _log_sigmoid(x):
    return jnp.minimum(x, 0.0) - jnp.log1p(jnp.exp(-jnp.abs(x)))


def _mod_kernel(c_ref, w_ref, b_ref, o_ref):
    a = _silu(c_ref[...])
    o_ref[...] = jnp.dot(a, w_ref[...], precision=lax.Precision.HIGHEST,
                         preferred_element_type=F32) + b_ref[...]


def _modulation(cs, w_mod, b_mod):
    depth, d, n = w_mod.shape
    tn = 1024
    return pl.pallas_call(
        _mod_kernel,
        grid=(depth, n // tn),
        in_specs=[pl.BlockSpec((8, d), lambda l, j: (0, 0)),
                  pl.BlockSpec((None, d, tn), lambda l, j: (l, 0, j)),
                  pl.BlockSpec((None, 1, tn), lambda l, j: (l, 0, j))],
        out_specs=pl.BlockSpec((None, 8, tn), lambda l, j: (l, 0, j)),
        out_shape=jax.ShapeDtypeStruct((depth, 8, n), F32),
        compiler_params=_params(2),
        name="modulation",
    )(cs, w_mod, b_mod.reshape(depth, 1, n))


def _norm_kernel(x_ref, g_ref, sh_ref, sc_ref, o_ref):
    x = x_ref[...]
    y = x * lax.rsqrt(jnp.mean(x * x, axis=-1, keepdims=True) + EPS)
    y = y * g_ref[...]
    o_ref[...] = (y * (1.0 + sc_ref[0]) + sh_ref[0]).astype(o_ref.dtype)


def _group_of(i, tiles_per_batch, n_batch):
    return jnp.minimum(i // tiles_per_batch, n_batch)


def _norm_mod(x, g, shift, scale, n_rows, seq, n_batch):
    d = x.shape[1]
    tm = 256
    tpb = seq // tm
    grp = lambda i: (_group_of(i, tpb, n_batch), 0, 0)
    return pl.pallas_call(
        _norm_kernel,
        grid=(n_rows // tm,),
        in_specs=[pl.BlockSpec((tm, d), lambda i: (i, 0)),
                  pl.BlockSpec((1, d), lambda i: (0, 0)),
                  pl.BlockSpec((1, 1, d), grp),
                  pl.BlockSpec((1, 1, d), grp)],
        out_specs=pl.BlockSpec((tm, d), lambda i: (i, 0)),
        out_shape=jax.ShapeDtypeStruct((n_rows, d), BF16),
        compiler_params=_params(1),
        name="norm_mod",
    )(x, g.reshape(1, d), shift, scale)


def _mm_kernel(a_ref, w_ref, o_ref):
    o_ref[...] = jnp.dot(a_ref[...], w_ref[...], preferred_element_type=F32).astype(o_ref.dtype)


def _matmul(a, w, out_dtype, tn):
    m, k = a.shape
    n = w.shape[1]
    tm = ROW_TILE
    return pl.pallas_call(
        _mm_kernel,
        grid=(n // tn, m // tm),
        in_specs=[pl.BlockSpec((tm, k), lambda j, i: (i, 0)),
                  pl.BlockSpec((k, tn), lambda j, i: (0, j))],
        out_specs=pl.BlockSpec((tm, tn), lambda j, i: (i, j)),
        out_shape=jax.ShapeDtypeStruct((m, n), out_dtype),
        compiler_params=_params(2),
        name="proj_in",
    )(a, w)


def _mm_res_kernel(a_ref, w_ref, r_ref, g_ref, o_ref, wb_ref):
    @pl.when(pl.program_id(1) == 0)
    def _():
        wb_ref[...] = w_ref[...].astype(BF16)

    y = jnp.dot(a_ref[...], wb_ref[...], preferred_element_type=F32)
    o_ref[...] = r_ref[...] + g_ref[0] * y


def _matmul_residual(a, w, res, gate, n_rows, seq, n_batch, name):
    k = a.shape[1]
    n = w.shape[1]
    tm, tn = ROW_TILE, COL_TILE
    tpb = seq // tm
    return pl.pallas_call(
        _mm_res_kernel,
        grid=(n // tn, n_rows // tm),
        in_specs=[pl.BlockSpec((tm, k), lambda j, i: (i, 0)),
                  pl.BlockSpec((k, tn), lambda j, i: (0, j)),
                  pl.BlockSpec((tm, tn), lambda j, i: (i, j)),
                  pl.BlockSpec((1, 1, tn), lambda j, i: (_group_of(i, tpb, n_batch), 0, j))],
        out_specs=pl.BlockSpec((tm, tn), lambda j, i: (i, j)),
        out_shape=jax.ShapeDtypeStruct((n_rows, n), F32),
        scratch_shapes=[pltpu.VMEM((k, tn), BF16)],
        compiler_params=_params(2),
        name=name,
    )(a, w, res, gate)


def _glu_kernel(a_ref, wg_ref, wu_ref, o_ref, wgb_ref, wub_ref):
    @pl.when(pl.program_id(1) == 0)
    def _():
        wgb_ref[...] = wg_ref[...].astype(BF16)
        wub_ref[...] = wu_ref[...].astype(BF16)

    a = a_ref[...]
    g = jnp.dot(a, wgb_ref[...], preferred_element_type=F32)
    u = jnp.dot(a, wub_ref[...], preferred_element_type=F32)
    o_ref[...] = (_silu(g) * u).astype(o_ref.dtype)


def _glu(a, wg, wu):
    m, k = a.shape
    n = wg.shape[1]
    tm, tn = ROW_TILE, COL_TILE
    return pl.pallas_call(
        _glu_kernel,
        grid=(n // tn, m // tm),
        in_specs=[pl.BlockSpec((tm, k), lambda j, i: (i, 0)),
                  pl.BlockSpec((k, tn), lambda j, i: (0, j)),
                  pl.BlockSpec((k, tn), lambda j, i: (0, j))],
        out_specs=pl.BlockSpec((tm, tn), lambda j, i: (i, j)),
        out_shape=jax.ShapeDtypeStruct((m, n), BF16),
        scratch_shapes=[pltpu.VMEM((k, tn), BF16), pltpu.VMEM((k, tn), BF16)],
        compiler_params=_params(2),
        name="ffn_up",
    )(a, wg, wu)


def _merge_kernel(ya_ref, ym_ref, yh_ref, w_ref, g0_ref, g1_ref, g2_ref, o_ref, wb_ref):
    @pl.when(pl.program_id(1) == 0)
    def _():
        wb_ref[...] = w_ref[...].astype(BF16)

    acc = jax.nn.sigmoid(g0_ref[...].astype(F32)) * jnp.dot(ya_ref[...], wb_ref[0], preferred_element_type=F32)
    acc += jax.nn.sigmoid(g1_ref[...].astype(F32)) * jnp.dot(ym_ref[...], wb_ref[1], preferred_element_type=F32)
    acc += jax.nn.sigmoid(g2_ref[...].astype(F32)) * jnp.dot(yh_ref[...], wb_ref[2], preferred_element_type=F32)
    o_ref[...] = acc.astype(o_ref.dtype)


def _branch_merge(ya, ym, yh, w_branch, proj_h, n_rows):
    k = ya.shape[1]
    n = w_branch.shape[2]
    tm, tn = ROW_TILE, COL_TILE
    gate_spec = lambda b: pl.BlockSpec((tm, tn), lambda j, i: (i, (H_MERGE + b * n) // tn + j))
    y_spec = pl.BlockSpec((tm, k), lambda j, i: (i, 0))
    return pl.pallas_call(
        _merge_kernel,
        grid=(n // tn, n_rows // tm),
        in_specs=[y_spec, y_spec, y_spec,
                  pl.BlockSpec((3, k, tn), lambda j, i: (0, 0, j)),
                  gate_spec(0), gate_spec(1), gate_spec(2)],
        out_specs=pl.BlockSpec((tm, tn), lambda j, i: (i, j)),
        out_shape=jax.ShapeDtypeStruct((n_rows, n), BF16),
        scratch_shapes=[pltpu.VMEM((3, k, tn), BF16)],
        compiler_params=_params(2),
        name="branch_merge",
    )(ya, ym, yh, w_branch, proj_h, proj_h, proj_h)


def _head_rms(x, g):
    return x * lax.rsqrt(jnp.mean(x * x, axis=-1, keepdims=True) + EPS) * g


def _rope(x, cos, sin, low_half):
    partner = jnp.where(low_half, pltpu.roll(x, 96, axis=1), pltpu.roll(x, 32, axis=1))
    return x * cos + partner * sin


def _att_prep_kernel(q_ref, k_ref, cos_ref, sin_ref, qg_ref, kg_ref, qo_ref, ko_ref):
    cos = cos_ref[...]
    sin = sin_ref[...]
    lane = lax.broadcasted_iota(jnp.int32, cos.shape, 1)
    low_half = jnp.bitwise_and(lane, 63) < 32
    scale = HEAD_DIM ** -0.5
    for h in range(ATT_HEADS):
        sl = slice(h * HEAD_DIM, (h + 1) * HEAD_DIM)
        x = _head_rms(q_ref[:, sl].astype(F32), qg_ref[...])
        qo_ref[:, sl] = (_rope(x, cos, sin, low_half) * scale).astype(qo_ref.dtype)
    for h in range(ATT_KV_HEADS):
        sl = slice(h * HEAD_DIM, (h + 1) * HEAD_DIM)
        x = _head_rms(k_ref[:, sl].astype(F32), kg_ref[...])
        ko_ref[:, sl] = _rope(x, cos, sin, low_half).astype(ko_ref.dtype)


def _att_prep(proj_h, cos, sin, qg, kg):
    m = proj_h.shape[0]
    tm = ROW_TILE
    return pl.pallas_call(
        _att_prep_kernel,
        grid=(m // tm,),
        in_specs=[pl.BlockSpec((tm, ATT_W), lambda i: (i, H_ATT_Q // ATT_W)),
                  pl.BlockSpec((tm, KV_W), lambda i: (i, H_ATT_K // KV_W)),
                  pl.BlockSpec((tm, HEAD_DIM), lambda i: (i, 0)),
                  pl.BlockSpec((tm, HEAD_DIM), lambda i: (i, 0)),
                  pl.BlockSpec((1, HEAD_DIM), lambda i: (0, 0)),
                  pl.BlockSpec((1, HEAD_DIM), lambda i: (0, 0))],
        out_specs=[pl.BlockSpec((tm, ATT_W), lambda i: (i, 0)),
                   pl.BlockSpec((tm, KV_W), lambda i: (i, 0))],
        out_shape=[jax.ShapeDtypeStruct((m, ATT_W), BF16),
                   jax.ShapeDtypeStruct((m, KV_W), BF16)],
        compiler_params=_params(1),
        name="att_prep",
    )(proj_h, proj_h, cos, sin, qg.reshape(1, HEAD_DIM), kg.reshape(1, HEAD_DIM))


def _att_kernel(*refs, with_latent):
    if with_latent:
        q_ref, kc_ref, vc_ref, kl_ref, vl_ref, o_ref = refs
    else:
        q_ref, kc_ref, vc_ref, o_ref = refs
    grp = ATT_HEADS // ATT_KV_HEADS
    for g in range(grp):
        sl = slice(g * HEAD_DIM, (g + 1) * HEAD_DIM)
        q = q_ref[:, sl]
        sc = lax.dot_general(q, kc_ref[...], _NT, preferred_element_type=F32)
        m = jnp.max(sc, axis=-1, keepdims=True)
        if with_latent:
            sl_ = lax.dot_general(q, kl_ref[...], _NT, preferred_element_type=F32)
            m = jnp.maximum(m, jnp.max(sl_, axis=-1, keepdims=True))
        pc = jnp.exp(sc - m)
        den = jnp.sum(pc, axis=-1, keepdims=True)
        acc = jnp.dot(pc.astype(BF16), vc_ref[...], preferred_element_type=F32)
        if with_latent:
            pl_ = jnp.exp(sl_ - m)
            den += jnp.sum(pl_, axis=-1, keepdims=True)
            acc += jnp.dot(pl_.astype(BF16), vl_ref[...], preferred_element_type=F32)
        o_ref[:, sl] = (acc / den).astype(o_ref.dtype)


def _attention(q_rot, k_rot, proj_h, n_batch, seq, ctx_len, latent_queries):
    grp_w = ATT_W // ATT_KV_HEADS
    ctx_blk0 = n_batch * seq // ctx_len
    v_col = H_ATT_V // HEAD_DIM
    if latent_queries:
        tq = ATT_TQ
        nq = seq // tq
        q_map = lambda b, k, i: (b * nq + i, k)
        n_out = n_batch * seq
    else:
        tq = ctx_len
        nq = 1
        q_map = lambda b, k, i: (ctx_blk0 + b, k)
        n_out = n_batch * ctx_len
    in_specs = [pl.BlockSpec((tq, grp_w), q_map),
                pl.BlockSpec((ctx_len, HEAD_DIM), lambda b, k, i: (ctx_blk0 + b, k)),
                pl.BlockSpec((ctx_len, HEAD_DIM), lambda b, k, i: (ctx_blk0 + b, v_col + k))]
    args = [q_rot, k_rot, proj_h]
    if latent_queries:
        in_specs += [pl.BlockSpec((seq, HEAD_DIM), lambda b, k, i: (b, k)),
                     pl.BlockSpec((seq, HEAD_DIM), lambda b, k, i: (b, v_col + k))]
        args += [k_rot, proj_h]
    return pl.pallas_call(
        functools.partial(_att_kernel, with_latent=latent_queries),
        grid=(n_batch, ATT_KV_HEADS, nq),
        in_specs=in_specs,
        out_specs=pl.BlockSpec((tq, grp_w), lambda b, k, i: (b * nq + i, k)),
        out_shape=jax.ShapeDtypeStruct((n_out, ATT_W), BF16),
        compiler_params=_params(3),
        name="attention_lat" if latent_queries else "attention_ctx",
    )(*args)


def _conv3(x_ref, w, c, n_chunks, chunk):
    r0 = pl.multiple_of(c * chunk, chunk)
    x = x_ref[pl.ds(r0, chunk), :].astype(F32)
    prev_blk = x_ref[pl.ds(pl.multiple_of(jnp.maximum(r0 - 16, 0), 16), 16), :].astype(F32)
    next_blk = x_ref[pl.ds(pl.multiple_of(jnp.minimum(r0 + chunk, (n_chunks - 1) * chunk), 16), 16), :].astype(F32)
    prev_row = jnp.where(c > 0, prev_blk[15:16, :], 0.0)
    next_row = jnp.where(c < n_chunks - 1, next_blk[0:1, :], 0.0)
    row = lax.broadcasted_iota(jnp.int32, x.shape, 0)
    x_m1 = jnp.where(row == 0, prev_row, pltpu.roll(x, 1, axis=0))
    x_p1 = jnp.where(row == chunk - 1, next_row, pltpu.roll(x, chunk - 1, axis=0))
    return x_m1 * w[0:1, :] + x * w[1:2, :] + x_p1 * w[2:3, :]


def _ml_chunk(q, k, v, lfr, lir, lfc, lic, c_ref, n_ref, m_st, rev):
    n_tok = q.shape[0]
    ti = lax.broadcasted_iota(jnp.int32, (n_tok, n_tok), 0)
    si = lax.broadcasted_iota(jnp.int32, (n_tok, n_tok), 1)
    causal = (si >= ti) if rev else (si <= ti)
    causal_t = (ti >= si) if rev else (ti <= si)
    fc_col = jnp.sum(jnp.where(causal, lfr, 0.0), axis=1, keepdims=True)
    fc_row = jnp.sum(jnp.where(causal_t, lfc, 0.0), axis=0, keepdims=True)
    f_tot = jnp.sum(lfr, axis=1, keepdims=True)
    dmat = jnp.where(causal, fc_col - fc_row + lir, -jnp.inf)
    inter = fc_col + m_st
    m_t = jnp.maximum(inter, jnp.max(dmat, axis=1, keepdims=True))
    w = jnp.exp(dmat - m_t)
    a = jnp.exp(inter - m_t)
    sc = lax.dot_general(q, k, _NT, preferred_element_type=F32) * w
    num = jnp.dot(sc.astype(BF16), v, preferred_element_type=F32)
    num += a * jnp.dot(q, c_ref[...].astype(BF16), preferred_element_type=F32)
    den = jnp.sum(sc, axis=1, keepdims=True)
    den += a * jnp.sum(q.astype(F32) * n_ref[...], axis=1, keepdims=True)
    h = num / jnp.maximum(jnp.abs(den), jnp.exp(-m_t))
    g_col = f_tot - fc_col + lic
    g_row = f_tot - fc_row + lir
    m_new = jnp.maximum(f_tot + m_st, jnp.max(g_row, axis=1, keepdims=True))
    decay = jnp.exp(f_tot + m_st - m_new)
    kw = k.astype(F32) * jnp.exp(g_col - m_new)
    c_ref[...] = decay * c_ref[...] + lax.dot_general(kw.astype(BF16), v, _TN, preferred_element_type=F32)
    n_ref[...] = decay * n_ref[...] + jnp.sum(kw, axis=0, keepdims=True)
    return h, m_new


def _ml_kernel(*refs, seq, ctx_len, need_ctx):
    (ql_ref, kl_ref, vl_ref, ol_ref, qc_ref, kc_ref, vc_ref, oc_ref,
     wq_ref, wk_ref, gcl_ref, grl_ref, gcc_ref, grc_ref, bc_ref, br_ref, ng_ref) = refs[:17]
    if need_ctx:
        yl_ref, yc_ref = refs[17:19]
        scratch = refs[19:]
    else:
        yl_ref = refs[17]
        yc_ref = None
        scratch = refs[18:]
    qsl_ref, ksl_ref, qsc_ref, ksc_ref, hl_ref, hc_ref, c_ref, n_ref = scratch
    chunk = ML_CHUNK
    k_scale = ML_DK ** -0.5
    wq = wq_ref[0]
    wk = wk_ref[0]

    def conv_pass(src_q, src_k, dst_q, dst_k, n_chunks):
        def body(c, carry):
            r0 = pl.multiple_of(c * chunk, chunk)
            dst_q[pl.ds(r0, chunk), :] = _conv3(src_q, wq, c, n_chunks, chunk).astype(BF16)
            dst_k[pl.ds(r0, chunk), :] = (_conv3(src_k, wk, c, n_chunks, chunk) * k_scale).astype(BF16)
            return carry
        lax.fori_loop(0, n_chunks, body, 0)

    conv_pass(qc_ref, kc_ref, qsc_ref, ksc_ref, ctx_len // chunk)
    conv_pass(ql_ref, kl_ref, qsl_ref, ksl_ref, seq // chunk)

    bias_c = bc_ref[0]
    bias_r = br_ref[0]
    norm_g = ng_ref[0]

    def run(q_s, k_s, v_s, o_s, gc_s, gr_s, h_s, y_s, n_chunks, m_st, rev, last):
        d = 1 if rev else 0

        def body(ci, m_st):
            c = (n_chunks - 1 - ci) if rev else ci
            r0 = pl.multiple_of(c * chunk, chunk)
            rows = pl.ds(r0, chunk)
            gc = gc_s[0, rows, :] + bias_c
            gr = gr_s[0, :, rows] + bias_r
            lic = gc[:, d:d + 1]
            lfc = _log_sigmoid(gc[:, 2 + d:3 + d])
            lir = gr[d:d + 1, :]
            lfr = _log_sigmoid(gr[2 + d:3 + d, :])
            h, m_new = _ml_chunk(q_s[rows, :], k_s[rows, :], v_s[rows, :], lfr, lir, lfc, lic,
                                 c_ref, n_ref, m_st, rev)
            if not last:
                h_s[rows, :] = h
            elif y_s is not None:
                hs = h_s[rows, :] + h
                y = _head_rms(hs, norm_g) * jax.nn.sigmoid(o_s[rows, :].astype(F32))
                y_s[rows, :] = y.astype(y_s.dtype)
            return m_new

        return lax.fori_loop(0, n_chunks, body, m_st)

    for rev in (False, True):
        c_ref[...] = jnp.zeros_like(c_ref)
        n_ref[...] = jnp.zeros_like(n_ref)
        m0 = jnp.full((1, 1), NEG_INIT, F32)
        m1 = run(qsc_ref, ksc_ref, vc_ref, oc_ref, gcc_ref, grc_ref, hc_ref, yc_ref,
                 ctx_len // chunk, m0, rev, rev)
        run(qsl_ref, ksl_ref, vl_ref, ol_ref, gcl_ref, grl_ref, hl_ref, yl_ref,
            seq // chunk, m1, rev, rev)


def _mlstm(proj_h, gates_c, gates_r, conv_w, gate_b, norm_g, n_batch, seq, ctx_len, need_ctx):
    w = ML_DK
    cb0 = n_batch * seq // ctx_len
    col = lambda off: off // w
    lat = lambda off: pl.BlockSpec((seq, w), lambda b, h: (b, col(off) + h))
    ctx = lambda off: pl.BlockSpec((ctx_len, w), lambda b, h: (cb0 + b, col(off) + h))
    conv = conv_w.reshape(3, 2 * ML_HEADS, w).transpose(1, 0, 2)
    gb = gate_b.reshape(4, ML_HEADS).T
    in_specs = [lat(H_ML_Q), lat(H_ML_K), lat(H_ML_V), lat(H_ML_O),
                ctx(H_ML_Q), ctx(H_ML_K), ctx(H_ML_V), ctx(H_ML_O),
                pl.BlockSpec((1, 3, w), lambda b, h: (h, 0, 0)),
                pl.BlockSpec((1, 3, w), lambda b, h: (ML_HEADS + h, 0, 0)),
                pl.BlockSpec((1, seq, 4), lambda b, h: (h, b, 0)),
                pl.BlockSpec((1, 4, seq), lambda b, h: (h, 0, b)),
                pl.BlockSpec((1, ctx_len, 4), lambda b, h: (h, cb0 + b, 0)),
                pl.BlockSpec((1, 4, ctx_len), lambda b, h: (h, 0, cb0 + b)),
                pl.BlockSpec((1, 1, 4), lambda b, h: (h, 0, 0)),
                pl.BlockSpec((1, 4, 1), lambda b, h: (h, 0, 0)),
                pl.BlockSpec((1, 1, w), lambda b, h: (h, 0, 0))]
    out_specs = [pl.BlockSpec((seq, w), lambda b, h: (b, h))]
    out_shape = [jax.ShapeDtypeStruct((n_batch * seq, ML_W), BF16)]
    if need_ctx:
        out_specs.append(pl.BlockSpec((ctx_len, w), lambda b, h: (b, h)))
        out_shape.append(jax.ShapeDtypeStruct((n_batch * ctx_len, ML_W), BF16))
    outs = pl.pallas_call(
        functools.partial(_ml_kernel, seq=seq, ctx_len=ctx_len, need_ctx=need_ctx),
        grid=(n_batch, ML_HEADS),
        in_specs=in_specs,
        out_specs=out_specs,
        out_shape=out_shape,
        scratch_shapes=[pltpu.VMEM((seq, w), BF16), pltpu.VMEM((seq, w), BF16),
                        pltpu.VMEM((ctx_len, w), BF16), pltpu.VMEM((ctx_len, w), BF16),
                        pltpu.VMEM((seq, w), F32), pltpu.VMEM((ctx_len, w), F32),
                        pltpu.VMEM((w, w), F32), pltpu.VMEM((1, w), F32)],
        compiler_params=_params(2),
        name="mlstm",
    )(proj_h, proj_h, proj_h, proj_h, proj_h, proj_h, proj_h, proj_h,
      conv, conv, gates_c, gates_r, gates_c, gates_r,
      gb.reshape(ML_HEADS, 1, 4), gb.reshape(ML_HEADS, 4, 1), norm_g.reshape(ML_HEADS, 1, w))
    return outs if need_ctx else (outs[0], None)


def _hg_level_tables(chunk, rev):
    levels = []
    n = HG_LEAF
    while 2 * n <= chunk:
        refs = []
        for start in range(0, chunk, 2 * n):
            refs.append((start + n - 1) if rev else (start + n))
        levels.append((n, refs))
        n *= 2
    return levels


def _hg_chunk(qraw, z, v, loglb, log1mlb, one_m_lb, st_ref, masks, tri, rev):
    chunk = z.shape[0]
    q = _silu(qraw.astype(F32))
    vf = v.astype(F32)
    e = jnp.exp(-jnp.abs(z))
    lsig = jnp.minimum(z, 0.0) - jnp.log1p(e)
    a = log1mlb + lsig
    lf = jnp.maximum(loglb, a) + jnp.log1p(jnp.exp(-jnp.abs(loglb - a)))
    k = one_m_lb * (jnp.where(z >= 0.0, e, 1.0) / (1.0 + e))
    gcum = jnp.dot(tri, lf, precision=lax.Precision.HIGHEST, preferred_element_type=F32)

    att = jnp.zeros((chunk, chunk), F32)
    for (n, refs), mask in zip(_hg_level_tables(chunk, rev), masks):
        gref = jnp.concatenate(
            [jnp.broadcast_to(gcum[r:r + 1, :], (2 * n, HG_DK)) for r in refs], axis=0)
        dec = jnp.exp(-jnp.abs(gcum - gref))
        p = lax.dot_general((q * dec).astype(BF16), (k * dec).astype(BF16), _NT,
                            preferred_element_type=F32)
        att += jnp.where(mask, p, 0.0)
    out = jnp.dot(att.astype(BF16), v, preferred_element_type=F32)

    nb = chunk // HG_LEAF
    g3 = gcum.reshape(nb, HG_LEAF, HG_DK)
    q3 = q.reshape(nb, HG_LEAF, HG_DK)
    k3 = k.reshape(nb, HG_LEAF, HG_DK)
    v3 = vf.reshape(nb, HG_LEAF, HG_DK)
    t_idx = lax.broadcasted_iota(jnp.int32, (1, HG_LEAF, 1), 1)
    leaf = jnp.zeros((nb, HG_LEAF, HG_DK), F32)
    for s in range(HG_LEAF):
        dec = jnp.exp(jnp.minimum(g3 - g3[:, s:s + 1, :], 0.0))
        col = jnp.sum(q3 * dec * k3[:, s:s + 1, :], axis=2, keepdims=True)
        valid = (t_idx <= s) if rev else (t_idx >= s)
        leaf += jnp.where(valid, col, 0.0) * v3[:, s:s + 1, :]
    out += leaf.reshape(chunk, HG_DK)

    g_tot = gcum[0:1, :] if rev else gcum[chunk - 1:chunk, :]
    st = st_ref[...]
    out += lax.dot_general((q * jnp.exp(gcum)).astype(BF16), st.astype(BF16), _NT,
                           preferred_element_type=F32)
    k_dec = (k * jnp.exp(g_tot - gcum)).astype(BF16)
    st_ref[...] = st * jnp.exp(g_tot) + lax.dot_general(v, k_dec, _TN, preferred_element_type=F32)
    return out


def _hg_kernel(*refs, seq, ctx_len, need_ctx):
    (ql_ref, il_ref, gl_ref, ffl_ref, fbl_ref, qc_ref, ic_ref, gc_ref, ffc_ref, fbc_ref,
     llb_ref, l1m_ref, oml_ref, ng_ref) = refs[:14]
    if need_ctx:
        yl_ref, yc_ref = refs[14:16]
        scratch = refs[16:]
    else:
        yl_ref = refs[14]
        yc_ref = None
        scratch = refs[15:]
    ol_ref, oc_ref, st_ref = scratch
    chunk = HG_CHUNK
    loglb = llb_ref[0]
    log1mlb = l1m_ref[0]
    one_m_lb = oml_ref[0]
    norm_g = ng_ref[0]
    ti = lax.broadcasted_iota(jnp.int32, (chunk, chunk), 0)
    si = lax.broadcasted_iota(jnp.int32, (chunk, chunk), 1)

    def run(q_s, i_s, g_s, z_s, o_s, y_s, n_chunks, rev, last):
        tri = ((si >= ti) if rev else (si <= ti)).astype(F32)
        masks = []
        for n, _ in _hg_level_tables(chunk, rev):
            shift = (2 * n).bit_length() - 1
            same = jnp.right_shift(ti, shift) == jnp.right_shift(si, shift)
            t_late = jnp.bitwise_and(ti, 2 * n - 1) >= n
            s_late = jnp.bitwise_and(si, 2 * n - 1) >= n
            if rev:
                masks.append(same & jnp.logical_not(t_late) & s_late)
            else:
                masks.append(same & t_late & jnp.logical_not(s_late))

        def body(ci, carry):
            c = (n_chunks - 1 - ci) if rev else ci
            rows = pl.ds(pl.multiple_of(c * chunk, chunk), chunk)
            o = _hg_chunk(q_s[rows, :], z_s[rows, :], i_s[rows, :], loglb, log1mlb, one_m_lb,
                          st_ref, masks, tri, rev)
            if not last:
                o_s[rows, :] = o
            elif y_s is not None:
                y = _head_rms(o_s[rows, :] + o, norm_g) * _silu(g_s[rows, :].astype(F32))
                y_s[rows, :] = y.astype(y_s.dtype)
            return carry

        lax.fori_loop(0, n_chunks, body, 0)

    for rev in (False, True):
        st_ref[...] = jnp.zeros_like(st_ref)
        run(qc_ref, ic_ref, gc_ref, fbc_ref if rev else ffc_ref, oc_ref, yc_ref,
            ctx_len // chunk, rev, rev)
        run(ql_ref, il_ref, gl_ref, fbl_ref if rev else ffl_ref, ol_ref, yl_ref,
            seq // chunk, rev, rev)


def _hgrn2(proj_h, proj_f, lb, norm_g, n_batch, seq, ctx_len, need_ctx):
    w = HG_DK
    cb0 = n_batch * seq // ctx_len
    lat = lambda off: pl.BlockSpec((seq, w), lambda b, h: (b, off // w + h))
    ctx = lambda off: pl.BlockSpec((ctx_len, w), lambda b, h: (cb0 + b, off // w + h))
    vec = pl.BlockSpec((1, 1, w), lambda b, h: (h, 0, 0))
    lb3 = lb.reshape(HG_HEADS, 1, w)
    in_specs = [lat(H_HG_Q), lat(H_HG_I), lat(H_HG_G), lat(F_HG_FF), lat(F_HG_FB),
                ctx(H_HG_Q), ctx(H_HG_I), ctx(H_HG_G), ctx(F_HG_FF), ctx(F_HG_FB),
                vec, vec, vec, vec]
    out_specs = [pl.BlockSpec((seq, w), lambda b, h: (b, h))]
    out_shape = [jax.ShapeDtypeStruct((n_batch * seq, HG_W), BF16)]
    if need_ctx:
        out_specs.append(pl.BlockSpec((ctx_len, w), lambda b, h: (b, h)))
        out_shape.append(jax.ShapeDtypeStruct((n_batch * ctx_len, HG_W), BF16))
    outs = pl.pallas_call(
        functools.partial(_hg_kernel, seq=seq, ctx_len=ctx_len, need_ctx=need_ctx),
        grid=(n_batch, HG_HEADS),
        in_specs=in_specs,
        out_specs=out_specs,
        out_shape=out_shape,
        scratch_shapes=[pltpu.VMEM((seq, w), F32), pltpu.VMEM((ctx_len, w), F32),
                        pltpu.VMEM((w, w), F32)],
        compiler_params=_params(2),
        name="hgrn2",
    )(proj_h, proj_h, proj_h, proj_f, proj_f, proj_h, proj_h, proj_h, proj_f, proj_f,
      jnp.log(lb3), jnp.log1p(-lb3), 1.0 - lb3, norm_g.reshape(HG_HEADS, 1, w))
    return outs if need_ctx else (outs[0], None)


def _router_kernel(x_ref, g_ref, sh_ref, sc_ref, wr_ref, h_ref, idx_ref, wt_ref):
    x = x_ref[...]
    y = x * lax.rsqrt(jnp.mean(x * x, axis=-1, keepdims=True) + EPS) * g_ref[...]
    h = y * (1.0 + sc_ref[0]) + sh_ref[0]
    h_ref[...] = h.astype(h_ref.dtype)
    logits = jnp.dot(h, wr_ref[...], precision=lax.Precision.HIGHEST, preferred_element_type=F32)
    lane = lax.broadcasted_iota(jnp.int32, logits.shape, 1).astype(F32)
    lg = jnp.where(lane < N_EXPERTS, logits, -jnp.inf)
    m1 = jnp.max(lg, axis=-1, keepdims=True)
    i1 = jnp.min(jnp.where(lg == m1, lane, float(LANE)), axis=-1, keepdims=True)
    lg2 = jnp.where(lane == i1, -jnp.inf, lg)
    m2 = jnp.max(lg2, axis=-1, keepdims=True)
    i2 = jnp.min(jnp.where(lg2 == m2, lane, float(LANE)), axis=-1, keepdims=True)
    e = jnp.exp(m2 - m1)
    w1 = 1.0 / (1.0 + e)
    w2 = e / (1.0 + e)
    idx_ref[...] = jnp.where(lane == 0, i1, jnp.where(lane == 1, i2, 0.0)).astype(jnp.int32)
    wt_ref[...] = jnp.where(lane == 0, w1, jnp.where(lane == 1, w2, 0.0))


def _router(x, g, shift, scale, w_router, n_rows, seq, n_batch):
    d = x.shape[1]
    tm = 256
    tpb = seq // tm
    grp = lambda i: (_group_of(i, tpb, n_batch), 0, 0)
    wr = jnp.pad(w_router, ((0, 0), (0, LANE - N_EXPERTS)))
    return pl.pallas_call(
        _router_kernel,
        grid=(n_rows // tm,),
        in_specs=[pl.BlockSpec((tm, d), lambda i: (i, 0)),
                  pl.BlockSpec((1, d), lambda i: (0, 0)),
                  pl.BlockSpec((1, 1, d), grp),
                  pl.BlockSpec((1, 1, d), grp),
                  pl.BlockSpec((d, LANE), lambda i: (0, 0))],
        out_specs=[pl.BlockSpec((tm, d), lambda i: (i, 0)),
                   pl.BlockSpec((tm, LANE), lambda i: (i, 0)),
                   pl.BlockSpec((tm, LANE), lambda i: (i, 0))],
        out_shape=[jax.ShapeDtypeStruct((n_rows, d), BF16),
                   jax.ShapeDtypeStruct((n_rows, LANE), jnp.int32),
                   jax.ShapeDtypeStruct((n_rows, LANE), F32)],
        compiler_params=_params(1),
        name="router",
    )(x, g.reshape(1, d), shift, scale, wr)


def _new_expert(te_ref, i):
    return jnp.logical_or(i == 0, te_ref[i] != te_ref[jnp.maximum(i - 1, 0)])


def _moe_up_kernel(te_ref, nu_ref, a_ref, wg_ref, wu_ref, o_ref, wgb_ref, wub_ref):
    i = pl.program_id(1)

    @pl.when(_new_expert(te_ref, i))
    def _():
        wgb_ref[...] = wg_ref[...].astype(BF16)
        wub_ref[...] = wu_ref[...].astype(BF16)

    @pl.when(i < nu_ref[0])
    def _():
        a = a_ref[...]
        g = jnp.dot(a, wgb_ref[...], preferred_element_type=F32)
        u = jnp.dot(a, wub_ref[...], preferred_element_type=F32)
        o_ref[...] = (_silu(g) * u).astype(o_ref.dtype)

    @pl.when(i >= nu_ref[0])
    def _():
        o_ref[...] = jnp.zeros_like(o_ref)


def _moe_down_kernel(te_ref, nu_ref, a_ref, w_ref, o_ref, wb_ref):
    i = pl.program_id(1)

    @pl.when(_new_expert(te_ref, i))
    def _():
        wb_ref[...] = w_ref[...].astype(BF16)

    @pl.when(i < nu_ref[0])
    def _():
        o_ref[...] = jnp.dot(a_ref[...], wb_ref[...], preferred_element_type=F32).astype(o_ref.dtype)

    @pl.when(i >= nu_ref[0])
    def _():
        o_ref[...] = jnp.zeros_like(o_ref)


def _moe_grouped(a, weights, tile_expert, n_used, kernel, out_dtype, name):
    n_rows, k = a.shape
    n = weights[0].shape[2]
    tm, tn = MOE_TILE, COL_TILE
    a_map = lambda j, i, te, nu: (jnp.minimum(i, nu[0] - 1), 0)
    w_spec = pl.BlockSpec((None, k, tn), lambda j, i, te, nu: (te[i], 0, j))
    return pl.pallas_call(
        kernel,
        grid_spec=pltpu.PrefetchScalarGridSpec(
            num_scalar_prefetch=2,
            grid=(n // tn, n_rows // tm),
            in_specs=[pl.BlockSpec((tm, k), a_map)] + [w_spec] * len(weights),
            out_specs=pl.BlockSpec((tm, tn), lambda j, i, te, nu: (i, j)),
            scratch_shapes=[pltpu.VMEM((k, tn), BF16)] * len(weights)),
        out_shape=jax.ShapeDtypeStruct((n_rows, n), out_dtype),
        compiler_params=_params(2),
        name=name,
    )(tile_expert, n_used, a, *weights)


def _combine_kernel(x_ref, ya_ref, yb_ref, wt_ref, g_ref, o_ref):
    wt = wt_ref[...]
    y = wt[:, 0:1] * ya_ref[...].astype(F32) + wt[:, 1:2] * yb_ref[...].astype(F32)
    o_ref[...] = x_ref[...] + g_ref[0] * y


def _moe_combine(x, ya, yb, wt, gate, n_rows, seq, n_batch):
    d = x.shape[1]
    tm = 256
    tpb = seq // tm
    row = pl.BlockSpec((tm, d), lambda i: (i, 0))
    return pl.pallas_call(
        _combine_kernel,
        grid=(n_rows // tm,),
        in_specs=[row, row, row,
                  pl.BlockSpec((tm, LANE), lambda i: (i, 0)),
                  pl.BlockSpec((1, 1, d), lambda i: (_group_of(i, tpb, n_batch), 0, 0))],
        out_specs=row,
        out_shape=jax.ShapeDtypeStruct((n_rows, d), F32),
        compiler_params=_params(1),
        name="moe_combine",
    )(x, ya, yb, wt, gate)


def _moe_ffn(x, g, shift, scale, gate, w_router, wg, wu, wd, n_rows, seq, n_batch):
    h, idx, wt = _router(x, g, shift, scale, w_router, n_rows, seq, n_batch)
    tm = MOE_TILE
    n_assign = 2 * n_rows
    n_tiles = n_assign // tm + N_EXPERTS
    expert = jnp.concatenate([idx[:, 0], idx[:, 1]])
    token = jnp.concatenate([jnp.arange(n_rows, dtype=jnp.int32)] * 2)
    one_hot = (expert[:, None] == jnp.arange(N_EXPERTS, dtype=jnp.int32)[None, :]).astype(jnp.int32)
    rank = jnp.sum((jnp.cumsum(one_hot, axis=0) - 1) * one_hot, axis=1)
    counts = jnp.sum(one_hot, axis=0)
    tiles = (counts + tm - 1) // tm
    tile_end = jnp.cumsum(tiles)
    tile_start = tile_end - tiles
    dest = jnp.sum(tile_start[None, :] * one_hot, axis=1) * tm + rank
    src_token = jnp.zeros((n_tiles * tm,), jnp.int32).at[dest].set(token)
    n_used = tile_end[-1:].astype(jnp.int32)
    tile_id = jnp.minimum(jnp.arange(n_tiles, dtype=jnp.int32), n_used[0] - 1)
    tile_expert = jnp.sum((tile_end[None, :] <= tile_id[:, None]).astype(jnp.int32), axis=1)

    a_sorted = jnp.take(h, src_token, axis=0)
    u = _moe_grouped(a_sorted, (wg, wu), tile_expert, n_used, _moe_up_kernel, BF16, "moe_up")
    y = _moe_grouped(u, (wd,), tile_expert, n_used, _moe_down_kernel, BF16, "moe_down")
    ya = jnp.take(y, dest[:n_rows], axis=0)
    yb = jnp.take(y, dest[n_rows:], axis=0)
    return _moe_combine(x, ya, yb, wt, gate, n_rows, seq, n_batch)


def _rope_tables(n_batch, seq, n_ctx_rows):
    t = jnp.arange(seq)
    pos = jnp.stack([t // GRID_W, t % GRID_W], axis=-1).astype(F32)
    inv_freq = jnp.exp(-jnp.log(ROPE_BASE) * jnp.arange(0, ROPE_AXIS_DIM, 2, dtype=F32) / ROPE_AXIS_DIM)
    ang = pos[..., None] * inv_freq
    cos, sin = jnp.cos(ang), jnp.sin(ang)
    cos = jnp.concatenate([cos[:, 0], cos[:, 0], cos[:, 1], cos[:, 1]], axis=-1)
    sin = jnp.concatenate([-sin[:, 0], sin[:, 0], -sin[:, 1], sin[:, 1]], axis=-1)
    cos = jnp.concatenate([jnp.tile(cos, (n_batch, 1)), jnp.ones((n_ctx_rows, HEAD_DIM), F32)], axis=0)
    sin = jnp.concatenate([jnp.tile(sin, (n_batch, 1)), jnp.zeros((n_ctx_rows, HEAD_DIM), F32)], axis=0)
    return cos, sin


def _split_w_in(w):
    w_h = jnp.concatenate([w[:, :_O_ML_GATES], w[:, _O_HG_Q:_O_HG_FF], w[:, _O_HG_I:]], axis=1)
    w_f = jnp.concatenate([w[:, _O_HG_FF:_O_HG_I], w[:, _O_ML_GATES:_O_HG_Q],
                           jnp.zeros((w.shape[0], F_WIDTH - F_ML_G - 4 * ML_HEADS), w.dtype)], axis=1)
    return w_h.astype(BF16), w_f.astype(BF16)


def kernel(x, c, ctx, c_ctx, w_mod, b_mod, norm1_g, norm2_g, w_in, attn_q_norm_g, attn_k_norm_g, ml_conv_w,
           ml_gate_b, ml_norm_g, hg_lb_logits, hg_norm_g, w_branch, w_out, ffn_w_gate, ffn_w_up, ffn_w_down,
           moe_w_router, moe_w_gate, moe_w_up, moe_w_down):
    n_batch, seq, d = x.shape
    ctx_len = ctx.shape[1]
    depth = w_mod.shape[0]
    m_lat = n_batch * seq
    m_ctx = n_batch * ctx_len
    m_all = m_lat + m_ctx
    assert d == D_MODEL and w_in.shape[2] == _N_IN
    assert seq % ROW_TILE == 0 and m_ctx % ROW_TILE == 0 and seq % ctx_len == 0
    assert seq % ML_CHUNK == 0 and ctx_len % ML_CHUNK == 0 and ctx_len % HG_CHUNK == 0

    xa = jnp.concatenate([x.reshape(m_lat, d), ctx.reshape(m_ctx, d)], axis=0)
    cs = jnp.concatenate([c, c_ctx[None], jnp.zeros((8 - n_batch - 1, d), F32)], axis=0)
    mod = _modulation(cs, w_mod, b_mod)
    cos, sin = _rope_tables(n_batch, seq, m_ctx)
    lb_all = jnp.cumsum(jax.nn.softmax(hg_lb_logits.astype(F32), axis=0), axis=0)
    lb_all = lb_all - lb_all[:1]

    for l in range(depth):
        need_ctx = l < depth - 1
        n_rows = m_all if need_ctx else m_lat
        mods = [mod[l, :n_batch + 1, i * d:(i + 1) * d].reshape(n_batch + 1, 1, d) for i in range(N_MOD)]
        w_h, w_f = _split_w_in(w_in[l])

        h = _norm_mod(xa, norm1_g[l], mods[0], mods[1], m_all, seq, n_batch)
        proj_h = _matmul(h, w_h, BF16, COL_TILE)
        proj_f = _matmul(h, w_f, F32, 256)

        q_rot, k_rot = _att_prep(proj_h, cos, sin, attn_q_norm_g[l], attn_k_norm_g[l])
        ya = _attention(q_rot, k_rot, proj_h, n_batch, seq, ctx_len, True)

        gates = proj_f[:, F_ML_G:F_ML_G + 4 * ML_HEADS].reshape(m_all, 4, ML_HEADS)
        gates_c = gates.transpose(2, 0, 1)
        gates_r = gates.transpose(2, 1, 0)
        ym, ym_c = _mlstm(proj_h, gates_c, gates_r, ml_conv_w[l], ml_gate_b[l], ml_norm_g[l],
                          n_batch, seq, ctx_len, need_ctx)
        yh, yh_c = _hgrn2(proj_h, proj_f, lb_all[l], hg_norm_g[l], n_batch, seq, ctx_len, need_ctx)
        if need_ctx:
            ya_c = _attention(q_rot, k_rot, proj_h, n_batch, seq, ctx_len, False)
            ya = jnp.concatenate([ya, ya_c], axis=0)
            ym = jnp.concatenate([ym, ym_c], axis=0)
            yh = jnp.concatenate([yh, yh_c], axis=0)

        merged = _branch_merge(ya, ym, yh, w_branch[l], proj_h, n_rows)
        xa = _matmul_residual(merged, w_out[l], xa, mods[2], n_rows, seq, n_batch, "proj_out")

        if l % 2 == 0:
            h2 = _norm_mod(xa, norm2_g[l], mods[3], mods[4], n_rows, seq, n_batch)
            u = _glu(h2, ffn_w_gate[l // 2], ffn_w_up[l // 2])
            xa = _matmul_residual(u, ffn_w_down[l // 2], xa, mods[5], n_rows, seq, n_batch, "ffn_down")
        else:
            xa = _moe_ffn(xa, norm2_g[l], mods[3], mods[4], mods[5], moe_w_router[l // 2],
                          moe_w_gate[l // 2], moe_w_up[l // 2], moe_w_down[l // 2], n_rows, seq, n_batch)
    return xa[:m_lat].reshape(n_batch, seq, d)
```

```python
import functools

import jax
import jax.numpy as jnp
from jax import lax
from jax.experimental import pallas as pl
from jax.experimental.pallas import tpu as pltpu

F32 = jnp.float32
BF16 = jnp.bfloat16

D_MODEL = 2048
GRID_W = 64
N_MOD = 6
ATT_HEADS = 8
ATT_KV_HEADS = 2
HEAD_DIM = 128
ATT_W = ATT_HEADS * HEAD_DIM
KV_W = ATT_KV_HEADS * HEAD_DIM
ROPE_AXIS_DIM = HEAD_DIM // 2
ROPE_BASE = 10000.0
ML_HEADS = 4
ML_DK = 256
ML_W = ML_HEADS * ML_DK
HG_HEADS = 8
HG_DK = 128
HG_W = HG_HEADS * HG_DK
D_FF = 5632
N_EXPERTS = 8
EPS = 1e-6
NEG_INIT = -1e30
LOG2E = 1.4426950408889634

_O_ATT_Q = 0
_O_ML_GATES = ATT_W + 2 * KV_W + 4 * ML_W
_O_HG_Q = _O_ML_GATES + 4 * ML_HEADS
_O_HG_FF = _O_HG_Q + HG_W
_O_HG_I = _O_HG_FF + 2 * HG_W
_O_MERGE = _O_HG_I + 2 * HG_W
_N_IN = _O_MERGE + 3 * D_MODEL

H_ATT_Q = 0
H_ATT_K = 1024
H_ATT_V = 1280
H_ML_Q = 1536
H_ML_K = 2560
H_ML_V = 3584
H_ML_O = 4608
H_HG_Q = 5632
H_HG_I = 6656
H_HG_G = 7680
H_MERGE = 8704
H_WIDTH = 14848
F_HG_FF = 0
F_HG_FB = 1024
F_ML_G = 2048
F_WIDTH = 2304

LANE = 128
ROW_TILE = 512
COL_TILE = 512
PROJ_ROW_TILES = 4
VMEM_LIMIT = 56 * 1024 * 1024

ML_CHUNK = 256
HG_CHUNK = 64
HG_LEAF = 8
HG_UNROLL = 2
ATT_TQ = 256
MOE_TILE = 512

_NT = (((1,), (1,)), ((), ()))
_TN = (((0,), (0,)), ((), ()))


def _params(n_axes):
    return pltpu.CompilerParams(dimension_semantics=("arbitrary",) * n_axes,
                                vmem_limit_bytes=VMEM_LIMIT)


def _silu(x):
    return x * jax.nn.sigmoid(x)


def _log_sigmoid(x):
    return jnp.minimum(x, 0.0) - jnp.log1p(jnp.exp(-jnp.abs(x)))


def _mod_kernel(c_ref, w_ref, b_ref, o_ref):
    a = _silu(c_ref[...])
    o_ref[...] = jnp.dot(a, w_ref[...], precision=lax.Precision.HIGHEST,
                         preferred_element_type=F32) + b_ref[...]


def _modulation(cs, w_mod, b_mod):
    depth, d, n = w_mod.shape
    tn = 1024
    return pl.pallas_call(
        _mod_kernel,
        grid=(depth, n // tn),
        in_specs=[pl.BlockSpec((8, d), lambda l, j: (0, 0)),
                  pl.BlockSpec((None, d, tn), lambda l, j: (l, 0, j)),
                  pl.BlockSpec((None, 1, tn), lambda l, j: (l, 0, j))],
        out_specs=pl.BlockSpec((None, 8, tn), lambda l, j: (l, 0, j)),
        out_shape=jax.ShapeDtypeStruct((depth, 8, n), F32),
        compiler_params=_params(2),
        name="modulation",
    )(cs, w_mod, b_mod.reshape(depth, 1, n))


def _norm_kernel(x_ref, g_ref, sh_ref, sc_ref, o_ref):
    x = x_ref[...]
    y = x * lax.rsqrt(jnp.mean(x * x, axis=-1, keepdims=True) + EPS)
    y = y * g_ref[...]
    o_ref[...] = (y * (1.0 + sc_ref[0]) + sh_ref[0]).astype(o_ref.dtype)


def _group_of(i, tiles_per_batch, n_batch):
    return jnp.minimum(i // tiles_per_batch, n_batch)


def _norm_mod(x, g, shift, scale, n_rows, seq, n_batch):
    d = x.shape[1]
    tm = 256
    tpb = seq // tm
    grp = lambda i: (_group_of(i, tpb, n_batch), 0, 0)
    return pl.pallas_call(
        _norm_kernel,
        grid=(n_rows // tm,),
        in_specs=[pl.BlockSpec((tm, d), lambda i: (i, 0)),
                  pl.BlockSpec((1, d), lambda i: (0, 0)),
                  pl.BlockSpec((1, 1, d), grp),
                  pl.BlockSpec((1, 1, d), grp)],
        out_specs=pl.BlockSpec((tm, d), lambda i: (i, 0)),
        out_shape=jax.ShapeDtypeStruct((n_rows, d), BF16),
        compiler_params=_params(1),
        name="norm_mod",
    )(x, g.reshape(1, d), shift, scale)


def _mm_kernel(a_ref, w_ref, o_ref):
    o_ref[...] = jnp.dot(a_ref[...], w_ref[...], preferred_element_type=F32).astype(o_ref.dtype)


def _matmul(a, w, out_dtype, tn):
    m, k = a.shape
    n = w.shape[1]
    tm = m // PROJ_ROW_TILES
    return pl.pallas_call(
        _mm_kernel,
        grid=(n // tn, m // tm),
        in_specs=[pl.BlockSpec((tm, k), lambda j, i: (i, 0)),
                  pl.BlockSpec((k, tn), lambda j, i: (0, j))],
        out_specs=pl.BlockSpec((tm, tn), lambda j, i: (i, j)),
        out_shape=jax.ShapeDtypeStruct((m, n), out_dtype),
        compiler_params=_params(2),
        name="proj_in",
    )(a, w)


def _mm_res_kernel(a_ref, w_ref, r_ref, g_ref, o_ref, wb_ref):
    @pl.when(pl.program_id(1) == 0)
    def _():
        wb_ref[...] = w_ref[...].astype(BF16)

    y = jnp.dot(a_ref[...], wb_ref[...], preferred_element_type=F32)
    o_ref[...] = r_ref[...] + g_ref[0] * y


def _matmul_residual(a, w, res, gate, n_rows, seq, n_batch, name):
    k = a.shape[1]
    n = w.shape[1]
    tm, tn = ROW_TILE, COL_TILE
    tpb = seq // tm
    return pl.pallas_call(
        _mm_res_kernel,
        grid=(n // tn, n_rows // tm),
        in_specs=[pl.BlockSpec((tm, k), lambda j, i: (i, 0)),
                  pl.BlockSpec((k, tn), lambda j, i: (0, j)),
                  pl.BlockSpec((tm, tn), lambda j, i: (i, j)),
                  pl.BlockSpec((1, 1, tn), lambda j, i: (_group_of(i, tpb, n_batch), 0, j))],
        out_specs=pl.BlockSpec((tm, tn), lambda j, i: (i, j)),
        out_shape=jax.ShapeDtypeStruct((n_rows, n), F32),
        scratch_shapes=[pltpu.VMEM((k, tn), BF16)],
        compiler_params=_params(2),
        name=name,
    )(a, w, res, gate)


def _glu_kernel(a_ref, wg_ref, wu_ref, o_ref, wgb_ref, wub_ref):
    @pl.when(pl.program_id(1) == 0)
    def _():
        wgb_ref[...] = wg_ref[...].astype(BF16)
        wub_ref[...] = wu_ref[...].astype(BF16)

    a = a_ref[...]
    g = jnp.dot(a, wgb_ref[...], preferred_element_type=F32)
    u = jnp.dot(a, wub_ref[...], preferred_element_type=F32)
    o_ref[...] = (_silu(g) * u).astype(o_ref.dtype)


def _glu(a, wg, wu):
    m, k = a.shape
    n = wg.shape[1]
    tm, tn = ROW_TILE, COL_TILE
    return pl.pallas_call(
        _glu_kernel,
        grid=(n // tn, m // tm),
        in_specs=[pl.BlockSpec((tm, k), lambda j, i: (i, 0)),
                  pl.BlockSpec((k, tn), lambda j, i: (0, j)),
                  pl.BlockSpec((k, tn), lambda j, i: (0, j))],
        out_specs=pl.BlockSpec((tm, tn), lambda j, i: (i, j)),
        out_shape=jax.ShapeDtypeStruct((m, n), BF16),
        scratch_shapes=[pltpu.VMEM((k, tn), BF16), pltpu.VMEM((k, tn), BF16)],
        compiler_params=_params(2),
        name="ffn_up",
    )(a, wg, wu)


def _merge_kernel(*refs, n_lat_tiles, with_ctx):
    if with_ctx:
        lat_refs, ctx_refs, rest = refs[0:3], refs[3:6], refs[6:]
    else:
        lat_refs, ctx_refs, rest = refs[0:3], None, refs[3:]
    w_ref, g0_ref, g1_ref, g2_ref, o_ref, wb_ref = rest

    @pl.when(pl.program_id(1) == 0)
    def _():
        wb_ref[...] = w_ref[...].astype(BF16)

    is_ctx = pl.program_id(1) >= n_lat_tiles
    acc = None
    for b, g_ref in enumerate((g0_ref, g1_ref, g2_ref)):
        y = lat_refs[b][...]
        if with_ctx:
            y = jnp.where(is_ctx, ctx_refs[b][...], y)
        term = jax.nn.sigmoid(g_ref[...].astype(F32)) * jnp.dot(y, wb_ref[b], preferred_element_type=F32)
        acc = term if acc is None else acc + term
    o_ref[...] = acc.astype(o_ref.dtype)


def _branch_merge(ys_lat, ys_ctx, w_branch, proj_h, n_rows):
    k = ys_lat[0].shape[1]
    n = w_branch.shape[2]
    tm, tn = ROW_TILE, COL_TILE
    n_lat_tiles = ys_lat[0].shape[0] // tm
    with_ctx = ys_ctx is not None
    gate_spec = lambda b: pl.BlockSpec((tm, tn), lambda j, i: (i, (H_MERGE + b * n) // tn + j))
    lat_spec = pl.BlockSpec((tm, k), lambda j, i: (jnp.minimum(i, n_lat_tiles - 1), 0))
    ctx_spec = pl.BlockSpec((tm, k), lambda j, i: (jnp.maximum(i - n_lat_tiles, 0), 0))
    in_specs = [lat_spec] * 3 + ([ctx_spec] * 3 if with_ctx else [])
    in_specs += [pl.BlockSpec((3, k, tn), lambda j, i: (0, 0, j)), gate_spec(0), gate_spec(1), gate_spec(2)]
    args = list(ys_lat) + (list(ys_ctx) if with_ctx else [])
    return pl.pallas_call(
        functools.partial(_merge_kernel, n_lat_tiles=n_lat_tiles, with_ctx=with_ctx),
        grid=(n // tn, n_rows // tm),
        in_specs=in_specs,
        out_specs=pl.BlockSpec((tm, tn), lambda j, i: (i, j)),
        out_shape=jax.ShapeDtypeStruct((n_rows, n), BF16),
        scratch_shapes=[pltpu.VMEM((3, k, tn), BF16)],
        compiler_params=_params(2),
        name="branch_merge",
    )(*args, w_branch, proj_h, proj_h, proj_h)


def _head_rms(x, g):
    return x * lax.rsqrt(jnp.mean(x * x, axis=-1, keepdims=True) + EPS) * g


def _rope(x, cos, sin, low_half):
    partner = jnp.where(low_half, pltpu.roll(x, 96, axis=1), pltpu.roll(x, 32, axis=1))
    return x * cos + partner * sin


def _att_prep_kernel(q_ref, k_ref, cos_ref, sin_ref, qg_ref, kg_ref, qo_ref, ko_ref):
    cos = cos_ref[...]
    sin = sin_ref[...]
    lane = lax.broadcasted_iota(jnp.int32, cos.shape, 1)
    low_half = jnp.bitwise_and(lane, 63) < 32
    scale = HEAD_DIM ** -0.5
    for h in range(ATT_HEADS):
        sl = slice(h * HEAD_DIM, (h + 1) * HEAD_DIM)
        x = _head_rms(q_ref[:, sl].astype(F32), qg_ref[...])
        qo_ref[:, sl] = (_rope(x, cos, sin, low_half) * scale).astype(qo_ref.dtype)
    for h in range(ATT_KV_HEADS):
        sl = slice(h * HEAD_DIM, (h + 1) * HEAD_DIM)
        x = _head_rms(k_ref[:, sl].astype(F32), kg_ref[...])
        ko_ref[:, sl] = _rope(x, cos, sin, low_half).astype(ko_ref.dtype)


def _att_prep(proj_h, cos, sin, qg, kg):
    m = proj_h.shape[0]
    tm = ROW_TILE
    return pl.pallas_call(
        _att_prep_kernel,
        grid=(m // tm,),
        in_specs=[pl.BlockSpec((tm, ATT_W), lambda i: (i, H_ATT_Q // ATT_W)),
                  pl.BlockSpec((tm, KV_W), lambda i: (i, H_ATT_K // KV_W)),
                  pl.BlockSpec((tm, HEAD_DIM), lambda i: (i, 0)),
                  pl.BlockSpec((tm, HEAD_DIM), lambda i: (i, 0)),
                  pl.BlockSpec((1, HEAD_DIM), lambda i: (0, 0)),
                  pl.BlockSpec((1, HEAD_DIM), lambda i: (0, 0))],
        out_specs=[pl.BlockSpec((tm, ATT_W), lambda i: (i, 0)),
                   pl.BlockSpec((tm, KV_W), lambda i: (i, 0))],
        out_shape=[jax.ShapeDtypeStruct((m, ATT_W), BF16),
                   jax.ShapeDtypeStruct((m, KV_W), BF16)],
        compiler_params=_params(1),
        name="att_prep",
    )(proj_h, proj_h, cos, sin, qg.reshape(1, HEAD_DIM), kg.reshape(1, HEAD_DIM))


def _att_kernel(*refs, with_latent):
    if with_latent:
        q_ref, kc_ref, vc_ref, kl_ref, vl_ref, o_ref = refs
    else:
        q_ref, kc_ref, vc_ref, o_ref = refs
    grp = ATT_HEADS // ATT_KV_HEADS
    for g in range(grp):
        sl = slice(g * HEAD_DIM, (g + 1) * HEAD_DIM)
        q = q_ref[:, sl]
        sc = lax.dot_general(q, kc_ref[...], _NT, preferred_element_type=F32)
        m = jnp.max(sc, axis=-1, keepdims=True)
        if with_latent:
            sl_ = lax.dot_general(q, kl_ref[...], _NT, preferred_element_type=F32)
            m = jnp.maximum(m, jnp.max(sl_, axis=-1, keepdims=True))
        pc = jnp.exp(sc - m)
        den = jnp.sum(pc, axis=-1, keepdims=True)
        acc = jnp.dot(pc.astype(BF16), vc_ref[...], preferred_element_type=F32)
        if with_latent:
            pl_ = jnp.exp(sl_ - m)
            den += jnp.sum(pl_, axis=-1, keepdims=True)
            acc += jnp.dot(pl_.astype(BF16), vl_ref[...], preferred_element_type=F32)
        o_ref[:, sl] = (acc / den).astype(o_ref.dtype)


def _attention(q_rot, k_rot, proj_h, n_batch, seq, ctx_len, latent_queries):
    grp_w = ATT_W // ATT_KV_HEADS
    ctx_blk0 = n_batch * seq // ctx_len
    v_col = H_ATT_V // HEAD_DIM
    if latent_queries:
        tq = ATT_TQ
        nq = seq // tq
        q_map = lambda b, k, i: (b * nq + i, k)
        n_out = n_batch * seq
    else:
        tq = ctx_len
        nq = 1
        q_map = lambda b, k, i: (ctx_blk0 + b, k)
        n_out = n_batch * ctx_len
    in_specs = [pl.BlockSpec((tq, grp_w), q_map),
                pl.BlockSpec((ctx_len, HEAD_DIM), lambda b, k, i: (ctx_blk0 + b, k)),
                pl.BlockSpec((ctx_len, HEAD_DIM), lambda b, k, i: (ctx_blk0 + b, v_col + k))]
    args = [q_rot, k_rot, proj_h]
    if latent_queries:
        in_specs += [pl.BlockSpec((seq, HEAD_DIM), lambda b, k, i: (b, k)),
                     pl.BlockSpec((seq, HEAD_DIM), lambda b, k, i: (b, v_col + k))]
        args += [k_rot, proj_h]
    return pl.pallas_call(
        functools.partial(_att_kernel, with_latent=latent_queries),
        grid=(n_batch, ATT_KV_HEADS, nq),
        in_specs=in_specs,
        out_specs=pl.BlockSpec((tq, grp_w), lambda b, k, i: (b * nq + i, k)),
        out_shape=jax.ShapeDtypeStruct((n_out, ATT_W), BF16),
        compiler_params=_params(3),
        name="attention_lat" if latent_queries else "attention_ctx",
    )(*args)


def _conv3(x_ref, w, c, n_chunks, chunk):
    r0 = pl.multiple_of(c * chunk, chunk)
    x = x_ref[pl.ds(r0, chunk), :].astype(F32)
    prev_blk = x_ref[pl.ds(pl.multiple_of(jnp.maximum(r0 - 16, 0), 16), 16), :].astype(F32)
    next_blk = x_ref[pl.ds(pl.multiple_of(jnp.minimum(r0 + chunk, (n_chunks - 1) * chunk), 16), 16), :].astype(F32)
    prev_row = jnp.where(c > 0, prev_blk[15:16, :], 0.0)
    next_row = jnp.where(c < n_chunks - 1, next_blk[0:1, :], 0.0)
    row = lax.broadcasted_iota(jnp.int32, x.shape, 0)
    x_m1 = jnp.where(row == 0, prev_row, pltpu.roll(x, 1, axis=0))
    x_p1 = jnp.where(row == chunk - 1, next_row, pltpu.roll(x, chunk - 1, axis=0))
    return x_m1 * w[0:1, :] + x * w[1:2, :] + x_p1 * w[2:3, :]


def _ml_chunk(q, k, v, lfr, lir, lfc, lic, c_ref, n_ref, m_st, rev):
    n_tok = q.shape[0]
    ti = lax.broadcasted_iota(jnp.int32, (n_tok, n_tok), 0)
    si = lax.broadcasted_iota(jnp.int32, (n_tok, n_tok), 1)
    causal = (si >= ti) if rev else (si <= ti)
    causal_t = (ti >= si) if rev else (ti <= si)
    fc_col = jnp.sum(jnp.where(causal, lfr, 0.0), axis=1, keepdims=True)
    fc_row = jnp.sum(jnp.where(causal_t, lfc, 0.0), axis=0, keepdims=True)
    f_tot = jnp.sum(lfr, axis=1, keepdims=True)
    dmat = jnp.where(causal, fc_col - fc_row + lir, -jnp.inf)
    inter = fc_col + m_st
    m_t = jnp.maximum(inter, jnp.max(dmat, axis=1, keepdims=True))
    w = jnp.exp(dmat - m_t)
    a = jnp.exp(inter - m_t)
    sc = lax.dot_general(q, k, _NT, preferred_element_type=F32) * w
    num = jnp.dot(sc.astype(BF16), v, preferred_element_type=F32)
    num += a * jnp.dot(q, c_ref[...].astype(BF16), preferred_element_type=F32)
    den = jnp.sum(sc, axis=1, keepdims=True)
    den += a * jnp.sum(q.astype(F32) * n_ref[...], axis=1, keepdims=True)
    h = num / jnp.maximum(jnp.abs(den), jnp.exp(-m_t))
    g_col = f_tot - fc_col + lic
    g_row = f_tot - fc_row + lir
    m_new = jnp.maximum(f_tot + m_st, jnp.max(g_row, axis=1, keepdims=True))
    decay = jnp.exp(f_tot + m_st - m_new)
    kw = k.astype(F32) * jnp.exp(g_col - m_new)
    c_ref[...] = decay * c_ref[...] + lax.dot_general(kw.astype(BF16), v, _TN, preferred_element_type=F32)
    n_ref[...] = decay * n_ref[...] + jnp.sum(kw, axis=0, keepdims=True)
    return h, m_new


def _ml_kernel(*refs, seq, ctx_len, need_ctx):
    (ql_ref, kl_ref, vl_ref, ol_ref, qc_ref, kc_ref, vc_ref, oc_ref,
     wq_ref, wk_ref, gcl_ref, grl_ref, gcc_ref, grc_ref, bc_ref, br_ref, ng_ref) = refs[:17]
    if need_ctx:
        yl_ref, yc_ref = refs[17:19]
        scratch = refs[19:]
    else:
        yl_ref = refs[17]
        yc_ref = None
        scratch = refs[18:]
    qsl_ref, ksl_ref, qsc_ref, ksc_ref, hl_ref, hc_ref, c_ref, n_ref = scratch
    chunk = ML_CHUNK
    k_scale = ML_DK ** -0.5
    wq = wq_ref[0]
    wk = wk_ref[0]

    def conv_pass(src_q, src_k, dst_q, dst_k, n_chunks):
        def body(c, carry):
            r0 = pl.multiple_of(c * chunk, chunk)
            dst_q[pl.ds(r0, chunk), :] = _conv3(src_q, wq, c, n_chunks, chunk).astype(BF16)
            dst_k[pl.ds(r0, chunk), :] = (_conv3(src_k, wk, c, n_chunks, chunk) * k_scale).astype(BF16)
            return carry
        lax.fori_loop(0, n_chunks, body, 0)

    conv_pass(qc_ref, kc_ref, qsc_ref, ksc_ref, ctx_len // chunk)
    conv_pass(ql_ref, kl_ref, qsl_ref, ksl_ref, seq // chunk)

    bias_c = bc_ref[0]
    bias_r = br_ref[0]
    norm_g = ng_ref[0]

    def run(q_s, k_s, v_s, o_s, gc_s, gr_s, h_s, y_s, n_chunks, m_st, rev, last):
        d = 1 if rev else 0

        def body(ci, m_st):
            c = (n_chunks - 1 - ci) if rev else ci
            r0 = pl.multiple_of(c * chunk, chunk)
            rows = pl.ds(r0, chunk)
            gc = gc_s[0, rows, :] + bias_c
            gr = gr_s[0, :, rows] + bias_r
            lic = gc[:, d:d + 1]
            lfc = _log_sigmoid(gc[:, 2 + d:3 + d])
            lir = gr[d:d + 1, :]
            lfr = _log_sigmoid(gr[2 + d:3 + d, :])
            h, m_new = _ml_chunk(q_s[rows, :], k_s[rows, :], v_s[rows, :], lfr, lir, lfc, lic,
                                 c_ref, n_ref, m_st, rev)
            if not last:
                h_s[rows, :] = h
            elif y_s is not None:
                hs = h_s[rows, :] + h
                y = _head_rms(hs, norm_g) * jax.nn.sigmoid(o_s[rows, :].astype(F32))
                y_s[rows, :] = y.astype(y_s.dtype)
            return m_new

        return lax.fori_loop(0, n_chunks, body, m_st)

    for rev in (False, True):
        c_ref[...] = jnp.zeros_like(c_ref)
        n_ref[...] = jnp.zeros_like(n_ref)
        m0 = jnp.full((1, 1), NEG_INIT, F32)
        m1 = run(qsc_ref, ksc_ref, vc_ref, oc_ref, gcc_ref, grc_ref, hc_ref, yc_ref,
                 ctx_len // chunk, m0, rev, rev)
        run(qsl_ref, ksl_ref, vl_ref, ol_ref, gcl_ref, grl_ref, hl_ref, yl_ref,
            seq // chunk, m1, rev, rev)


def _mlstm(proj_h, gates_c, gates_r, conv_w, gate_b, norm_g, n_batch, seq, ctx_len, need_ctx):
    w = ML_DK
    cb0 = n_batch * seq // ctx_len
    col = lambda off: off // w
    lat = lambda off: pl.BlockSpec((seq, w), lambda b, h: (b, col(off) + h))
    ctx = lambda off: pl.BlockSpec((ctx_len, w), lambda b, h: (cb0 + b, col(off) + h))
    conv = conv_w.reshape(3, 2 * ML_HEADS, w).transpose(1, 0, 2)
    gb = gate_b.reshape(4, ML_HEADS).T
    in_specs = [lat(H_ML_Q), lat(H_ML_K), lat(H_ML_V), lat(H_ML_O),
                ctx(H_ML_Q), ctx(H_ML_K), ctx(H_ML_V), ctx(H_ML_O),
                pl.BlockSpec((1, 3, w), lambda b, h: (h, 0, 0)),
                pl.BlockSpec((1, 3, w), lambda b, h: (ML_HEADS + h, 0, 0)),
                pl.BlockSpec((1, seq, 4), lambda b, h: (h, b, 0)),
                pl.BlockSpec((1, 4, seq), lambda b, h: (h, 0, b)),
                pl.BlockSpec((1, ctx_len, 4), lambda b, h: (h, cb0 + b, 0)),
                pl.BlockSpec((1, 4, ctx_len), lambda b, h: (h, 0, cb0 + b)),
                pl.BlockSpec((1, 1, 4), lambda b, h: (h, 0, 0)),
                pl.BlockSpec((1, 4, 1), lambda b, h: (h, 0, 0)),
                pl.BlockSpec((1, 1, w), lambda b, h: (h, 0, 0))]
    out_specs = [pl.BlockSpec((seq, w), lambda b, h: (b, h))]
    out_shape = [jax.ShapeDtypeStruct((n_batch * seq, ML_W), BF16)]
    if need_ctx:
        out_specs.append(pl.BlockSpec((ctx_len, w), lambda b, h: (b, h)))
        out_shape.append(jax.ShapeDtypeStruct((n_batch * ctx_len, ML_W), BF16))
    outs = pl.pallas_call(
        functools.partial(_ml_kernel, seq=seq, ctx_len=ctx_len, need_ctx=need_ctx),
        grid=(n_batch, ML_HEADS),
        in_specs=in_specs,
        out_specs=out_specs,
        out_shape=out_shape,
        scratch_shapes=[pltpu.VMEM((seq, w), BF16), pltpu.VMEM((seq, w), BF16),
                        pltpu.VMEM((ctx_len, w), BF16), pltpu.VMEM((ctx_len, w), BF16),
                        pltpu.VMEM((seq, w), F32), pltpu.VMEM((ctx_len, w), F32),
                        pltpu.VMEM((w, w), F32), pltpu.VMEM((1, w), F32)],
        compiler_params=_params(2),
        name="mlstm",
    )(proj_h, proj_h, proj_h, proj_h, proj_h, proj_h, proj_h, proj_h,
      conv, conv, gates_c, gates_r, gates_c, gates_r,
      gb.reshape(ML_HEADS, 1, 4), gb.reshape(ML_HEADS, 4, 1), norm_g.reshape(ML_HEADS, 1, w))
    return outs if need_ctx else (outs[0], None)


def _hg_level_tables(chunk, rev):
    levels = []
    n = HG_LEAF
    while 2 * n <= chunk:
        refs = []
        for start in range(0, chunk, 2 * n):
            refs.append((start + n - 1) if rev else (start + n))
        levels.append((n, refs))
        n *= 2
    return levels


def _hg_chunks(chains, loglb, log1mlb, one_m_lb):
    chunk = chains[0][1].shape[0]
    nb = chunk // HG_LEAF
    t_idx = lax.broadcasted_iota(jnp.int32, (1, HG_LEAF, 1), 1)

    qs, ks, lks, vfs, gcums = [], [], [], [], []
    for qraw, z, v, st_ref, masks, tri, rev in chains:
        e = jnp.exp(-jnp.abs(z))
        one_pe = 1.0 + e
        lsig = jnp.minimum(z, 0.0) - jnp.log(one_pe)
        a = log1mlb + lsig
        lf = jnp.maximum(loglb, a) + jnp.log(1.0 + jnp.exp(-jnp.abs(loglb - a)))
        ks.append(one_m_lb * (jnp.where(z >= 0.0, e, 1.0) / one_pe))
        lks.append((a - z) * LOG2E)
        gcums.append(jnp.dot(tri, lf * LOG2E, precision=lax.Precision.HIGHEST, preferred_element_type=F32))
        x = qraw.astype(F32)
        qs.append(x / (1.0 + jnp.exp(-x)))
        vfs.append(v.astype(F32))

    atts = [jnp.zeros((chunk, chunk), F32) for _ in chains]
    for lvl, (n, _) in enumerate(_hg_level_tables(chunk, False)):
        for c, (qraw, z, v, st_ref, masks, tri, rev) in enumerate(chains):
            refs = _hg_level_tables(chunk, rev)[lvl][1]
            gref = jnp.concatenate(
                [jnp.broadcast_to(gcums[c][r:r + 1, :], (2 * n, HG_DK)) for r in refs], axis=0)
            dec = jnp.exp2(-jnp.abs(gcums[c] - gref))
            p = lax.dot_general((qs[c] * dec).astype(BF16), (ks[c] * dec).astype(BF16), _NT,
                                preferred_element_type=F32)
            atts[c] += jnp.where(masks[lvl], p, 0.0)
    outs = [jnp.dot(atts[c].astype(BF16), chain[2], preferred_element_type=F32)
            for c, chain in enumerate(chains)]

    q_decs, k_decs, tot_decs = [], [], []
    for c, (qraw, z, v, st_ref, masks, tri, rev) in enumerate(chains):
        g_tot = gcums[c][0:1, :] if rev else gcums[c][chunk - 1:chunk, :]
        q_decs.append((qs[c] * jnp.exp2(gcums[c])).astype(BF16))
        k_decs.append((ks[c] * jnp.exp2(g_tot - gcums[c])).astype(BF16))
        tot_decs.append(jnp.exp2(g_tot))

    for c, (qraw, z, v, st_ref, masks, tri, rev) in enumerate(chains):
        g3 = gcums[c].reshape(nb, HG_LEAF, HG_DK)
        h3 = (lks[c] - gcums[c]).reshape(nb, HG_LEAF, HG_DK)
        q3 = qs[c].reshape(nb, HG_LEAF, HG_DK)
        v3 = vfs[c].reshape(nb, HG_LEAF, HG_DK)
        leaf = jnp.zeros((nb, HG_LEAF, HG_DK), F32)
        for s in range(HG_LEAF):
            col = jnp.sum(q3 * jnp.exp2(g3 + h3[:, s:s + 1, :]), axis=2, keepdims=True)
            valid = (t_idx <= s) if rev else (t_idx >= s)
            leaf += jnp.where(valid, col, 0.0) * v3[:, s:s + 1, :]
        outs[c] += leaf.reshape(chunk, HG_DK)

    for c, (qraw, z, v, st_ref, masks, tri, rev) in enumerate(chains):
        st = st_ref[...]
        outs[c] += lax.dot_general(q_decs[c], st.astype(BF16), _NT, preferred_element_type=F32)
        st_ref[...] = st * tot_decs[c] + lax.dot_general(v, k_decs[c], _TN, preferred_element_type=F32)
    return outs


def _hg_kernel(*refs, seq, ctx_len, need_ctx):
    (ql_ref, il_ref, gl_ref, ffl_ref, fbl_ref, qc_ref, ic_ref, gc_ref, ffc_ref, fbc_ref,
     llb_ref, l1m_ref, oml_ref, ng_ref) = refs[:14]
    if need_ctx:
        yl_ref, yc_ref = refs[14:16]
        scratch = refs[16:]
    else:
        yl_ref = refs[14]
        yc_ref = None
        scratch = refs[15:]
    ol_ref, oc_ref, stf_ref, stb_ref = scratch
    chunk = HG_CHUNK
    loglb = llb_ref[0]
    log1mlb = l1m_ref[0]
    one_m_lb = oml_ref[0]
    norm_g = ng_ref[0]
    ti = lax.broadcasted_iota(jnp.int32, (chunk, chunk), 0)
    si = lax.broadcasted_iota(jnp.int32, (chunk, chunk), 1)

    def direction_consts(rev):
        tri = ((si >= ti) if rev else (si <= ti)).astype(F32)
        masks = []
        for n, _ in _hg_level_tables(chunk, rev):
            shift = (2 * n).bit_length() - 1
            same = jnp.right_shift(ti, shift) == jnp.right_shift(si, shift)
            t_late = jnp.bitwise_and(ti, 2 * n - 1) >= n
            s_late = jnp.bitwise_and(si, 2 * n - 1) >= n
            if rev:
                masks.append(same & jnp.logical_not(t_late) & s_late)
            else:
                masks.append(same & t_late & jnp.logical_not(s_late))
        return tri, masks

    tri_f, masks_f = direction_consts(False)
    tri_b, masks_b = direction_consts(True)

    def run(q_s, i_s, g_s, zf_s, zb_s, o_s, y_s, n_chunks):
        def step(pi, second):
            rows = []
            chains = []
            for u in range(HG_UNROLL):
                c = pi * HG_UNROLL + u
                r_f = pl.ds(pl.multiple_of(c * chunk, chunk), chunk)
                rows.append(r_f)
                chains.append((q_s[r_f, :], zf_s[r_f, :], i_s[r_f, :], stf_ref, masks_f, tri_f, False))
            for u in range(HG_UNROLL):
                c = n_chunks - 1 - (pi * HG_UNROLL + u)
                r_b = pl.ds(pl.multiple_of(c * chunk, chunk), chunk)
                rows.append(r_b)
                chains.append((q_s[r_b, :], zb_s[r_b, :], i_s[r_b, :], stb_ref, masks_b, tri_b, True))
            outs = _hg_chunks(chains, loglb, log1mlb, one_m_lb)
            for r, o in zip(rows, outs):
                if not second:
                    o_s[r, :] = o
                elif y_s is not None:
                    y = _head_rms(o_s[r, :] + o, norm_g) * _silu(g_s[r, :].astype(F32))
                    y_s[r, :] = y.astype(y_s.dtype)

        def first(pi, carry):
            step(pi, False)
            return carry

        def second(pi, carry):
            step(pi, True)
            return carry

        n_steps = n_chunks // HG_UNROLL
        lax.fori_loop(0, n_steps // 2, first, 0)
        lax.fori_loop(n_steps // 2, n_steps, second, 0)

    stf_ref[...] = jnp.zeros_like(stf_ref)
    stb_ref[...] = jnp.zeros_like(stb_ref)
    run(qc_ref, ic_ref, gc_ref, ffc_ref, fbc_ref, oc_ref, yc_ref, ctx_len // chunk)
    run(ql_ref, il_ref, gl_ref, ffl_ref, fbl_ref, ol_ref, yl_ref, seq // chunk)


def _hgrn2(proj_h, proj_f, lb, norm_g, n_batch, seq, ctx_len, need_ctx):
    w = HG_DK
    cb0 = n_batch * seq // ctx_len
    lat = lambda off: pl.BlockSpec((seq, w), lambda b, h: (b, off // w + h))
    ctx = lambda off: pl.BlockSpec((ctx_len, w), lambda b, h: (cb0 + b, off // w + h))
    vec = pl.BlockSpec((1, 1, w), lambda b, h: (h, 0, 0))
    lb3 = lb.reshape(HG_HEADS, 1, w)
    in_specs = [lat(H_HG_Q), lat(H_HG_I), lat(H_HG_G), lat(F_HG_FF), lat(F_HG_FB),
                ctx(H_HG_Q), ctx(H_HG_I), ctx(H_HG_G), ctx(F_HG_FF), ctx(F_HG_FB),
                vec, vec, vec, vec]
    out_specs = [pl.BlockSpec((seq, w), lambda b, h: (b, h))]
    out_shape = [jax.ShapeDtypeStruct((n_batch * seq, HG_W), BF16)]
    if need_ctx:
        out_specs.append(pl.BlockSpec((ctx_len, w), lambda b, h: (b, h)))
        out_shape.append(jax.ShapeDtypeStruct((n_batch * ctx_len, HG_W), BF16))
    outs = pl.pallas_call(
        functools.partial(_hg_kernel, seq=seq, ctx_len=ctx_len, need_ctx=need_ctx),
        grid=(n_batch, HG_HEADS),
        in_specs=in_specs,
        out_specs=out_specs,
        out_shape=out_shape,
        scratch_shapes=[pltpu.VMEM((seq, w), F32), pltpu.VMEM((ctx_len, w), F32),
                        pltpu.VMEM((w, w), F32), pltpu.VMEM((w, w), F32)],
        compiler_params=_params(2),
        name="hgrn2",
    )(proj_h, proj_h, proj_h, proj_f, proj_f, proj_h, proj_h, proj_h, proj_f, proj_f,
      jnp.log(lb3), jnp.log1p(-lb3), 1.0 - lb3, norm_g.reshape(HG_HEADS, 1, w))
    return outs if need_ctx else (outs[0], None)


def _router_kernel(x_ref, g_ref, sh_ref, sc_ref, wr_ref, h_ref, idx_ref, wt_ref):
    x = x_ref[...]
    y = x * lax.rsqrt(jnp.mean(x * x, axis=-1, keepdims=True) + EPS) * g_ref[...]
    h = y * (1.0 + sc_ref[0]) + sh_ref[0]
    h_ref[...] = h.astype(h_ref.dtype)
    logits = jnp.dot(h, wr_ref[...], precision=lax.Precision.HIGHEST, preferred_element_type=F32)
    lane = lax.broadcasted_iota(jnp.int32, logits.shape, 1).astype(F32)
    lg = jnp.where(lane < N_EXPERTS, logits, -jnp.inf)
    m1 = jnp.max(lg, axis=-1, keepdims=True)
    i1 = jnp.min(jnp.where(lg == m1, lane, float(LANE)), axis=-1, keepdims=True)
    lg2 = jnp.where(lane == i1, -jnp.inf, lg)
    m2 = jnp.max(lg2, axis=-1, keepdims=True)
    i2 = jnp.min(jnp.where(lg2 == m2, lane, float(LANE)), axis=-1, keepdims=True)
    e = jnp.exp(m2 - m1)
    w1 = 1.0 / (1.0 + e)
    w2 = e / (1.0 + e)
    idx_ref[...] = jnp.where(lane == 0, i1, jnp.where(lane == 1, i2, 0.0)).astype(jnp.int32)
    wt_ref[...] = jnp.where(lane == 0, w1, jnp.where(lane == 1, w2, 0.0))


def _router(x, g, shift, scale, w_router, n_rows, seq, n_batch):
    d = x.shape[1]
    tm = 256
    tpb = seq // tm
    grp = lambda i: (_group_of(i, tpb, n_batch), 0, 0)
    wr = jnp.pad(w_router, ((0, 0), (0, LANE - N_EXPERTS)))
    return pl.pallas_call(
        _router_kernel,
        grid=(n_rows // tm,),
        in_specs=[pl.BlockSpec((tm, d), lambda i: (i, 0)),
                  pl.BlockSpec((1, d), lambda i: (0, 0)),
                  pl.BlockSpec((1, 1, d), grp),
                  pl.BlockSpec((1, 1, d), grp),
                  pl.BlockSpec((d, LANE), lambda i: (0, 0))],
        out_specs=[pl.BlockSpec((tm, d), lambda i: (i, 0)),
                   pl.BlockSpec((tm, LANE), lambda i: (i, 0)),
                   pl.BlockSpec((tm, LANE), lambda i: (i, 0))],
        out_shape=[jax.ShapeDtypeStruct((n_rows, d), BF16),
                   jax.ShapeDtypeStruct((n_rows, LANE), jnp.int32),
                   jax.ShapeDtypeStruct((n_rows, LANE), F32)],
        compiler_params=_params(1),
        name="router",
    )(x, g.reshape(1, d), shift, scale, wr)


def _new_expert(te_ref, i):
    return jnp.logical_or(i == 0, te_ref[i] != te_ref[jnp.maximum(i - 1, 0)])


def _moe_up_kernel(te_ref, nu_ref, a_ref, wg_ref, wu_ref, o_ref, wgb_ref, wub_ref):
    i = pl.program_id(1)

    @pl.when(_new_expert(te_ref, i))
    def _():
        wgb_ref[...] = wg_ref[...].astype(BF16)
        wub_ref[...] = wu_ref[...].astype(BF16)

    @pl.when(i < nu_ref[0])
    def _():
        a = a_ref[...]
        g = jnp.dot(a, wgb_ref[...], preferred_element_type=F32)
        u = jnp.dot(a, wub_ref[...], preferred_element_type=F32)
        o_ref[...] = (_silu(g) * u).astype(o_ref.dtype)

    @pl.when(i >= nu_ref[0])
    def _():
        o_ref[...] = jnp.zeros_like(o_ref)


def _moe_down_kernel(te_ref, nu_ref, a_ref, w_ref, o_ref, wb_ref):
    i = pl.program_id(1)

    @pl.when(_new_expert(te_ref, i))
    def _():
        wb_ref[...] = w_ref[...].astype(BF16)

    @pl.when(i < nu_ref[0])
    def _():
        o_ref[...] = jnp.dot(a_ref[...], wb_ref[...], preferred_element_type=F32).astype(o_ref.dtype)

    @pl.when(i >= nu_ref[0])
    def _():
        o_ref[...] = jnp.zeros_like(o_ref)


def _moe_grouped(a, weights, tile_expert, n_used, kernel, out_dtype, name):
    n_rows, k = a.shape
    n = weights[0].shape[2]
    tm, tn = MOE_TILE, COL_TILE
    a_map = lambda j, i, te, nu: (jnp.minimum(i, nu[0] - 1), 0)
    w_spec = pl.BlockSpec((None, k, tn), lambda j, i, te, nu: (te[i], 0, j))
    return pl.pallas_call(
        kernel,
        grid_spec=pltpu.PrefetchScalarGridSpec(
            num_scalar_prefetch=2,
            grid=(n // tn, n_rows // tm),
            in_specs=[pl.BlockSpec((tm, k), a_map)] + [w_spec] * len(weights),
            out_specs=pl.BlockSpec((tm, tn), lambda j, i, te, nu: (i, j)),
            scratch_shapes=[pltpu.VMEM((k, tn), BF16)] * len(weights)),
        out_shape=jax.ShapeDtypeStruct((n_rows, n), out_dtype),
        compiler_params=_params(2),
        name=name,
    )(tile_expert, n_used, a, *weights)


def _combine_kernel(x_ref, ya_ref, yb_ref, wt_ref, g_ref, o_ref):
    wt = wt_ref[...]
    y = wt[:, 0:1] * ya_ref[...].astype(F32) + wt[:, 1:2] * yb_ref[...].astype(F32)
    o_ref[...] = x_ref[...] + g_ref[0] * y


def _moe_combine(x, ya, yb, wt, gate, n_rows, seq, n_batch):
    d = x.shape[1]
    tm = 256
    tpb = seq // tm
    row = pl.BlockSpec((tm, d), lambda i: (i, 0))
    return pl.pallas_call(
        _combine_kernel,
        grid=(n_rows // tm,),
        in_specs=[row, row, row,
                  pl.BlockSpec((tm, LANE), lambda i: (i, 0)),
                  pl.BlockSpec((1, 1, d), lambda i: (_group_of(i, tpb, n_batch), 0, 0))],
        out_specs=row,
        out_shape=jax.ShapeDtypeStruct((n_rows, d), F32),
        compiler_params=_params(1),
        name="moe_combine",
    )(x, ya, yb, wt, gate)


def _moe_ffn(x, g, shift, scale, gate, w_router, wg, wu, wd, n_rows, seq, n_batch):
    h, idx, wt = _router(x, g, shift, scale, w_router, n_rows, seq, n_batch)
    tm = MOE_TILE
    n_assign = 2 * n_rows
    n_tiles = n_assign // tm + N_EXPERTS
    expert = jnp.concatenate([idx[:, 0], idx[:, 1]])
    token = jnp.concatenate([jnp.arange(n_rows, dtype=jnp.int32)] * 2)
    one_hot = (expert[:, None] == jnp.arange(N_EXPERTS, dtype=jnp.int32)[None, :]).astype(jnp.int32)
    rank = jnp.sum((jnp.cumsum(one_hot, axis=0) - 1) * one_hot, axis=1)
    counts = jnp.sum(one_hot, axis=0)
    tiles = (counts + tm - 1) // tm
    tile_end = jnp.cumsum(tiles)
    tile_start = tile_end - tiles
    dest = jnp.sum(tile_start[None, :] * one_hot, axis=1) * tm + rank
    src_token = jnp.zeros((n_tiles * tm,), jnp.int32).at[dest].set(token)
    n_used = tile_end[-1:].astype(jnp.int32)
    tile_id = jnp.minimum(jnp.arange(n_tiles, dtype=jnp.int32), n_used[0] - 1)
    tile_expert = jnp.sum((tile_end[None, :] <= tile_id[:, None]).astype(jnp.int32), axis=1)

    a_sorted = jnp.take(h, src_token, axis=0, mode="clip")
    u = _moe_grouped(a_sorted, (wg, wu), tile_expert, n_used, _moe_up_kernel, BF16, "moe_up")
    y = _moe_grouped(u, (wd,), tile_expert, n_used, _moe_down_kernel, BF16, "moe_down")
    ya = jnp.take(y, dest[:n_rows], axis=0, mode="clip")
    yb = jnp.take(y, dest[n_rows:], axis=0, mode="clip")
    return _moe_combine(x, ya, yb, wt, gate, n_rows, seq, n_batch)


def _rope_tables(n_batch, seq, n_ctx_rows):
    t = jnp.arange(seq)
    pos = jnp.stack([t // GRID_W, t % GRID_W], axis=-1).astype(F32)
    inv_freq = jnp.exp(-jnp.log(ROPE_BASE) * jnp.arange(0, ROPE_AXIS_DIM, 2, dtype=F32) / ROPE_AXIS_DIM)
    ang = pos[..., None] * inv_freq
    cos, sin = jnp.cos(ang), jnp.sin(ang)
    cos = jnp.concatenate([cos[:, 0], cos[:, 0], cos[:, 1], cos[:, 1]], axis=-1)
    sin = jnp.concatenate([-sin[:, 0], sin[:, 0], -sin[:, 1], sin[:, 1]], axis=-1)
    cos = jnp.concatenate([jnp.tile(cos, (n_batch, 1)), jnp.ones((n_ctx_rows, HEAD_DIM), F32)], axis=0)
    sin = jnp.concatenate([jnp.tile(sin, (n_batch, 1)), jnp.zeros((n_ctx_rows, HEAD_DIM), F32)], axis=0)
    return cos, sin


def _split_w_in(w):
    w_h = jnp.concatenate([w[:, :_O_ML_GATES], w[:, _O_HG_Q:_O_HG_FF], w[:, _O_HG_I:]], axis=1)
    w_f = jnp.concatenate([w[:, _O_HG_FF:_O_HG_I], w[:, _O_ML_GATES:_O_HG_Q],
                           jnp.zeros((w.shape[0], F_WIDTH - F_ML_G - 4 * ML_HEADS), w.dtype)], axis=1)
    return w_h.astype(BF16), w_f.astype(BF16)


def kernel(x, c, ctx, c_ctx, w_mod, b_mod, norm1_g, norm2_g, w_in, attn_q_norm_g, attn_k_norm_g, ml_conv_w,
           ml_gate_b, ml_norm_g, hg_lb_logits, hg_norm_g, w_branch, w_out, ffn_w_gate, ffn_w_up, ffn_w_down,
           moe_w_router, moe_w_gate, moe_w_up, moe_w_down):
    n_batch, seq, d = x.shape
    ctx_len = ctx.shape[1]
    depth = w_mod.shape[0]
    m_lat = n_batch * seq
    m_ctx = n_batch * ctx_len
    m_all = m_lat + m_ctx
    assert d == D_MODEL and w_in.shape[2] == _N_IN
    assert seq % ROW_TILE == 0 and m_ctx % ROW_TILE == 0 and seq % ctx_len == 0
    assert seq % ML_CHUNK == 0 and ctx_len % ML_CHUNK == 0 and ctx_len % HG_CHUNK == 0

    xa = jnp.concatenate([x.reshape(m_lat, d), ctx.reshape(m_ctx, d)], axis=0)
    cs = jnp.concatenate([c, c_ctx[None], jnp.zeros((8 - n_batch - 1, d), F32)], axis=0)
    mod = _modulation(cs, w_mod, b_mod)
    cos, sin = _rope_tables(n_batch, seq, m_ctx)
    lb_all = jnp.cumsum(jax.nn.softmax(hg_lb_logits.astype(F32), axis=0), axis=0)
    lb_all = lb_all - lb_all[:1]

    for l in range(depth):
        need_ctx = l < depth - 1
        n_rows = m_all if need_ctx else m_lat
        mods = [mod[l, :n_batch + 1, i * d:(i + 1) * d].reshape(n_batch + 1, 1, d) for i in range(N_MOD)]
        w_h, w_f = _split_w_in(w_in[l])

        h = _norm_mod(xa, norm1_g[l], mods[0], mods[1], m_all, seq, n_batch)
        proj_h = _matmul(h, w_h, BF16, COL_TILE)
        proj_f = _matmul(h, w_f, F32, F_WIDTH // 3)

        q_rot, k_rot = _att_prep(proj_h, cos, sin, attn_q_norm_g[l], attn_k_norm_g[l])
        ya = _attention(q_rot, k_rot, proj_h, n_batch, seq, ctx_len, True)

        gates = proj_f[:, F_ML_G:F_ML_G + 4 * ML_HEADS].reshape(m_all, 4, ML_HEADS)
        gates_c = gates.transpose(2, 0, 1)
        gates_r = gates.transpose(2, 1, 0)
        ym, ym_c = _mlstm(proj_h, gates_c, gates_r, ml_conv_w[l], ml_gate_b[l], ml_norm_g[l],
                          n_batch, seq, ctx_len, need_ctx)
        yh, yh_c = _hgrn2(proj_h, proj_f, lb_all[l], hg_norm_g[l], n_batch, seq, ctx_len, need_ctx)
        ys_ctx = None
        if need_ctx:
            ya_c = _attention(q_rot, k_rot, proj_h, n_batch, seq, ctx_len, False)
            ys_ctx = (ya_c, ym_c, yh_c)

        merged = _branch_merge((ya, ym, yh), ys_ctx, w_branch[l], proj_h, n_rows)
        xa = _matmul_residual(merged, w_out[l], xa, mods[2], n_rows, seq, n_batch, "proj_out")

        if l % 2 == 0:
            h2 = _norm_mod(xa, norm2_g[l], mods[3], mods[4], n_rows, seq, n_batch)
            u = _glu(h2, ffn_w_gate[l // 2], ffn_w_up[l // 2])
            xa = _matmul_residual(u, ffn_w_down[l // 2], xa, mods[5], n_rows, seq, n_batch, "ffn_down")
        else:
            xa = _moe_ffn(xa, norm2_g[l], mods[3], mods[4], mods[5], moe_w_router[l // 2],
                          moe_w_gate[l // 2], moe_w_up[l // 2], moe_w_down[l // 2], n_rows, seq, n_batch)
    return xa[:m_lat].reshape(n_batch, seq, d)
```

```python
import functools

import jax
import jax.numpy as jnp
from jax import lax
from jax.experimental import pallas as pl
from jax.experimental.pallas import tpu as pltpu

F32 = jnp.float32
BF16 = jnp.bfloat16

D_MODEL = 2048
GRID_W = 64
N_MOD = 6
ATT_HEADS = 8
ATT_KV_HEADS = 2
HEAD_DIM = 128
ATT_W = ATT_HEADS * HEAD_DIM
KV_W = ATT_KV_HEADS * HEAD_DIM
ROPE_AXIS_DIM = HEAD_DIM // 2
ROPE_BASE = 10000.0
ML_HEADS = 4
ML_DK = 256
ML_W = ML_HEADS * ML_DK
HG_HEADS = 8
HG_DK = 128
HG_W = HG_HEADS * HG_DK
D_FF = 5632
N_EXPERTS = 8
EPS = 1e-6
NEG_INIT = -1e30
LOG2E = 1.4426950408889634

_O_ATT_Q = 0
_O_ML_GATES = ATT_W + 2 * KV_W + 4 * ML_W
_O_HG_Q = _O_ML_GATES + 4 * ML_HEADS
_O_HG_FF = _O_HG_Q + HG_W
_O_HG_I = _O_HG_FF + 2 * HG_W
_O_MERGE = _O_HG_I + 2 * HG_W
_N_IN = _O_MERGE + 3 * D_MODEL

H_ATT_Q = 0
H_ATT_K = 1024
H_ATT_V = 1280
H_ML_Q = 1536
H_ML_K = 2560
H_ML_V = 3584
H_ML_O = 4608
H_HG_Q = 5632
H_HG_I = 6656
H_HG_G = 7680
H_MERGE = 8704
H_WIDTH = 14848
F_HG_FF = 0
F_HG_FB = 1024
F_ML_G = 2048
F_WIDTH = 2304

LANE = 128
ROW_TILE = 512
COL_TILE = 512
PROJ_ROW_TILES = 4
VMEM_LIMIT = 56 * 1024 * 1024

ML_CHUNK = 256
HG_CHUNK = 64
HG_LEAF = 8
HG_UNROLL = 2
ATT_TQ = 256
MOE_TILE = 512

_NT = (((1,), (1,)), ((), ()))
_TN = (((0,), (0,)), ((), ()))


def _params(n_axes):
    return pltpu.CompilerParams(dimension_semantics=("arbitrary",) * n_axes,
                                vmem_limit_bytes=VMEM_LIMIT)


def _silu(x):
    return x * jax.nn.sigmoid(x)


def _log_sigmoid(x):
    return jnp.minimum(x, 0.0) - jnp.log1p(jnp.exp(-jnp.abs(x)))


def _mod_kernel(c_ref, w_ref, b_ref, o_ref):
    a = _silu(c_ref[...])
    o_ref[...] = jnp.dot(a.astype(BF16), w_ref[...].astype(BF16), preferred_element_type=F32) + b_ref[...]


def _modulation(cs, w_mod, b_mod):
    depth, d, n = w_mod.shape
    tn = 1024
    return pl.pallas_call(
        _mod_kernel,
        grid=(depth, n // tn),
        in_specs=[pl.BlockSpec((8, d), lambda l, j: (0, 0)),
                  pl.BlockSpec((None, d, tn), lambda l, j: (l, 0, j)),
                  pl.BlockSpec((None, 1, tn), lambda l, j: (l, 0, j))],
        out_specs=pl.BlockSpec((None, 8, tn), lambda l, j: (l, 0, j)),
        out_shape=jax.ShapeDtypeStruct((depth, 8, n), F32),
        compiler_params=_params(2),
        name="modulation",
    )(cs, w_mod, b_mod.reshape(depth, 1, n))


def _norm_kernel(x_ref, g_ref, sh_ref, sc_ref, o_ref):
    x = x_ref[...]
    y = x * lax.rsqrt(jnp.mean(x * x, axis=-1, keepdims=True) + EPS)
    y = y * g_ref[...]
    o_ref[...] = (y * (1.0 + sc_ref[0]) + sh_ref[0]).astype(o_ref.dtype)


def _group_of(i, tiles_per_batch, n_batch):
    return jnp.minimum(i // tiles_per_batch, n_batch)


def _norm_mod(x, g, shift, scale, n_rows, seq, n_batch):
    d = x.shape[1]
    tm = 256
    tpb = seq // tm
    grp = lambda i: (_group_of(i, tpb, n_batch), 0, 0)
    return pl.pallas_call(
        _norm_kernel,
        grid=(n_rows // tm,),
        in_specs=[pl.BlockSpec((tm, d), lambda i: (i, 0)),
                  pl.BlockSpec((1, d), lambda i: (0, 0)),
                  pl.BlockSpec((1, 1, d), grp),
                  pl.BlockSpec((1, 1, d), grp)],
        out_specs=pl.BlockSpec((tm, d), lambda i: (i, 0)),
        out_shape=jax.ShapeDtypeStruct((n_rows, d), BF16),
        compiler_params=_params(1),
        name="norm_mod",
    )(x, g.reshape(1, d), shift, scale)


def _mm_kernel(a_ref, w_ref, o_ref):
    o_ref[...] = jnp.dot(a_ref[...], w_ref[...], preferred_element_type=F32).astype(o_ref.dtype)


def _matmul(a, w, out_dtype, tn):
    m, k = a.shape
    n = w.shape[1]
    tm = m // PROJ_ROW_TILES
    return pl.pallas_call(
        _mm_kernel,
        grid=(n // tn, m // tm),
        in_specs=[pl.BlockSpec((tm, k), lambda j, i: (i, 0)),
                  pl.BlockSpec((k, tn), lambda j, i: (0, j))],
        out_specs=pl.BlockSpec((tm, tn), lambda j, i: (i, j)),
        out_shape=jax.ShapeDtypeStruct((m, n), out_dtype),
        compiler_params=_params(2),
        name="proj_in",
    )(a, w)


def _mm_res_kernel(a_ref, w_ref, r_ref, g_ref, o_ref, wb_ref):
    @pl.when(pl.program_id(1) == 0)
    def _():
        wb_ref[...] = w_ref[...].astype(BF16)

    y = jnp.dot(a_ref[...], wb_ref[...], preferred_element_type=F32)
    o_ref[...] = r_ref[...] + g_ref[0] * y


def _matmul_residual(a, w, res, gate, n_rows, seq, n_batch, name):
    k = a.shape[1]
    n = w.shape[1]
    tm, tn = ROW_TILE, COL_TILE
    tpb = seq // tm
    return pl.pallas_call(
        _mm_res_kernel,
        grid=(n // tn, n_rows // tm),
        in_specs=[pl.BlockSpec((tm, k), lambda j, i: (i, 0)),
                  pl.BlockSpec((k, tn), lambda j, i: (0, j)),
                  pl.BlockSpec((tm, tn), lambda j, i: (i, j)),
                  pl.BlockSpec((1, 1, tn), lambda j, i: (_group_of(i, tpb, n_batch), 0, j))],
        out_specs=pl.BlockSpec((tm, tn), lambda j, i: (i, j)),
        out_shape=jax.ShapeDtypeStruct((n_rows, n), F32),
        scratch_shapes=[pltpu.VMEM((k, tn), BF16)],
        compiler_params=_params(2),
        name=name,
    )(a, w, res, gate)


def _glu_kernel(a_ref, wg_ref, wu_ref, o_ref, wgb_ref, wub_ref):
    @pl.when(pl.program_id(1) == 0)
    def _():
        wgb_ref[...] = wg_ref[...].astype(BF16)
        wub_ref[...] = wu_ref[...].astype(BF16)

    a = a_ref[...]
    g = jnp.dot(a, wgb_ref[...], preferred_element_type=F32)
    u = jnp.dot(a, wub_ref[...], preferred_element_type=F32)
    o_ref[...] = (_silu(g) * u).astype(o_ref.dtype)


def _glu(a, wg, wu):
    m, k = a.shape
    n = wg.shape[1]
    tm, tn = ROW_TILE, COL_TILE
    return pl.pallas_call(
        _glu_kernel,
        grid=(n // tn, m // tm),
        in_specs=[pl.BlockSpec((tm, k), lambda j, i: (i, 0)),
                  pl.BlockSpec((k, tn), lambda j, i: (0, j)),
                  pl.BlockSpec((k, tn), lambda j, i: (0, j))],
        out_specs=pl.BlockSpec((tm, tn), lambda j, i: (i, j)),
        out_shape=jax.ShapeDtypeStruct((m, n), BF16),
        scratch_shapes=[pltpu.VMEM((k, tn), BF16), pltpu.VMEM((k, tn), BF16)],
        compiler_params=_params(2),
        name="ffn_up",
    )(a, wg, wu)


def _merge_kernel(*refs, n_lat_tiles, with_ctx):
    if with_ctx:
        lat_refs, ctx_refs, rest = refs[0:3], refs[3:6], refs[6:]
    else:
        lat_refs, ctx_refs, rest = refs[0:3], None, refs[3:]
    w_ref, g0_ref, g1_ref, g2_ref, o_ref, wb_ref = rest

    @pl.when(pl.program_id(1) == 0)
    def _():
        wb_ref[...] = w_ref[...].astype(BF16)

    is_ctx = pl.program_id(1) >= n_lat_tiles
    acc = None
    for b, g_ref in enumerate((g0_ref, g1_ref, g2_ref)):
        y = lat_refs[b][...]
        if with_ctx:
            y = jnp.where(is_ctx, ctx_refs[b][...], y)
        term = jax.nn.sigmoid(g_ref[...].astype(F32)) * jnp.dot(y, wb_ref[b], preferred_element_type=F32)
        acc = term if acc is None else acc + term
    o_ref[...] = acc.astype(o_ref.dtype)


def _branch_merge(ys_lat, ys_ctx, w_branch, proj_h, n_rows):
    k = ys_lat[0].shape[1]
    n = w_branch.shape[2]
    tm, tn = ROW_TILE, COL_TILE
    n_lat_tiles = ys_lat[0].shape[0] // tm
    with_ctx = ys_ctx is not None
    gate_spec = lambda b: pl.BlockSpec((tm, tn), lambda j, i: (i, (H_MERGE + b * n) // tn + j))
    lat_spec = pl.BlockSpec((tm, k), lambda j, i: (jnp.minimum(i, n_lat_tiles - 1), 0))
    ctx_spec = pl.BlockSpec((tm, k), lambda j, i: (jnp.maximum(i - n_lat_tiles, 0), 0))
    in_specs = [lat_spec] * 3 + ([ctx_spec] * 3 if with_ctx else [])
    in_specs += [pl.BlockSpec((3, k, tn), lambda j, i: (0, 0, j)), gate_spec(0), gate_spec(1), gate_spec(2)]
    args = list(ys_lat) + (list(ys_ctx) if with_ctx else [])
    return pl.pallas_call(
        functools.partial(_merge_kernel, n_lat_tiles=n_lat_tiles, with_ctx=with_ctx),
        grid=(n // tn, n_rows // tm),
        in_specs=in_specs,
        out_specs=pl.BlockSpec((tm, tn), lambda j, i: (i, j)),
        out_shape=jax.ShapeDtypeStruct((n_rows, n), BF16),
        scratch_shapes=[pltpu.VMEM((3, k, tn), BF16)],
        compiler_params=_params(2),
        name="branch_merge",
    )(*args, w_branch, proj_h, proj_h, proj_h)


def _head_rms(x, g):
    return x * lax.rsqrt(jnp.mean(x * x, axis=-1, keepdims=True) + EPS) * g


def _rope(x, cos, sin, low_half):
    partner = jnp.where(low_half, pltpu.roll(x, 96, axis=1), pltpu.roll(x, 32, axis=1))
    return x * cos + partner * sin


def _att_prep_kernel(q_ref, k_ref, cos_ref, sin_ref, qg_ref, kg_ref, qo_ref, ko_ref):
    cos = cos_ref[...]
    sin = sin_ref[...]
    lane = lax.broadcasted_iota(jnp.int32, cos.shape, 1)
    low_half = jnp.bitwise_and(lane, 63) < 32
    scale = HEAD_DIM ** -0.5 * LOG2E
    for h in range(ATT_HEADS):
        sl = slice(h * HEAD_DIM, (h + 1) * HEAD_DIM)
        x = _head_rms(q_ref[:, sl].astype(F32), qg_ref[...])
        qo_ref[:, sl] = (_rope(x, cos, sin, low_half) * scale).astype(qo_ref.dtype)
    for h in range(ATT_KV_HEADS):
        sl = slice(h * HEAD_DIM, (h + 1) * HEAD_DIM)
        x = _head_rms(k_ref[:, sl].astype(F32), kg_ref[...])
        ko_ref[:, sl] = _rope(x, cos, sin, low_half).astype(ko_ref.dtype)


def _att_prep(proj_h, cos, sin, qg, kg):
    m = proj_h.shape[0]
    tm = ROW_TILE
    return pl.pallas_call(
        _att_prep_kernel,
        grid=(m // tm,),
        in_specs=[pl.BlockSpec((tm, ATT_W), lambda i: (i, H_ATT_Q // ATT_W)),
                  pl.BlockSpec((tm, KV_W), lambda i: (i, H_ATT_K // KV_W)),
                  pl.BlockSpec((tm, HEAD_DIM), lambda i: (i, 0)),
                  pl.BlockSpec((tm, HEAD_DIM), lambda i: (i, 0)),
                  pl.BlockSpec((1, HEAD_DIM), lambda i: (0, 0)),
                  pl.BlockSpec((1, HEAD_DIM), lambda i: (0, 0))],
        out_specs=[pl.BlockSpec((tm, ATT_W), lambda i: (i, 0)),
                   pl.BlockSpec((tm, KV_W), lambda i: (i, 0))],
        out_shape=[jax.ShapeDtypeStruct((m, ATT_W), BF16),
                   jax.ShapeDtypeStruct((m, KV_W), BF16)],
        compiler_params=_params(1),
        name="att_prep",
    )(proj_h, proj_h, cos, sin, qg.reshape(1, HEAD_DIM), kg.reshape(1, HEAD_DIM))


def _att_kernel(*refs, with_latent):
    if with_latent:
        q_ref, kc_ref, vc_ref, kl_ref, vl_ref, o_ref = refs
    else:
        q_ref, kc_ref, vc_ref, o_ref = refs
    grp = ATT_HEADS // ATT_KV_HEADS
    for g in range(grp):
        sl = slice(g * HEAD_DIM, (g + 1) * HEAD_DIM)
        q = q_ref[:, sl]
        sc = lax.dot_general(q, kc_ref[...], _NT, preferred_element_type=F32)
        m = jnp.max(sc, axis=-1, keepdims=True)
        if with_latent:
            sl_ = lax.dot_general(q, kl_ref[...], _NT, preferred_element_type=F32)
            m = jnp.maximum(m, jnp.max(sl_, axis=-1, keepdims=True))
        pc = jnp.exp2(sc - m)
        den = jnp.sum(pc, axis=-1, keepdims=True)
        acc = jnp.dot(pc.astype(BF16), vc_ref[...], preferred_element_type=F32)
        if with_latent:
            pl_ = jnp.exp2(sl_ - m)
            den += jnp.sum(pl_, axis=-1, keepdims=True)
            acc += jnp.dot(pl_.astype(BF16), vl_ref[...], preferred_element_type=F32)
        o_ref[:, sl] = (acc / den).astype(o_ref.dtype)


def _attention(q_rot, k_rot, proj_h, n_batch, seq, ctx_len, latent_queries):
    grp_w = ATT_W // ATT_KV_HEADS
    ctx_blk0 = n_batch * seq // ctx_len
    v_col = H_ATT_V // HEAD_DIM
    if latent_queries:
        tq = ATT_TQ
        nq = seq // tq
        q_map = lambda b, k, i: (b * nq + i, k)
        n_out = n_batch * seq
    else:
        tq = ctx_len
        nq = 1
        q_map = lambda b, k, i: (ctx_blk0 + b, k)
        n_out = n_batch * ctx_len
    in_specs = [pl.BlockSpec((tq, grp_w), q_map),
                pl.BlockSpec((ctx_len, HEAD_DIM), lambda b, k, i: (ctx_blk0 + b, k)),
                pl.BlockSpec((ctx_len, HEAD_DIM), lambda b, k, i: (ctx_blk0 + b, v_col + k))]
    args = [q_rot, k_rot, proj_h]
    if latent_queries:
        in_specs += [pl.BlockSpec((seq, HEAD_DIM), lambda b, k, i: (b, k)),
                     pl.BlockSpec((seq, HEAD_DIM), lambda b, k, i: (b, v_col + k))]
        args += [k_rot, proj_h]
    return pl.pallas_call(
        functools.partial(_att_kernel, with_latent=latent_queries),
        grid=(n_batch, ATT_KV_HEADS, nq),
        in_specs=in_specs,
        out_specs=pl.BlockSpec((tq, grp_w), lambda b, k, i: (b * nq + i, k)),
        out_shape=jax.ShapeDtypeStruct((n_out, ATT_W), BF16),
        compiler_params=_params(3),
        name="attention_lat" if latent_queries else "attention_ctx",
    )(*args)


def _conv3(x_ref, w, c, n_chunks, chunk):
    r0 = pl.multiple_of(c * chunk, chunk)
    x = x_ref[pl.ds(r0, chunk), :].astype(F32)
    prev_blk = x_ref[pl.ds(pl.multiple_of(jnp.maximum(r0 - 16, 0), 16), 16), :].astype(F32)
    next_blk = x_ref[pl.ds(pl.multiple_of(jnp.minimum(r0 + chunk, (n_chunks - 1) * chunk), 16), 16), :].astype(F32)
    prev_row = jnp.where(c > 0, prev_blk[15:16, :], 0.0)
    next_row = jnp.where(c < n_chunks - 1, next_blk[0:1, :], 0.0)
    row = lax.broadcasted_iota(jnp.int32, x.shape, 0)
    x_m1 = jnp.where(row == 0, prev_row, pltpu.roll(x, 1, axis=0))
    x_p1 = jnp.where(row == chunk - 1, next_row, pltpu.roll(x, chunk - 1, axis=0))
    return x_m1 * w[0:1, :] + x * w[1:2, :] + x_p1 * w[2:3, :]


def _ml_chunk(q, k, v, lfr, lir, lfc, lic, c_ref, n_ref, m_st, rev):
    n_tok = q.shape[0]
    ti = lax.broadcasted_iota(jnp.int32, (n_tok, n_tok), 0)
    si = lax.broadcasted_iota(jnp.int32, (n_tok, n_tok), 1)
    causal = (si >= ti) if rev else (si <= ti)
    causal_t = (ti >= si) if rev else (ti <= si)
    fc_col = jnp.sum(jnp.where(causal, lfr, 0.0), axis=1, keepdims=True)
    fc_row = jnp.sum(jnp.where(causal_t, lfc, 0.0), axis=0, keepdims=True)
    f_tot = jnp.sum(lfr, axis=1, keepdims=True)
    dmat = jnp.where(causal, fc_col - fc_row + lir, -jnp.inf)
    inter = fc_col + m_st
    m_t = jnp.maximum(inter, jnp.max(dmat, axis=1, keepdims=True))
    w = jnp.exp(dmat - m_t)
    a = jnp.exp(inter - m_t)
    sc = lax.dot_general(q, k, _NT, preferred_element_type=F32) * w
    num = jnp.dot(sc.astype(BF16), v, preferred_element_type=F32)
    num += a * jnp.dot(q, c_ref[...].astype(BF16), preferred_element_type=F32)
    den = jnp.sum(sc, axis=1, keepdims=True)
    den += a * jnp.sum(q.astype(F32) * n_ref[...], axis=1, keepdims=True)
    h = num / jnp.maximum(jnp.abs(den), jnp.exp(-m_t))
    g_col = f_tot - fc_col + lic
    g_row = f_tot - fc_row + lir
    m_new = jnp.maximum(f_tot + m_st, jnp.max(g_row, axis=1, keepdims=True))
    decay = jnp.exp(f_tot + m_st - m_new)
    kw = k.astype(F32) * jnp.exp(g_col - m_new)
    c_ref[...] = decay * c_ref[...] + lax.dot_general(kw.astype(BF16), v, _TN, preferred_element_type=F32)
    n_ref[...] = decay * n_ref[...] + jnp.sum(kw, axis=0, keepdims=True)
    return h, m_new


def _ml_kernel(*refs, seq, ctx_len, need_ctx):
    (ql_ref, kl_ref, vl_ref, ol_ref, qc_ref, kc_ref, vc_ref, oc_ref,
     wq_ref, wk_ref, gcl_ref, grl_ref, gcc_ref, grc_ref, bc_ref, br_ref, ng_ref) = refs[:17]
    if need_ctx:
        yl_ref, yc_ref = refs[17:19]
        scratch = refs[19:]
    else:
        yl_ref = refs[17]
        yc_ref = None
        scratch = refs[18:]
    qsl_ref, ksl_ref, qsc_ref, ksc_ref, hl_ref, hc_ref, c_ref, n_ref = scratch
    chunk = ML_CHUNK
    k_scale = ML_DK ** -0.5
    wq = wq_ref[0]
    wk = wk_ref[0]

    def conv_pass(src_q, src_k, dst_q, dst_k, n_chunks):
        def body(c, carry):
            r0 = pl.multiple_of(c * chunk, chunk)
            dst_q[pl.ds(r0, chunk), :] = _conv3(src_q, wq, c, n_chunks, chunk).astype(BF16)
            dst_k[pl.ds(r0, chunk), :] = (_conv3(src_k, wk, c, n_chunks, chunk) * k_scale).astype(BF16)
            return carry
        lax.fori_loop(0, n_chunks, body, 0)

    conv_pass(qc_ref, kc_ref, qsc_ref, ksc_ref, ctx_len // chunk)
    conv_pass(ql_ref, kl_ref, qsl_ref, ksl_ref, seq // chunk)

    bias_c = bc_ref[0]
    bias_r = br_ref[0]
    norm_g = ng_ref[0]

    def run(q_s, k_s, v_s, o_s, gc_s, gr_s, h_s, y_s, n_chunks, m_st, rev, last):
        d = 1 if rev else 0

        def body(ci, m_st):
            c = (n_chunks - 1 - ci) if rev else ci
            r0 = pl.multiple_of(c * chunk, chunk)
            rows = pl.ds(r0, chunk)
            gc = gc_s[0, rows, :] + bias_c
            gr = gr_s[0, :, rows] + bias_r
            lic = gc[:, d:d + 1]
            lfc = _log_sigmoid(gc[:, 2 + d:3 + d])
            lir = gr[d:d + 1, :]
            lfr = _log_sigmoid(gr[2 + d:3 + d, :])
            h, m_new = _ml_chunk(q_s[rows, :], k_s[rows, :], v_s[rows, :], lfr, lir, lfc, lic,
                                 c_ref, n_ref, m_st, rev)
            if not last:
                h_s[rows, :] = h
            elif y_s is not None:
                hs = h_s[rows, :] + h
                y = _head_rms(hs, norm_g) * jax.nn.sigmoid(o_s[rows, :].astype(F32))
                y_s[rows, :] = y.astype(y_s.dtype)
            return m_new

        return lax.fori_loop(0, n_chunks, body, m_st)

    for rev in (False, True):
        c_ref[...] = jnp.zeros_like(c_ref)
        n_ref[...] = jnp.zeros_like(n_ref)
        m0 = jnp.full((1, 1), NEG_INIT, F32)
        m1 = run(qsc_ref, ksc_ref, vc_ref, oc_ref, gcc_ref, grc_ref, hc_ref, yc_ref,
                 ctx_len // chunk, m0, rev, rev)
        run(qsl_ref, ksl_ref, vl_ref, ol_ref, gcl_ref, grl_ref, hl_ref, yl_ref,
            seq // chunk, m1, rev, rev)


def _mlstm(proj_h, gates_c, gates_r, conv_w, gate_b, norm_g, n_batch, seq, ctx_len, need_ctx):
    w = ML_DK
    cb0 = n_batch * seq // ctx_len
    col = lambda off: off // w
    lat = lambda off: pl.BlockSpec((seq, w), lambda b, h: (b, col(off) + h))
    ctx = lambda off: pl.BlockSpec((ctx_len, w), lambda b, h: (cb0 + b, col(off) + h))
    conv = conv_w.reshape(3, 2 * ML_HEADS, w).transpose(1, 0, 2)
    gb = gate_b.reshape(4, ML_HEADS).T
    in_specs = [lat(H_ML_Q), lat(H_ML_K), lat(H_ML_V), lat(H_ML_O),
                ctx(H_ML_Q), ctx(H_ML_K), ctx(H_ML_V), ctx(H_ML_O),
                pl.BlockSpec((1, 3, w), lambda b, h: (h, 0, 0)),
                pl.BlockSpec((1, 3, w), lambda b, h: (ML_HEADS + h, 0, 0)),
                pl.BlockSpec((1, seq, 4), lambda b, h: (h, b, 0)),
                pl.BlockSpec((1, 4, seq), lambda b, h: (h, 0, b)),
                pl.BlockSpec((1, ctx_len, 4), lambda b, h: (h, cb0 + b, 0)),
                pl.BlockSpec((1, 4, ctx_len), lambda b, h: (h, 0, cb0 + b)),
                pl.BlockSpec((1, 1, 4), lambda b, h: (h, 0, 0)),
                pl.BlockSpec((1, 4, 1), lambda b, h: (h, 0, 0)),
                pl.BlockSpec((1, 1, w), lambda b, h: (h, 0, 0))]
    out_specs = [pl.BlockSpec((seq, w), lambda b, h: (b, h))]
    out_shape = [jax.ShapeDtypeStruct((n_batch * seq, ML_W), BF16)]
    if need_ctx:
        out_specs.append(pl.BlockSpec((ctx_len, w), lambda b, h: (b, h)))
        out_shape.append(jax.ShapeDtypeStruct((n_batch * ctx_len, ML_W), BF16))
    outs = pl.pallas_call(
        functools.partial(_ml_kernel, seq=seq, ctx_len=ctx_len, need_ctx=need_ctx),
        grid=(n_batch, ML_HEADS),
        in_specs=in_specs,
        out_specs=out_specs,
        out_shape=out_shape,
        scratch_shapes=[pltpu.VMEM((seq, w), BF16), pltpu.VMEM((seq, w), BF16),
                        pltpu.VMEM((ctx_len, w), BF16), pltpu.VMEM((ctx_len, w), BF16),
                        pltpu.VMEM((seq, w), F32), pltpu.VMEM((ctx_len, w), F32),
                        pltpu.VMEM((w, w), F32), pltpu.VMEM((1, w), F32)],
        compiler_params=_params(2),
        name="mlstm",
    )(proj_h, proj_h, proj_h, proj_h, proj_h, proj_h, proj_h, proj_h,
      conv, conv, gates_c, gates_r, gates_c, gates_r,
      gb.reshape(ML_HEADS, 1, 4), gb.reshape(ML_HEADS, 4, 1), norm_g.reshape(ML_HEADS, 1, w))
    return outs if need_ctx else (outs[0], None)


def _hg_level_tables(chunk, rev):
    levels = []
    n = HG_LEAF
    while 2 * n <= chunk:
        refs = []
        for start in range(0, chunk, 2 * n):
            refs.append((start + n - 1) if rev else (start + n))
        levels.append((n, refs))
        n *= 2
    return levels


def _hg_chunks(chains, loglb, log1mlb, one_m_lb):
    chunk = chains[0][1].shape[0]
    nb = chunk // HG_LEAF
    t_idx = lax.broadcasted_iota(jnp.int32, (1, HG_LEAF, 1), 1)

    qs, ks, lks, vfs, gcums = [], [], [], [], []
    for qraw, z, v, st_ref, masks, tri, rev in chains:
        e = jnp.exp(-jnp.abs(z))
        one_pe = 1.0 + e
        lsig = jnp.minimum(z, 0.0) - jnp.log(one_pe)
        a = log1mlb + lsig
        lf = jnp.maximum(loglb, a) + jnp.log(1.0 + jnp.exp(-jnp.abs(loglb - a)))
        ks.append(one_m_lb * (jnp.where(z >= 0.0, e, 1.0) / one_pe))
        lks.append((a - z) * LOG2E)
        gcums.append(jnp.dot(tri, lf * LOG2E, precision=lax.Precision.HIGHEST, preferred_element_type=F32))
        x = qraw.astype(F32)
        qs.append(x / (1.0 + jnp.exp(-x)))
        vfs.append(v.astype(F32))

    atts = [jnp.zeros((chunk, chunk), F32) for _ in chains]
    for lvl, (n, _) in enumerate(_hg_level_tables(chunk, False)):
        for c, (qraw, z, v, st_ref, masks, tri, rev) in enumerate(chains):
            refs = _hg_level_tables(chunk, rev)[lvl][1]
            gref = jnp.concatenate(
                [jnp.broadcast_to(gcums[c][r:r + 1, :], (2 * n, HG_DK)) for r in refs], axis=0)
            dec = jnp.exp2(-jnp.abs(gcums[c] - gref))
            p = lax.dot_general((qs[c] * dec).astype(BF16), (ks[c] * dec).astype(BF16), _NT,
                                preferred_element_type=F32)
            atts[c] += jnp.where(masks[lvl], p, 0.0)
    outs = [jnp.dot(atts[c].astype(BF16), chain[2], preferred_element_type=F32)
            for c, chain in enumerate(chains)]

    q_decs, k_decs, tot_decs = [], [], []
    for c, (qraw, z, v, st_ref, masks, tri, rev) in enumerate(chains):
        g_tot = gcums[c][0:1, :] if rev else gcums[c][chunk - 1:chunk, :]
        q_decs.append((qs[c] * jnp.exp2(gcums[c])).astype(BF16))
        k_decs.append((ks[c] * jnp.exp2(g_tot - gcums[c])).astype(BF16))
        tot_decs.append(jnp.exp2(g_tot))

    for c, (qraw, z, v, st_ref, masks, tri, rev) in enumerate(chains):
        g3 = gcums[c].reshape(nb, HG_LEAF, HG_DK)
        h3 = (lks[c] - gcums[c]).reshape(nb, HG_LEAF, HG_DK)
        q3 = qs[c].reshape(nb, HG_LEAF, HG_DK)
        v3 = vfs[c].reshape(nb, HG_LEAF, HG_DK)
        leaf = jnp.zeros((nb, HG_LEAF, HG_DK), F32)
        for s in range(HG_LEAF):
            col = jnp.sum(q3 * jnp.exp2(g3 + h3[:, s:s + 1, :]), axis=2, keepdims=True)
            valid = (t_idx <= s) if rev else (t_idx >= s)
            leaf += jnp.where(valid, col, 0.0) * v3[:, s:s + 1, :]
        outs[c] += leaf.reshape(chunk, HG_DK)

    for c, (qraw, z, v, st_ref, masks, tri, rev) in enumerate(chains):
        st = st_ref[...]
        outs[c] += lax.dot_general(q_decs[c], st.astype(BF16), _NT, preferred_element_type=F32)
        st_ref[...] = st * tot_decs[c] + lax.dot_general(v, k_decs[c], _TN, preferred_element_type=F32)
    return outs


def _hg_kernel(*refs, seq, ctx_len, need_ctx):
    (ql_ref, il_ref, gl_ref, ffl_ref, fbl_ref, qc_ref, ic_ref, gc_ref, ffc_ref, fbc_ref,
     llb_ref, l1m_ref, oml_ref, ng_ref) = refs[:14]
    if need_ctx:
        yl_ref, yc_ref = refs[14:16]
        scratch = refs[16:]
    else:
        yl_ref = refs[14]
        yc_ref = None
        scratch = refs[15:]
    ol_ref, oc_ref, stf_ref, stb_ref = scratch
    chunk = HG_CHUNK
    loglb = llb_ref[0]
    log1mlb = l1m_ref[0]
    one_m_lb = oml_ref[0]
    norm_g = ng_ref[0]
    ti = lax.broadcasted_iota(jnp.int32, (chunk, chunk), 0)
    si = lax.broadcasted_iota(jnp.int32, (chunk, chunk), 1)

    def direction_consts(rev):
        tri = ((si >= ti) if rev else (si <= ti)).astype(F32)
        masks = []
        for n, _ in _hg_level_tables(chunk, rev):
            shift = (2 * n).bit_length() - 1
            same = jnp.right_shift(ti, shift) == jnp.right_shift(si, shift)
            t_late = jnp.bitwise_and(ti, 2 * n - 1) >= n
            s_late = jnp.bitwise_and(si, 2 * n - 1) >= n
            if rev:
                masks.append(same & jnp.logical_not(t_late) & s_late)
            else:
                masks.append(same & t_late & jnp.logical_not(s_late))
        return tri, masks

    tri_f, masks_f = direction_consts(False)
    tri_b, masks_b = direction_consts(True)

    def run(q_s, i_s, g_s, zf_s, zb_s, o_s, y_s, n_chunks):
        def step(pi, second):
            rows = []
            chains = []
            for u in range(HG_UNROLL):
                c = pi * HG_UNROLL + u
                r_f = pl.ds(pl.multiple_of(c * chunk, chunk), chunk)
                rows.append(r_f)
                chains.append((q_s[r_f, :], zf_s[r_f, :], i_s[r_f, :], stf_ref, masks_f, tri_f, False))
            for u in range(HG_UNROLL):
                c = n_chunks - 1 - (pi * HG_UNROLL + u)
                r_b = pl.ds(pl.multiple_of(c * chunk, chunk), chunk)
                rows.append(r_b)
                chains.append((q_s[r_b, :], zb_s[r_b, :], i_s[r_b, :], stb_ref, masks_b, tri_b, True))
            outs = _hg_chunks(chains, loglb, log1mlb, one_m_lb)
            for r, o in zip(rows, outs):
                if not second:
                    o_s[r, :] = o
                elif y_s is not None:
                    y = _head_rms(o_s[r, :] + o, norm_g) * _silu(g_s[r, :].astype(F32))
                    y_s[r, :] = y.astype(y_s.dtype)

        def first(pi, carry):
            step(pi, False)
            return carry

        def second(pi, carry):
            step(pi, True)
            return carry

        n_steps = n_chunks // HG_UNROLL
        lax.fori_loop(0, n_steps // 2, first, 0)
        lax.fori_loop(n_steps // 2, n_steps, second, 0)

    stf_ref[...] = jnp.zeros_like(stf_ref)
    stb_ref[...] = jnp.zeros_like(stb_ref)
    run(qc_ref, ic_ref, gc_ref, ffc_ref, fbc_ref, oc_ref, yc_ref, ctx_len // chunk)
    run(ql_ref, il_ref, gl_ref, ffl_ref, fbl_ref, ol_ref, yl_ref, seq // chunk)


def _hgrn2(proj_h, proj_f, lb, norm_g, n_batch, seq, ctx_len, need_ctx):
    w = HG_DK
    cb0 = n_batch * seq // ctx_len
    lat = lambda off: pl.BlockSpec((seq, w), lambda b, h: (b, off // w + h))
    ctx = lambda off: pl.BlockSpec((ctx_len, w), lambda b, h: (cb0 + b, off // w + h))
    vec = pl.BlockSpec((1, 1, w), lambda b, h: (h, 0, 0))
    lb3 = lb.reshape(HG_HEADS, 1, w)
    in_specs = [lat(H_HG_Q), lat(H_HG_I), lat(H_HG_G), lat(F_HG_FF), lat(F_HG_FB),
                ctx(H_HG_Q), ctx(H_HG_I), ctx(H_HG_G), ctx(F_HG_FF), ctx(F_HG_FB),
                vec, vec, vec, vec]
    out_specs = [pl.BlockSpec((seq, w), lambda b, h: (b, h))]
    out_shape = [jax.ShapeDtypeStruct((n_batch * seq, HG_W), BF16)]
    if need_ctx:
        out_specs.append(pl.BlockSpec((ctx_len, w), lambda b, h: (b, h)))
        out_shape.append(jax.ShapeDtypeStruct((n_batch * ctx_len, HG_W), BF16))
    outs = pl.pallas_call(
        functools.partial(_hg_kernel, seq=seq, ctx_len=ctx_len, need_ctx=need_ctx),
        grid=(n_batch, HG_HEADS),
        in_specs=in_specs,
        out_specs=out_specs,
        out_shape=out_shape,
        scratch_shapes=[pltpu.VMEM((seq, w), F32), pltpu.VMEM((ctx_len, w), F32),
                        pltpu.VMEM((w, w), F32), pltpu.VMEM((w, w), F32)],
        compiler_params=_params(2),
        name="hgrn2",
    )(proj_h, proj_h, proj_h, proj_f, proj_f, proj_h, proj_h, proj_h, proj_f, proj_f,
      jnp.log(lb3), jnp.log1p(-lb3), 1.0 - lb3, norm_g.reshape(HG_HEADS, 1, w))
    return outs if need_ctx else (outs[0], None)


def _router_kernel(x_ref, g_ref, sh_ref, sc_ref, wr_ref, h_ref, idx_ref, wt_ref, cnt_ref, carry_ref):
    @pl.when(pl.program_id(0) == 0)
    def _():
        carry_ref[...] = jnp.zeros_like(carry_ref)

    x = x_ref[...]
    y = x * lax.rsqrt(jnp.mean(x * x, axis=-1, keepdims=True) + EPS) * g_ref[...]
    h = y * (1.0 + sc_ref[0]) + sh_ref[0]
    h_ref[...] = h.astype(h_ref.dtype)
    logits = jnp.dot(h, wr_ref[...], precision=lax.Precision.HIGHEST, preferred_element_type=F32)
    lane = lax.broadcasted_iota(jnp.int32, logits.shape, 1).astype(F32)
    lg = jnp.where(lane < N_EXPERTS, logits, -jnp.inf)
    m1 = jnp.max(lg, axis=-1, keepdims=True)
    i1 = jnp.min(jnp.where(lg == m1, lane, float(LANE)), axis=-1, keepdims=True)
    lg2 = jnp.where(lane == i1, -jnp.inf, lg)
    m2 = jnp.max(lg2, axis=-1, keepdims=True)
    i2 = jnp.min(jnp.where(lg2 == m2, lane, float(LANE)), axis=-1, keepdims=True)
    e = jnp.exp(m2 - m1)
    w1 = 1.0 / (1.0 + e)
    w2 = e / (1.0 + e)
    wt_ref[...] = jnp.where(lane == 0, w1, jnp.where(lane == 1, w2, 0.0))
    n_tok = x.shape[0]
    routed = jnp.where(jnp.logical_or(lane == i1, lane == i2), 1.0, 0.0)
    ti = lax.broadcasted_iota(jnp.int32, (n_tok, n_tok), 0)
    ui = lax.broadcasted_iota(jnp.int32, (n_tok, n_tok), 1)
    earlier = jnp.where(ui < ti, 1.0, 0.0).astype(BF16)
    before = carry_ref[...] + jnp.dot(earlier, routed.astype(BF16), preferred_element_type=F32)
    r1 = jnp.sum(jnp.where(lane == i1, before, 0.0), axis=-1, keepdims=True)
    r2 = jnp.sum(jnp.where(lane == i2, before, 0.0), axis=-1, keepdims=True)
    carry_ref[...] += jnp.sum(routed, axis=0, keepdims=True)
    cnt_ref[...] = jnp.broadcast_to(carry_ref[...], cnt_ref.shape).astype(jnp.int32)
    idx_ref[...] = jnp.where(lane == 0, i1, jnp.where(lane == 1, i2, jnp.where(
        lane == 2, r1, jnp.where(lane == 3, r2, 0.0)))).astype(jnp.int32)


def _router(x, g, shift, scale, w_router, n_rows, seq, n_batch):
    d = x.shape[1]
    tm = 256
    tpb = seq // tm
    grp = lambda i: (_group_of(i, tpb, n_batch), 0, 0)
    wr = jnp.pad(w_router, ((0, 0), (0, LANE - N_EXPERTS)))
    return pl.pallas_call(
        _router_kernel,
        grid=(n_rows // tm,),
        in_specs=[pl.BlockSpec((tm, d), lambda i: (i, 0)),
                  pl.BlockSpec((1, d), lambda i: (0, 0)),
                  pl.BlockSpec((1, 1, d), grp),
                  pl.BlockSpec((1, 1, d), grp),
                  pl.BlockSpec((d, LANE), lambda i: (0, 0))],
        out_specs=[pl.BlockSpec((tm, d), lambda i: (i, 0)),
                   pl.BlockSpec((tm, LANE), lambda i: (i, 0)),
                   pl.BlockSpec((tm, LANE), lambda i: (i, 0)),
                   pl.BlockSpec((8, LANE), lambda i: (0, 0))],
        out_shape=[jax.ShapeDtypeStruct((n_rows, d), F32),
                   jax.ShapeDtypeStruct((n_rows, LANE), jnp.int32),
                   jax.ShapeDtypeStruct((n_rows, LANE), F32),
                   jax.ShapeDtypeStruct((8, LANE), jnp.int32)],
        scratch_shapes=[pltpu.VMEM((1, LANE), F32)],
        compiler_params=_params(1),
        name="router",
    )(x, g.reshape(1, d), shift, scale, wr)


def _row_copy(src_hbm, buf_ref, sem, src_row, r):
    return pltpu.make_async_copy(src_hbm.at[pl.ds(src_row, 1)], buf_ref.at[pl.ds(r, 1)], sem)


def _dispatch_kernel(src_ref, nu_ref, h_hbm, o_ref, buf_ref, sem):
    t = pl.program_id(0)
    tm = buf_ref.shape[0]

    @pl.when(t < nu_ref[0])
    def _():
        base = t * tm

        def start(r, carry):
            _row_copy(h_hbm, buf_ref, sem, src_ref[base + r], r).start()
            return carry

        def wait(r, carry):
            _row_copy(h_hbm, buf_ref, sem, src_ref[base + r], r).wait()
            return carry

        lax.fori_loop(0, tm, start, 0, unroll=8)
        lax.fori_loop(0, tm, wait, 0, unroll=8)
        o_ref[...] = buf_ref[...].astype(o_ref.dtype)

    @pl.when(t >= nu_ref[0])
    def _():
        o_ref[...] = jnp.zeros_like(o_ref)


def _dispatch(h, src_token, n_used):
    d = h.shape[1]
    tm = MOE_TILE
    n_pad = src_token.shape[0]
    return pl.pallas_call(
        _dispatch_kernel,
        grid_spec=pltpu.PrefetchScalarGridSpec(
            num_scalar_prefetch=2,
            grid=(n_pad // tm,),
            in_specs=[pl.BlockSpec(memory_space=pl.ANY)],
            out_specs=pl.BlockSpec((tm, d), lambda t, src, nu: (t, 0)),
            scratch_shapes=[pltpu.VMEM((tm, d), F32), pltpu.SemaphoreType.DMA]),
        out_shape=jax.ShapeDtypeStruct((n_pad, d), BF16),
        compiler_params=_params(1),
        name="moe_dispatch",
    )(src_token, n_used, h)


def _new_expert(te_ref, i):
    return jnp.logical_or(i == 0, te_ref[i] != te_ref[jnp.maximum(i - 1, 0)])


def _moe_up_kernel(te_ref, nu_ref, a_ref, wg_ref, wu_ref, o_ref, wgb_ref, wub_ref):
    i = pl.program_id(1)

    @pl.when(_new_expert(te_ref, i))
    def _():
        wgb_ref[...] = wg_ref[...].astype(BF16)
        wub_ref[...] = wu_ref[...].astype(BF16)

    @pl.when(i < nu_ref[0])
    def _():
        a = a_ref[...]
        g = jnp.dot(a, wgb_ref[...], preferred_element_type=F32)
        u = jnp.dot(a, wub_ref[...], preferred_element_type=F32)
        o_ref[...] = (_silu(g) * u).astype(o_ref.dtype)

    @pl.when(i >= nu_ref[0])
    def _():
        o_ref[...] = jnp.zeros_like(o_ref)


def _moe_down_kernel(te_ref, nu_ref, a_ref, w_ref, o_ref, wb_ref):
    i = pl.program_id(1)

    @pl.when(_new_expert(te_ref, i))
    def _():
        wb_ref[...] = w_ref[...].astype(BF16)

    @pl.when(i < nu_ref[0])
    def _():
        o_ref[...] = jnp.dot(a_ref[...], wb_ref[...], preferred_element_type=F32).astype(o_ref.dtype)

    @pl.when(i >= nu_ref[0])
    def _():
        o_ref[...] = jnp.zeros_like(o_ref)


def _moe_grouped(a, weights, tile_expert, n_used, kernel, out_dtype, name):
    n_rows, k = a.shape
    n = weights[0].shape[2]
    tm, tn = MOE_TILE, COL_TILE
    a_map = lambda j, i, te, nu: (jnp.minimum(i, nu[0] - 1), 0)
    w_spec = pl.BlockSpec((None, k, tn), lambda j, i, te, nu: (te[i], 0, j))
    return pl.pallas_call(
        kernel,
        grid_spec=pltpu.PrefetchScalarGridSpec(
            num_scalar_prefetch=2,
            grid=(n // tn, n_rows // tm),
            in_specs=[pl.BlockSpec((tm, k), a_map)] + [w_spec] * len(weights),
            out_specs=pl.BlockSpec((tm, tn), lambda j, i, te, nu: (i, j)),
            scratch_shapes=[pltpu.VMEM((k, tn), BF16)] * len(weights)),
        out_shape=jax.ShapeDtypeStruct((n_rows, n), out_dtype),
        compiler_params=_params(2),
        name=name,
    )(tile_expert, n_used, a, *weights)


def _combine_kernel(x_ref, ya_ref, yb_ref, wt_ref, g_ref, o_ref):
    wt = wt_ref[...]
    y = wt[:, 0:1] * ya_ref[...].astype(F32) + wt[:, 1:2] * yb_ref[...].astype(F32)
    o_ref[...] = x_ref[...] + g_ref[0] * y


def _moe_combine(x, ya, yb, wt, gate, n_rows, seq, n_batch):
    d = x.shape[1]
    tm = 256
    tpb = seq // tm
    row = pl.BlockSpec((tm, d), lambda i: (i, 0))
    return pl.pallas_call(
        _combine_kernel,
        grid=(n_rows // tm,),
        in_specs=[row, row, row,
                  pl.BlockSpec((tm, LANE), lambda i: (i, 0)),
                  pl.BlockSpec((1, 1, d), lambda i: (_group_of(i, tpb, n_batch), 0, 0))],
        out_specs=row,
        out_shape=jax.ShapeDtypeStruct((n_rows, d), F32),
        compiler_params=_params(1),
        name="moe_combine",
    )(x, ya, yb, wt, gate)


def _moe_ffn(x, g, shift, scale, gate, w_router, wg, wu, wd, n_rows, seq, n_batch):
    h, idx, wt, cnt = _router(x, g, shift, scale, w_router, n_rows, seq, n_batch)
    tm = MOE_TILE
    n_tiles = 2 * n_rows // tm + N_EXPERTS
    counts = cnt[0, :N_EXPERTS]
    tiles = (counts + tm - 1) // tm
    tile_end = jnp.cumsum(tiles)
    tile_start = tile_end - tiles
    expert = jnp.concatenate([idx[:, 0], idx[:, 1]])
    rank = jnp.concatenate([idx[:, 2], idx[:, 3]])
    token = jnp.concatenate([jnp.arange(n_rows, dtype=jnp.int32)] * 2)
    one_hot = (expert[:, None] == jnp.arange(N_EXPERTS, dtype=jnp.int32)[None, :]).astype(jnp.int32)
    dest = jnp.sum(tile_start[None, :] * one_hot, axis=1) * tm + rank
    src_token = jnp.zeros((n_tiles * tm,), jnp.int32).at[dest].set(token)
    n_used = tile_end[-1:].astype(jnp.int32)
    tile_id = jnp.minimum(jnp.arange(n_tiles, dtype=jnp.int32), n_used[0] - 1)
    tile_expert = jnp.sum((tile_end[None, :] <= tile_id[:, None]).astype(jnp.int32), axis=1)

    a_sorted = _dispatch(h, src_token, n_used)
    u = _moe_grouped(a_sorted, (wg, wu), tile_expert, n_used, _moe_up_kernel, BF16, "moe_up")
    y = _moe_grouped(u, (wd,), tile_expert, n_used, _moe_down_kernel, BF16, "moe_down")
    ya = jnp.take(y, dest[:n_rows], axis=0, mode="clip")
    yb = jnp.take(y, dest[n_rows:], axis=0, mode="clip")
    return _moe_combine(x, ya, yb, wt, gate, n_rows, seq, n_batch)


def _rope_tables(n_batch, seq, n_ctx_rows):
    t = jnp.arange(seq)
    pos = jnp.stack([t // GRID_W, t % GRID_W], axis=-1).astype(F32)
    inv_freq = jnp.exp(-jnp.log(ROPE_BASE) * jnp.arange(0, ROPE_AXIS_DIM, 2, dtype=F32) / ROPE_AXIS_DIM)
    ang = pos[..., None] * inv_freq
    cos, sin = jnp.cos(ang), jnp.sin(ang)
    cos = jnp.concatenate([cos[:, 0], cos[:, 0], cos[:, 1], cos[:, 1]], axis=-1)
    sin = jnp.concatenate([-sin[:, 0], sin[:, 0], -sin[:, 1], sin[:, 1]], axis=-1)
    cos = jnp.concatenate([jnp.tile(cos, (n_batch, 1)), jnp.ones((n_ctx_rows, HEAD_DIM), F32)], axis=0)
    sin = jnp.concatenate([jnp.tile(sin, (n_batch, 1)), jnp.zeros((n_ctx_rows, HEAD_DIM), F32)], axis=0)
    return cos, sin


def _split_w_in(w):
    w_h = jnp.concatenate([w[:, :_O_ML_GATES], w[:, _O_HG_Q:_O_HG_FF], w[:, _O_HG_I:]], axis=1)
    w_f = jnp.concatenate([w[:, _O_HG_FF:_O_HG_I], w[:, _O_ML_GATES:_O_HG_Q],
                           jnp.zeros((w.shape[0], F_WIDTH - F_ML_G - 4 * ML_HEADS), w.dtype)], axis=1)
    return w_h.astype(BF16), w_f.astype(BF16)


def kernel(x, c, ctx, c_ctx, w_mod, b_mod, norm1_g, norm2_g, w_in, attn_q_norm_g, attn_k_norm_g, ml_conv_w,
           ml_gate_b, ml_norm_g, hg_lb_logits, hg_norm_g, w_branch, w_out, ffn_w_gate, ffn_w_up, ffn_w_down,
           moe_w_router, moe_w_gate, moe_w_up, moe_w_down):
    n_batch, seq, d = x.shape
    ctx_len = ctx.shape[1]
    depth = w_mod.shape[0]
    m_lat = n_batch * seq
    m_ctx = n_batch * ctx_len
    m_all = m_lat + m_ctx
    assert d == D_MODEL and w_in.shape[2] == _N_IN
    assert seq % ROW_TILE == 0 and m_ctx % ROW_TILE == 0 and seq % ctx_len == 0
    assert seq % ML_CHUNK == 0 and ctx_len % ML_CHUNK == 0 and ctx_len % HG_CHUNK == 0

    xa = jnp.concatenate([x.reshape(m_lat, d), ctx.reshape(m_ctx, d)], axis=0)
    cs = jnp.concatenate([c, c_ctx[None], jnp.zeros((8 - n_batch - 1, d), F32)], axis=0)
    mod = _modulation(cs, w_mod, b_mod)
    cos, sin = _rope_tables(n_batch, seq, m_ctx)
    lb_all = jnp.cumsum(jax.nn.softmax(hg_lb_logits.astype(F32), axis=0), axis=0)
    lb_all = lb_all - lb_all[:1]

    for l in range(depth):
        need_ctx = l < depth - 1
        n_rows = m_all if need_ctx else m_lat
        mods = [mod[l, :n_batch + 1, i * d:(i + 1) * d].reshape(n_batch + 1, 1, d) for i in range(N_MOD)]
        w_h, w_f = _split_w_in(w_in[l])

        h = _norm_mod(xa, norm1_g[l], mods[0], mods[1], m_all, seq, n_batch)
        proj_h = _matmul(h, w_h, BF16, COL_TILE)
        proj_f = _matmul(h, w_f, F32, F_WIDTH // 3)

        q_rot, k_rot = _att_prep(proj_h, cos, sin, attn_q_norm_g[l], attn_k_norm_g[l])
        ya = _attention(q_rot, k_rot, proj_h, n_batch, seq, ctx_len, True)

        gates = proj_f[:, F_ML_G:F_ML_G + 4 * ML_HEADS].reshape(m_all, 4, ML_HEADS)
        gates_c = gates.transpose(2, 0, 1)
        gates_r = gates.transpose(2, 1, 0)
        ym, ym_c = _mlstm(proj_h, gates_c, gates_r, ml_conv_w[l], ml_gate_b[l], ml_norm_g[l],
                          n_batch, seq, ctx_len, need_ctx)
        yh, yh_c = _hgrn2(proj_h, proj_f, lb_all[l], hg_norm_g[l], n_batch, seq, ctx_len, need_ctx)
        ys_ctx = None
        if need_ctx:
            ya_c = _attention(q_rot, k_rot, proj_h, n_batch, seq, ctx_len, False)
            ys_ctx = (ya_c, ym_c, yh_c)

        merged = _branch_merge((ya, ym, yh), ys_ctx, w_branch[l], proj_h, n_rows)
        xa = _matmul_residual(merged, w_out[l], xa, mods[2], n_rows, seq, n_batch, "proj_out")

        if l % 2 == 0:
            h2 = _norm_mod(xa, norm2_g[l], mods[3], mods[4], n_rows, seq, n_batch)
            u = _glu(h2, ffn_w_gate[l // 2], ffn_w_up[l // 2])
            xa = _matmul_residual(u, ffn_w_down[l // 2], xa, mods[5], n_rows, seq, n_batch, "ffn_down")
        else:
            xa = _moe_ffn(xa, norm2_g[l], mods[3], mods[4], mods[5], moe_w_router[l // 2],
                          moe_w_gate[l // 2], moe_w_up[l // 2], moe_w_down[l // 2], n_rows, seq, n_batch)
    return xa[:m_lat].reshape(n_batch, seq, d)
```

```python
import functools

import jax
import jax.numpy as jnp
from jax import lax
from jax.experimental import pallas as pl
from jax.experimental.pallas import tpu as pltpu

F32 = jnp.float32
BF16 = jnp.bfloat16

D_MODEL = 2048
GRID_W = 64
N_MOD = 6
ATT_HEADS = 8
ATT_KV_HEADS = 2
HEAD_DIM = 128
ATT_W = ATT_HEADS * HEAD_DIM
KV_W = ATT_KV_HEADS * HEAD_DIM
ROPE_AXIS_DIM = HEAD_DIM // 2
ROPE_BASE = 10000.0
ML_HEADS = 4
ML_DK = 256
ML_W = ML_HEADS * ML_DK
HG_HEADS = 8
HG_DK = 128
HG_W = HG_HEADS * HG_DK
D_FF = 5632
N_EXPERTS = 8
EPS = 1e-6
NEG_INIT = -1e30
LOG2E = 1.4426950408889634

_O_ATT_Q = 0
_O_ML_GATES = ATT_W + 2 * KV_W + 4 * ML_W
_O_HG_Q = _O_ML_GATES + 4 * ML_HEADS
_O_HG_FF = _O_HG_Q + HG_W
_O_HG_I = _O_HG_FF + 2 * HG_W
_O_MERGE = _O_HG_I + 2 * HG_W
_N_IN = _O_MERGE + 3 * D_MODEL

H_ATT_Q = 0
H_ATT_K = 1024
H_ATT_V = 1280
H_ML_Q = 1536
H_ML_K = 2560
H_ML_V = 3584
H_ML_O = 4608
H_HG_Q = 5632
H_HG_I = 6656
H_HG_G = 7680
H_MERGE = 8704
H_WIDTH = 14848
F_HG_FF = 0
F_HG_FB = 1024
F_ML_G = 2048

LANE = 128
ROW_TILE = 512
COL_TILE = 512
PROJ_ROW_TILES = 4
W_MISALIGN = _O_HG_Q % LANE


def _source_tiles(segments):
    blocks, shifted = [], []
    for off, width in segments:
        for start in range(off, off + width, COL_TILE):
            assert start % COL_TILE in (0, W_MISALIGN)
            blocks.append(start // COL_TILE)
            shifted.append(int(start % COL_TILE != 0))
    return blocks, shifted


H_TILES = _source_tiles([(0, _O_ML_GATES), (_O_HG_Q, HG_W), (_O_HG_I, 2 * HG_W + 3 * D_MODEL)])
F_TILES = _source_tiles([(_O_HG_FF, 2 * HG_W), (_O_ML_GATES, COL_TILE)])
VMEM_LIMIT = 56 * 1024 * 1024

ML_CHUNK = 256
HG_CHUNK = 64
HG_LEAF = 8
HG_UNROLL = 2
ATT_TQ = 256
MOE_TILE = 512

_NT = (((1,), (1,)), ((), ()))
_TN = (((0,), (0,)), ((), ()))


def _params(n_axes):
    return pltpu.CompilerParams(dimension_semantics=("arbitrary",) * n_axes,
                                vmem_limit_bytes=VMEM_LIMIT)


def _silu(x):
    return x * jax.nn.sigmoid(x)


def _log_sigmoid(x):
    return jnp.minimum(x, 0.0) - jnp.log1p(jnp.exp(-jnp.abs(x)))


def _mod_kernel(c_ref, w_ref, b_ref, o_ref):
    a = _silu(c_ref[...])
    o_ref[...] = jnp.dot(a.astype(BF16), w_ref[...].astype(BF16), preferred_element_type=F32) + b_ref[...]


def _modulation(cs, w_mod, b_mod):
    depth, d, n = w_mod.shape
    tn = 1024
    return pl.pallas_call(
        _mod_kernel,
        grid=(depth, n // tn),
        in_specs=[pl.BlockSpec((8, d), lambda l, j: (0, 0)),
                  pl.BlockSpec((None, d, tn), lambda l, j: (l, 0, j)),
                  pl.BlockSpec((None, 1, tn), lambda l, j: (l, 0, j))],
        out_specs=pl.BlockSpec((None, 8, tn), lambda l, j: (l, 0, j)),
        out_shape=jax.ShapeDtypeStruct((depth, 8, n), F32),
        compiler_params=_params(2),
        name="modulation",
    )(cs, w_mod, b_mod.reshape(depth, 1, n))


def _norm_kernel(x_ref, g_ref, sh_ref, sc_ref, o_ref):
    x = x_ref[...]
    y = x * lax.rsqrt(jnp.mean(x * x, axis=-1, keepdims=True) + EPS)
    y = y * g_ref[...]
    o_ref[...] = (y * (1.0 + sc_ref[0]) + sh_ref[0]).astype(o_ref.dtype)


def _group_of(i, tiles_per_batch, n_batch):
    return jnp.minimum(i // tiles_per_batch, n_batch)


def _norm_mod(x, g, shift, scale, n_rows, seq, n_batch):
    d = x.shape[1]
    tm = 256
    tpb = seq // tm
    grp = lambda i: (_group_of(i, tpb, n_batch), 0, 0)
    return pl.pallas_call(
        _norm_kernel,
        grid=(n_rows // tm,),
        in_specs=[pl.BlockSpec((tm, d), lambda i: (i, 0)),
                  pl.BlockSpec((1, d), lambda i: (0, 0)),
                  pl.BlockSpec((1, 1, d), grp),
                  pl.BlockSpec((1, 1, d), grp)],
        out_specs=pl.BlockSpec((tm, d), lambda i: (i, 0)),
        out_shape=jax.ShapeDtypeStruct((n_rows, d), BF16),
        compiler_params=_params(1),
        name="norm_mod",
    )(x, g.reshape(1, d), shift, scale)


def _proj_in_kernel(blk_ref, shift_ref, a_ref, wa_ref, wb_ref, o_ref, w_scr):
    j = pl.program_id(0)

    @pl.when(pl.program_id(1) == 0)
    def _():
        @pl.when(shift_ref[j] == 0)
        def _():
            w_scr[...] = wa_ref[...].astype(BF16)

        @pl.when(shift_ref[j] != 0)
        def _():
            w = jnp.concatenate([wa_ref[:, W_MISALIGN:], wb_ref[:, :W_MISALIGN]], axis=1)
            w_scr[...] = w.astype(BF16)

    o_ref[...] = jnp.dot(a_ref[...], w_scr[...], preferred_element_type=F32).astype(o_ref.dtype)


def _proj_in(a, w_in, layer, src_blocks, shifted, out_dtype):
    m, k = a.shape
    n_src = w_in.shape[2]
    tm, tn = m // PROJ_ROW_TILES, COL_TILE
    n_tiles = len(src_blocks)
    last_lane_blk = (n_src - 1) // LANE
    per = tn // LANE
    return pl.pallas_call(
        _proj_in_kernel,
        grid_spec=pltpu.PrefetchScalarGridSpec(
            num_scalar_prefetch=2,
            grid=(n_tiles, m // tm),
            in_specs=[pl.BlockSpec((tm, k), lambda j, i, blk, sh: (i, 0)),
                      pl.BlockSpec((None, k, tn), lambda j, i, blk, sh: (layer, 0, blk[j])),
                      pl.BlockSpec((None, k, LANE), lambda j, i, blk, sh: (
                          layer, 0, jnp.minimum(per * (blk[j] + 1), last_lane_blk)))],
            out_specs=pl.BlockSpec((tm, tn), lambda j, i, blk, sh: (i, j)),
            scratch_shapes=[pltpu.VMEM((k, tn), BF16)]),
        out_shape=jax.ShapeDtypeStruct((m, n_tiles * tn), out_dtype),
        compiler_params=_params(2),
        name="proj_in",
    )(jnp.asarray(src_blocks, jnp.int32), jnp.asarray(shifted, jnp.int32), a, w_in, w_in)


def _mm_res_kernel(a_ref, w_ref, r_ref, g_ref, o_ref, wb_ref):
    @pl.when(pl.program_id(1) == 0)
    def _():
        wb_ref[...] = w_ref[...].astype(BF16)

    y = jnp.dot(a_ref[...], wb_ref[...], preferred_element_type=F32)
    o_ref[...] = r_ref[...] + g_ref[0] * y


def _matmul_residual(a, w, layer, res, gate, n_rows, seq, n_batch, name):
    k = a.shape[1]
    n = w.shape[2]
    tm, tn = ROW_TILE, COL_TILE
    tpb = seq // tm
    return pl.pallas_call(
        _mm_res_kernel,
        grid=(n // tn, n_rows // tm),
        in_specs=[pl.BlockSpec((tm, k), lambda j, i: (i, 0)),
                  pl.BlockSpec((None, k, tn), lambda j, i: (layer, 0, j)),
                  pl.BlockSpec((tm, tn), lambda j, i: (i, j)),
                  pl.BlockSpec((1, 1, tn), lambda j, i: (_group_of(i, tpb, n_batch), 0, j))],
        out_specs=pl.BlockSpec((tm, tn), lambda j, i: (i, j)),
        out_shape=jax.ShapeDtypeStruct((n_rows, n), F32),
        scratch_shapes=[pltpu.VMEM((k, tn), BF16)],
        compiler_params=_params(2),
        name=name,
    )(a, w, res, gate)


def _glu_kernel(a_ref, wg_ref, wu_ref, o_ref, wgb_ref, wub_ref):
    @pl.when(pl.program_id(1) == 0)
    def _():
        wgb_ref[...] = wg_ref[...].astype(BF16)
        wub_ref[...] = wu_ref[...].astype(BF16)

    a = a_ref[...]
    g = jnp.dot(a, wgb_ref[...], preferred_element_type=F32)
    u = jnp.dot(a, wub_ref[...], preferred_element_type=F32)
    o_ref[...] = (_silu(g) * u).astype(o_ref.dtype)


def _glu(a, wg, wu, layer):
    m, k = a.shape
    n = wg.shape[2]
    tm, tn = ROW_TILE, COL_TILE
    return pl.pallas_call(
        _glu_kernel,
        grid=(n // tn, m // tm),
        in_specs=[pl.BlockSpec((tm, k), lambda j, i: (i, 0)),
                  pl.BlockSpec((None, k, tn), lambda j, i: (layer, 0, j)),
                  pl.BlockSpec((None, k, tn), lambda j, i: (layer, 0, j))],
        out_specs=pl.BlockSpec((tm, tn), lambda j, i: (i, j)),
        out_shape=jax.ShapeDtypeStruct((m, n), BF16),
        scratch_shapes=[pltpu.VMEM((k, tn), BF16), pltpu.VMEM((k, tn), BF16)],
        compiler_params=_params(2),
        name="ffn_up",
    )(a, wg, wu)


def _merge_kernel(*refs, n_lat_tiles, with_ctx):
    if with_ctx:
        lat_refs, ctx_refs, rest = refs[0:3], refs[3:6], refs[6:]
    else:
        lat_refs, ctx_refs, rest = refs[0:3], None, refs[3:]
    w_ref, g0_ref, g1_ref, g2_ref, o_ref, wb_ref = rest

    @pl.when(pl.program_id(1) == 0)
    def _():
        wb_ref[...] = w_ref[...].astype(BF16)

    is_ctx = pl.program_id(1) >= n_lat_tiles
    acc = None
    for b, g_ref in enumerate((g0_ref, g1_ref, g2_ref)):
        y = lat_refs[b][...]
        if with_ctx:
            y = jnp.where(is_ctx, ctx_refs[b][...], y)
        term = jax.nn.sigmoid(g_ref[...].astype(F32)) * jnp.dot(y, wb_ref[b], preferred_element_type=F32)
        acc = term if acc is None else acc + term
    o_ref[...] = acc.astype(o_ref.dtype)


def _branch_merge(ys_lat, ys_ctx, w_branch, layer, proj_h, n_rows):
    k = ys_lat[0].shape[1]
    n = w_branch.shape[3]
    tm, tn = ROW_TILE, COL_TILE
    n_lat_tiles = ys_lat[0].shape[0] // tm
    with_ctx = ys_ctx is not None
    gate_spec = lambda b: pl.BlockSpec((tm, tn), lambda j, i: (i, (H_MERGE + b * n) // tn + j))
    lat_spec = pl.BlockSpec((tm, k), lambda j, i: (jnp.minimum(i, n_lat_tiles - 1), 0))
    ctx_spec = pl.BlockSpec((tm, k), lambda j, i: (jnp.maximum(i - n_lat_tiles, 0), 0))
    in_specs = [lat_spec] * 3 + ([ctx_spec] * 3 if with_ctx else [])
    in_specs += [pl.BlockSpec((None, 3, k, tn), lambda j, i: (layer, 0, 0, j)),
                 gate_spec(0), gate_spec(1), gate_spec(2)]
    args = list(ys_lat) + (list(ys_ctx) if with_ctx else [])
    return pl.pallas_call(
        functools.partial(_merge_kernel, n_lat_tiles=n_lat_tiles, with_ctx=with_ctx),
        grid=(n // tn, n_rows // tm),
        in_specs=in_specs,
        out_specs=pl.BlockSpec((tm, tn), lambda j, i: (i, j)),
        out_shape=jax.ShapeDtypeStruct((n_rows, n), BF16),
        scratch_shapes=[pltpu.VMEM((3, k, tn), BF16)],
        compiler_params=_params(2),
        name="branch_merge",
    )(*args, w_branch, proj_h, proj_h, proj_h)


def _head_rms(x, g):
    return x * lax.rsqrt(jnp.mean(x * x, axis=-1, keepdims=True) + EPS) * g


def _rope(x, cos, sin, low_half):
    partner = jnp.where(low_half, pltpu.roll(x, 96, axis=1), pltpu.roll(x, 32, axis=1))
    return x * cos + partner * sin


def _att_prep_kernel(q_ref, k_ref, cos_ref, sin_ref, qg_ref, kg_ref, qo_ref, ko_ref):
    cos = cos_ref[...]
    sin = sin_ref[...]
    lane = lax.broadcasted_iota(jnp.int32, cos.shape, 1)
    low_half = jnp.bitwise_and(lane, 63) < 32
    scale = HEAD_DIM ** -0.5 * LOG2E
    for h in range(ATT_HEADS):
        sl = slice(h * HEAD_DIM, (h + 1) * HEAD_DIM)
        x = _head_rms(q_ref[:, sl].astype(F32), qg_ref[...])
        qo_ref[:, sl] = (_rope(x, cos, sin, low_half) * scale).astype(qo_ref.dtype)
    for h in range(ATT_KV_HEADS):
        sl = slice(h * HEAD_DIM, (h + 1) * HEAD_DIM)
        x = _head_rms(k_ref[:, sl].astype(F32), kg_ref[...])
        ko_ref[:, sl] = _rope(x, cos, sin, low_half).astype(ko_ref.dtype)


def _att_prep(proj_h, cos, sin, qg, kg):
    m = proj_h.shape[0]
    tm = ROW_TILE
    return pl.pallas_call(
        _att_prep_kernel,
        grid=(m // tm,),
        in_specs=[pl.BlockSpec((tm, ATT_W), lambda i: (i, H_ATT_Q // ATT_W)),
                  pl.BlockSpec((tm, KV_W), lambda i: (i, H_ATT_K // KV_W)),
                  pl.BlockSpec((tm, HEAD_DIM), lambda i: (i, 0)),
                  pl.BlockSpec((tm, HEAD_DIM), lambda i: (i, 0)),
                  pl.BlockSpec((1, HEAD_DIM), lambda i: (0, 0)),
                  pl.BlockSpec((1, HEAD_DIM), lambda i: (0, 0))],
        out_specs=[pl.BlockSpec((tm, ATT_W), lambda i: (i, 0)),
                   pl.BlockSpec((tm, KV_W), lambda i: (i, 0))],
        out_shape=[jax.ShapeDtypeStruct((m, ATT_W), BF16),
                   jax.ShapeDtypeStruct((m, KV_W), BF16)],
        compiler_params=_params(1),
        name="att_prep",
    )(proj_h, proj_h, cos, sin, qg.reshape(1, HEAD_DIM), kg.reshape(1, HEAD_DIM))


def _att_kernel(*refs, with_latent):
    if with_latent:
        q_ref, kc_ref, vc_ref, kl_ref, vl_ref, o_ref = refs
    else:
        q_ref, kc_ref, vc_ref, o_ref = refs
    grp = ATT_HEADS // ATT_KV_HEADS
    for g in range(grp):
        sl = slice(g * HEAD_DIM, (g + 1) * HEAD_DIM)
        q = q_ref[:, sl]
        sc = lax.dot_general(q, kc_ref[...], _NT, preferred_element_type=F32)
        m = jnp.max(sc, axis=-1, keepdims=True)
        if with_latent:
            sl_ = lax.dot_general(q, kl_ref[...], _NT, preferred_element_type=F32)
            m = jnp.maximum(m, jnp.max(sl_, axis=-1, keepdims=True))
        pc = jnp.exp2(sc - m)
        den = jnp.sum(pc, axis=-1, keepdims=True)
        acc = jnp.dot(pc.astype(BF16), vc_ref[...], preferred_element_type=F32)
        if with_latent:
            pl_ = jnp.exp2(sl_ - m)
            den += jnp.sum(pl_, axis=-1, keepdims=True)
            acc += jnp.dot(pl_.astype(BF16), vl_ref[...], preferred_element_type=F32)
        o_ref[:, sl] = (acc / den).astype(o_ref.dtype)


def _attention(q_rot, k_rot, proj_h, n_batch, seq, ctx_len, latent_queries):
    grp_w = ATT_W // ATT_KV_HEADS
    ctx_blk0 = n_batch * seq // ctx_len
    v_col = H_ATT_V // HEAD_DIM
    if latent_queries:
        tq = ATT_TQ
        nq = seq // tq
        q_map = lambda b, k, i: (b * nq + i, k)
        n_out = n_batch * seq
    else:
        tq = ctx_len
        nq = 1
        q_map = lambda b, k, i: (ctx_blk0 + b, k)
        n_out = n_batch * ctx_len
    in_specs = [pl.BlockSpec((tq, grp_w), q_map),
                pl.BlockSpec((ctx_len, HEAD_DIM), lambda b, k, i: (ctx_blk0 + b, k)),
                pl.BlockSpec((ctx_len, HEAD_DIM), lambda b, k, i: (ctx_blk0 + b, v_col + k))]
    args = [q_rot, k_rot, proj_h]
    if latent_queries:
        in_specs += [pl.BlockSpec((seq, HEAD_DIM), lambda b, k, i: (b, k)),
                     pl.BlockSpec((seq, HEAD_DIM), lambda b, k, i: (b, v_col + k))]
        args += [k_rot, proj_h]
    return pl.pallas_call(
        functools.partial(_att_kernel, with_latent=latent_queries),
        grid=(n_batch, ATT_KV_HEADS, nq),
        in_specs=in_specs,
        out_specs=pl.BlockSpec((tq, grp_w), lambda b, k, i: (b * nq + i, k)),
        out_shape=jax.ShapeDtypeStruct((n_out, ATT_W), BF16),
        compiler_params=_params(3),
        name="attention_lat" if latent_queries else "attention_ctx",
    )(*args)


def _conv3(x_ref, w, c, n_chunks, chunk):
    r0 = pl.multiple_of(c * chunk, chunk)
    x = x_ref[pl.ds(r0, chunk), :].astype(F32)
    prev_blk = x_ref[pl.ds(pl.multiple_of(jnp.maximum(r0 - 16, 0), 16), 16), :].astype(F32)
    next_blk = x_ref[pl.ds(pl.multiple_of(jnp.minimum(r0 + chunk, (n_chunks - 1) * chunk), 16), 16), :].astype(F32)
    prev_row = jnp.where(c > 0, prev_blk[15:16, :], 0.0)
    next_row = jnp.where(c < n_chunks - 1, next_blk[0:1, :], 0.0)
    row = lax.broadcasted_iota(jnp.int32, x.shape, 0)
    x_m1 = jnp.where(row == 0, prev_row, pltpu.roll(x, 1, axis=0))
    x_p1 = jnp.where(row == chunk - 1, next_row, pltpu.roll(x, chunk - 1, axis=0))
    return x_m1 * w[0:1, :] + x * w[1:2, :] + x_p1 * w[2:3, :]


def _ml_chunks(chains):
    n_tok = chains[0][0].shape[0]
    ti = lax.broadcasted_iota(jnp.int32, (n_tok, n_tok), 0)
    si = lax.broadcasted_iota(jnp.int32, (n_tok, n_tok), 1)
    gate = []
    for q, k, v, lfr, lir, lfc, lic, c_ref, n_ref, m_st, rev in chains:
        causal = (si >= ti) if rev else (si <= ti)
        causal_t = (ti >= si) if rev else (ti <= si)
        fc_col = jnp.sum(jnp.where(causal, lfr, 0.0), axis=1, keepdims=True)
        fc_row = jnp.sum(jnp.where(causal_t, lfc, 0.0), axis=0, keepdims=True)
        f_tot = jnp.sum(lfr, axis=1, keepdims=True)
        dmat = jnp.where(causal, fc_col - fc_row + lir, -jnp.inf)
        inter = fc_col + m_st
        m_t = jnp.maximum(inter, jnp.max(dmat, axis=1, keepdims=True))
        w = jnp.exp(dmat - m_t)
        a = jnp.exp(inter - m_t)
        g_col = f_tot - fc_col + lic
        g_row = f_tot - fc_row + lir
        m_new = jnp.maximum(f_tot + m_st, jnp.max(g_row, axis=1, keepdims=True))
        decay = jnp.exp(f_tot + m_st - m_new)
        kw = k.astype(F32) * jnp.exp(g_col - m_new)
        gate.append((w, a, m_t, m_new, decay, kw))
    scs = [lax.dot_general(ch[0], ch[1], _NT, preferred_element_type=F32) * g[0]
           for ch, g in zip(chains, gate)]
    nums = [jnp.dot(sc.astype(BF16), ch[2], preferred_element_type=F32) for ch, sc in zip(chains, scs)]
    carried = [jnp.dot(ch[0], ch[7][...].astype(BF16), preferred_element_type=F32) for ch in chains]
    hs = []
    for ch, g, sc, num, car in zip(chains, gate, scs, nums, carried):
        q, n_ref = ch[0], ch[8]
        w, a, m_t, m_new, decay, kw = g
        den = jnp.sum(sc, axis=1, keepdims=True)
        den += a * jnp.sum(q.astype(F32) * n_ref[...], axis=1, keepdims=True)
        hs.append((num + a * car) / jnp.maximum(jnp.abs(den), jnp.exp(-m_t)))
    for ch, g in zip(chains, gate):
        v, c_ref, n_ref = ch[2], ch[7], ch[8]
        w, a, m_t, m_new, decay, kw = g
        c_ref[...] = decay * c_ref[...] + lax.dot_general(kw.astype(BF16), v, _TN, preferred_element_type=F32)
        n_ref[...] = decay * n_ref[...] + jnp.sum(kw, axis=0, keepdims=True)
    return hs, [g[3] for g in gate]


def _ml_kernel(*refs, seq, ctx_len, need_ctx):
    (ql_ref, kl_ref, vl_ref, ol_ref, qc_ref, kc_ref, vc_ref, oc_ref,
     wq_ref, wk_ref, gcl_ref, grl_ref, gcc_ref, grc_ref, bc_ref, br_ref, ng_ref) = refs[:17]
    if need_ctx:
        yl_ref, yc_ref = refs[17:19]
        scratch = refs[19:]
    else:
        yl_ref = refs[17]
        yc_ref = None
        scratch = refs[18:]
    qsl_ref, ksl_ref, qsc_ref, ksc_ref, hl_ref, hc_ref, cf_ref, nf_ref, cb_ref, nb_ref = scratch
    chunk = ML_CHUNK
    k_scale = ML_DK ** -0.5
    wq = wq_ref[0]
    wk = wk_ref[0]

    def conv_pass(src_q, src_k, dst_q, dst_k, n_chunks):
        def body(c, carry):
            r0 = pl.multiple_of(c * chunk, chunk)
            dst_q[pl.ds(r0, chunk), :] = _conv3(src_q, wq, c, n_chunks, chunk).astype(BF16)
            dst_k[pl.ds(r0, chunk), :] = (_conv3(src_k, wk, c, n_chunks, chunk) * k_scale).astype(BF16)
            return carry
        lax.fori_loop(0, n_chunks, body, 0)

    conv_pass(qc_ref, kc_ref, qsc_ref, ksc_ref, ctx_len // chunk)
    conv_pass(ql_ref, kl_ref, qsl_ref, ksl_ref, seq // chunk)

    bias_c = bc_ref[0]
    bias_r = br_ref[0]
    norm_g = ng_ref[0]

    def run(q_s, k_s, v_s, o_s, gc_s, gr_s, h_s, y_s, n_chunks, m_states):
        def chain(c, rev, c_r, n_r, m_st):
            d = 1 if rev else 0
            rows = pl.ds(pl.multiple_of(c * chunk, chunk), chunk)
            gc = gc_s[0, rows, :] + bias_c
            gr = gr_s[0, :, rows] + bias_r
            lic = gc[:, d:d + 1]
            lfc = _log_sigmoid(gc[:, 2 + d:3 + d])
            lir = gr[d:d + 1, :]
            lfr = _log_sigmoid(gr[2 + d:3 + d, :])
            return rows, (q_s[rows, :], k_s[rows, :], v_s[rows, :], lfr, lir, lfc, lic, c_r, n_r, m_st, rev)

        def finish(rows, h_sum):
            if y_s is not None:
                y = _head_rms(h_sum, norm_g) * jax.nn.sigmoid(o_s[rows, :].astype(F32))
                y_s[rows, :] = y.astype(y_s.dtype)

        def step(ci, m_states, phase):
            rows_f, chain_f = chain(ci, False, cf_ref, nf_ref, m_states[0])
            rows_b, chain_b = chain(n_chunks - 1 - ci, True, cb_ref, nb_ref, m_states[1])
            (h_f, h_b), m_new = _ml_chunks([chain_f, chain_b])
            if phase == 0:
                h_s[rows_f, :] = h_f
                h_s[rows_b, :] = h_b
            elif phase == 1:
                finish(rows_f, h_f + h_b)
            else:
                finish(rows_f, h_s[rows_f, :] + h_f)
                finish(rows_b, h_s[rows_b, :] + h_b)
            return tuple(m_new)

        half = n_chunks // 2
        m_states = lax.fori_loop(0, half, lambda ci, m: step(ci, m, 0), m_states)
        if n_chunks % 2:
            m_states = step(half, m_states, 1)
        return lax.fori_loop(n_chunks - half, n_chunks, lambda ci, m: step(ci, m, 2), m_states)

    for ref in (cf_ref, nf_ref, cb_ref, nb_ref):
        ref[...] = jnp.zeros_like(ref)
    m0 = jnp.full((1, 1), NEG_INIT, F32)
    m1 = run(qsc_ref, ksc_ref, vc_ref, oc_ref, gcc_ref, grc_ref, hc_ref, yc_ref, ctx_len // chunk, (m0, m0))
    run(qsl_ref, ksl_ref, vl_ref, ol_ref, gcl_ref, grl_ref, hl_ref, yl_ref, seq // chunk, m1)


def _mlstm(proj_h, gates_c, gates_r, conv_w, gate_b, norm_g, n_batch, seq, ctx_len, need_ctx):
    w = ML_DK
    cb0 = n_batch * seq // ctx_len
    col = lambda off: off // w
    lat = lambda off: pl.BlockSpec((seq, w), lambda b, h: (b, col(off) + h))
    ctx = lambda off: pl.BlockSpec((ctx_len, w), lambda b, h: (cb0 + b, col(off) + h))
    conv = conv_w.reshape(3, 2 * ML_HEADS, w).transpose(1, 0, 2)
    gb = gate_b.reshape(4, ML_HEADS).T
    in_specs = [lat(H_ML_Q), lat(H_ML_K), lat(H_ML_V), lat(H_ML_O),
                ctx(H_ML_Q), ctx(H_ML_K), ctx(H_ML_V), ctx(H_ML_O),
                pl.BlockSpec((1, 3, w), lambda b, h: (h, 0, 0)),
                pl.BlockSpec((1, 3, w), lambda b, h: (ML_HEADS + h, 0, 0)),
                pl.BlockSpec((1, seq, 4), lambda b, h: (h, b, 0)),
                pl.BlockSpec((1, 4, seq), lambda b, h: (h, 0, b)),
                pl.BlockSpec((1, ctx_len, 4), lambda b, h: (h, cb0 + b, 0)),
                pl.BlockSpec((1, 4, ctx_len), lambda b, h: (h, 0, cb0 + b)),
                pl.BlockSpec((1, 1, 4), lambda b, h: (h, 0, 0)),
                pl.BlockSpec((1, 4, 1), lambda b, h: (h, 0, 0)),
                pl.BlockSpec((1, 1, w), lambda b, h: (h, 0, 0))]
    out_specs = [pl.BlockSpec((seq, w), lambda b, h: (b, h))]
    out_shape = [jax.ShapeDtypeStruct((n_batch * seq, ML_W), BF16)]
    if need_ctx:
        out_specs.append(pl.BlockSpec((ctx_len, w), lambda b, h: (b, h)))
        out_shape.append(jax.ShapeDtypeStruct((n_batch * ctx_len, ML_W), BF16))
    outs = pl.pallas_call(
        functools.partial(_ml_kernel, seq=seq, ctx_len=ctx_len, need_ctx=need_ctx),
        grid=(n_batch, ML_HEADS),
        in_specs=in_specs,
        out_specs=out_specs,
        out_shape=out_shape,
        scratch_shapes=[pltpu.VMEM((seq, w), BF16), pltpu.VMEM((seq, w), BF16),
                        pltpu.VMEM((ctx_len, w), BF16), pltpu.VMEM((ctx_len, w), BF16),
                        pltpu.VMEM((seq, w), F32), pltpu.VMEM((ctx_len, w), F32),
                        pltpu.VMEM((w, w), F32), pltpu.VMEM((1, w), F32),
                        pltpu.VMEM((w, w), F32), pltpu.VMEM((1, w), F32)],
        compiler_params=_params(2),
        name="mlstm",
    )(proj_h, proj_h, proj_h, proj_h, proj_h, proj_h, proj_h, proj_h,
      conv, conv, gates_c, gates_r, gates_c, gates_r,
      gb.reshape(ML_HEADS, 1, 4), gb.reshape(ML_HEADS, 4, 1), norm_g.reshape(ML_HEADS, 1, w))
    return outs if need_ctx else (outs[0], None)


def _hg_level_tables(chunk, rev):
    levels = []
    n = HG_LEAF
    while 2 * n <= chunk:
        refs = []
        for start in range(0, chunk, 2 * n):
            refs.append((start + n - 1) if rev else (start + n))
        levels.append((n, refs))
        n *= 2
    return levels


def _hg_chunks(chains, loglb, log1mlb, one_m_lb):
    chunk = chains[0][1].shape[0]
    nb = chunk // HG_LEAF
    t_idx = lax.broadcasted_iota(jnp.int32, (1, HG_LEAF, 1), 1)

    qs, ks, lks, vfs, gcums = [], [], [], [], []
    for qraw, z, v, st_ref, masks, tri, rev in chains:
        e = jnp.exp(-jnp.abs(z))
        one_pe = 1.0 + e
        lsig = jnp.minimum(z, 0.0) - jnp.log(one_pe)
        a = log1mlb + lsig
        lf = jnp.maximum(loglb, a) + jnp.log(1.0 + jnp.exp(-jnp.abs(loglb - a)))
        ks.append(one_m_lb * (jnp.where(z >= 0.0, e, 1.0) / one_pe))
        lks.append((a - z) * LOG2E)
        gcums.append(jnp.dot(tri, lf * LOG2E, precision=lax.Precision.HIGHEST, preferred_element_type=F32))
        x = qraw.astype(F32)
        qs.append(x / (1.0 + jnp.exp(-x)))
        vfs.append(v.astype(F32))

    atts = [jnp.zeros((chunk, chunk), F32) for _ in chains]
    for lvl, (n, _) in enumerate(_hg_level_tables(chunk, False)):
        for c, (qraw, z, v, st_ref, masks, tri, rev) in enumerate(chains):
            refs = _hg_level_tables(chunk, rev)[lvl][1]
            gref = jnp.concatenate(
                [jnp.broadcast_to(gcums[c][r:r + 1, :], (2 * n, HG_DK)) for r in refs], axis=0)
            dec = jnp.exp2(-jnp.abs(gcums[c] - gref))
            p = lax.dot_general((qs[c] * dec).astype(BF16), (ks[c] * dec).astype(BF16), _NT,
                                preferred_element_type=F32)
            atts[c] += jnp.where(masks[lvl], p, 0.0)
    outs = [jnp.dot(atts[c].astype(BF16), chain[2], preferred_element_type=F32)
            for c, chain in enumerate(chains)]

    q_decs, k_decs, tot_decs = [], [], []
    for c, (qraw, z, v, st_ref, masks, tri, rev) in enumerate(chains):
        g_tot = gcums[c][0:1, :] if rev else gcums[c][chunk - 1:chunk, :]
        q_decs.append((qs[c] * jnp.exp2(gcums[c])).astype(BF16))
        k_decs.append((ks[c] * jnp.exp2(g_tot - gcums[c])).astype(BF16))
        tot_decs.append(jnp.exp2(g_tot))

    for c, (qraw, z, v, st_ref, masks, tri, rev) in enumerate(chains):
        g3 = gcums[c].reshape(nb, HG_LEAF, HG_DK)
        h3 = (lks[c] - gcums[c]).reshape(nb, HG_LEAF, HG_DK)
        q3 = qs[c].reshape(nb, HG_LEAF, HG_DK)
        v3 = vfs[c].reshape(nb, HG_LEAF, HG_DK)
        leaf = jnp.zeros((nb, HG_LEAF, HG_DK), F32)
        for s in range(HG_LEAF):
            col = jnp.sum(q3 * jnp.exp2(g3 + h3[:, s:s + 1, :]), axis=2, keepdims=True)
            valid = (t_idx <= s) if rev else (t_idx >= s)
            leaf += jnp.where(valid, col, 0.0) * v3[:, s:s + 1, :]
        outs[c] += leaf.reshape(chunk, HG_DK)

    for c, (qraw, z, v, st_ref, masks, tri, rev) in enumerate(chains):
        st = st_ref[...]
        outs[c] += lax.dot_general(q_decs[c], st.astype(BF16), _NT, preferred_element_type=F32)
        st_ref[...] = st * tot_decs[c] + lax.dot_general(v, k_decs[c], _TN, preferred_element_type=F32)
    return outs


def _hg_kernel(*refs, seq, ctx_len, need_ctx):
    (ql_ref, il_ref, gl_ref, ffl_ref, fbl_ref, qc_ref, ic_ref, gc_ref, ffc_ref, fbc_ref,
     llb_ref, l1m_ref, oml_ref, ng_ref) = refs[:14]
    if need_ctx:
        yl_ref, yc_ref = refs[14:16]
        scratch = refs[16:]
    else:
        yl_ref = refs[14]
        yc_ref = None
        scratch = refs[15:]
    ol_ref, oc_ref, stf_ref, stb_ref = scratch
    chunk = HG_CHUNK
    loglb = llb_ref[0]
    log1mlb = l1m_ref[0]
    one_m_lb = oml_ref[0]
    norm_g = ng_ref[0]
    ti = lax.broadcasted_iota(jnp.int32, (chunk, chunk), 0)
    si = lax.broadcasted_iota(jnp.int32, (chunk, chunk), 1)

    def direction_consts(rev):
        tri = ((si >= ti) if rev else (si <= ti)).astype(F32)
        masks = []
        for n, _ in _hg_level_tables(chunk, rev):
            shift = (2 * n).bit_length() - 1
            same = jnp.right_shift(ti, shift) == jnp.right_shift(si, shift)
            t_late = jnp.bitwise_and(ti, 2 * n - 1) >= n
            s_late = jnp.bitwise_and(si, 2 * n - 1) >= n
            if rev:
                masks.append(same & jnp.logical_not(t_late) & s_late)
            else:
                masks.append(same & t_late & jnp.logical_not(s_late))
        return tri, masks

    tri_f, masks_f = direction_consts(False)
    tri_b, masks_b = direction_consts(True)

    def run(q_s, i_s, g_s, zf_s, zb_s, o_s, y_s, n_chunks):
        def step(pi, second):
            rows = []
            chains = []
            for u in range(HG_UNROLL):
                c = pi * HG_UNROLL + u
                r_f = pl.ds(pl.multiple_of(c * chunk, chunk), chunk)
                rows.append(r_f)
                chains.append((q_s[r_f, :], zf_s[r_f, :], i_s[r_f, :], stf_ref, masks_f, tri_f, False))
            for u in range(HG_UNROLL):
                c = n_chunks - 1 - (pi * HG_UNROLL + u)
                r_b = pl.ds(pl.multiple_of(c * chunk, chunk), chunk)
                rows.append(r_b)
                chains.append((q_s[r_b, :], zb_s[r_b, :], i_s[r_b, :], stb_ref, masks_b, tri_b, True))
            outs = _hg_chunks(chains, loglb, log1mlb, one_m_lb)
            for r, o in zip(rows, outs):
                if not second:
                    o_s[r, :] = o
                elif y_s is not None:
                    y = _head_rms(o_s[r, :] + o, norm_g) * _silu(g_s[r, :].astype(F32))
                    y_s[r, :] = y.astype(y_s.dtype)

        def first(pi, carry):
            step(pi, False)
            return carry

        def second(pi, carry):
            step(pi, True)
            return carry

        n_steps = n_chunks // HG_UNROLL
        lax.fori_loop(0, n_steps // 2, first, 0)
        lax.fori_loop(n_steps // 2, n_steps, second, 0)

    stf_ref[...] = jnp.zeros_like(stf_ref)
    stb_ref[...] = jnp.zeros_like(stb_ref)
    run(qc_ref, ic_ref, gc_ref, ffc_ref, fbc_ref, oc_ref, yc_ref, ctx_len // chunk)
    run(ql_ref, il_ref, gl_ref, ffl_ref, fbl_ref, ol_ref, yl_ref, seq // chunk)


def _hgrn2(proj_h, proj_f, lb, norm_g, n_batch, seq, ctx_len, need_ctx):
    w = HG_DK
    cb0 = n_batch * seq // ctx_len
    lat = lambda off: pl.BlockSpec((seq, w), lambda b, h: (b, off // w + h))
    ctx = lambda off: pl.BlockSpec((ctx_len, w), lambda b, h: (cb0 + b, off // w + h))
    vec = pl.BlockSpec((1, 1, w), lambda b, h: (h, 0, 0))
    lb3 = lb.reshape(HG_HEADS, 1, w)
    in_specs = [lat(H_HG_Q), lat(H_HG_I), lat(H_HG_G), lat(F_HG_FF), lat(F_HG_FB),
                ctx(H_HG_Q), ctx(H_HG_I), ctx(H_HG_G), ctx(F_HG_FF), ctx(F_HG_FB),
                vec, vec, vec, vec]
    out_specs = [pl.BlockSpec((seq, w), lambda b, h: (b, h))]
    out_shape = [jax.ShapeDtypeStruct((n_batch * seq, HG_W), BF16)]
    if need_ctx:
        out_specs.append(pl.BlockSpec((ctx_len, w), lambda b, h: (b, h)))
        out_shape.append(jax.ShapeDtypeStruct((n_batch * ctx_len, HG_W), BF16))
    outs = pl.pallas_call(
        functools.partial(_hg_kernel, seq=seq, ctx_len=ctx_len, need_ctx=need_ctx),
        grid=(n_batch, HG_HEADS),
        in_specs=in_specs,
        out_specs=out_specs,
        out_shape=out_shape,
        scratch_shapes=[pltpu.VMEM((seq, w), F32), pltpu.VMEM((ctx_len, w), F32),
                        pltpu.VMEM((w, w), F32), pltpu.VMEM((w, w), F32)],
        compiler_params=_params(2),
        name="hgrn2",
    )(proj_h, proj_h, proj_h, proj_f, proj_f, proj_h, proj_h, proj_h, proj_f, proj_f,
      jnp.log(lb3), jnp.log1p(-lb3), 1.0 - lb3, norm_g.reshape(HG_HEADS, 1, w))
    return outs if need_ctx else (outs[0], None)


def _router_kernel(x_ref, g_ref, sh_ref, sc_ref, wr_ref, h_ref, idx_ref, wt_ref, cnt_ref, carry_ref):
    @pl.when(pl.program_id(0) == 0)
    def _():
        carry_ref[...] = jnp.zeros_like(carry_ref)

    x = x_ref[...]
    y = x * lax.rsqrt(jnp.mean(x * x, axis=-1, keepdims=True) + EPS) * g_ref[...]
    h = y * (1.0 + sc_ref[0]) + sh_ref[0]
    h_ref[...] = h.astype(h_ref.dtype)
    logits = jnp.dot(h, wr_ref[...], precision=lax.Precision.HIGHEST, preferred_element_type=F32)
    lane = lax.broadcasted_iota(jnp.int32, logits.shape, 1).astype(F32)
    lg = jnp.where(lane < N_EXPERTS, logits, -jnp.inf)
    m1 = jnp.max(lg, axis=-1, keepdims=True)
    i1 = jnp.min(jnp.where(lg == m1, lane, float(LANE)), axis=-1, keepdims=True)
    lg2 = jnp.where(lane == i1, -jnp.inf, lg)
    m2 = jnp.max(lg2, axis=-1, keepdims=True)
    i2 = jnp.min(jnp.where(lg2 == m2, lane, float(LANE)), axis=-1, keepdims=True)
    e = jnp.exp(m2 - m1)
    w1 = 1.0 / (1.0 + e)
    w2 = e / (1.0 + e)
    wt_ref[...] = jnp.where(lane == 0, w1, jnp.where(lane == 1, w2, 0.0))
    n_tok = x.shape[0]
    routed = jnp.where(jnp.logical_or(lane == i1, lane == i2), 1.0, 0.0)
    ti = lax.broadcasted_iota(jnp.int32, (n_tok, n_tok), 0)
    ui = lax.broadcasted_iota(jnp.int32, (n_tok, n_tok), 1)
    earlier = jnp.where(ui < ti, 1.0, 0.0).astype(BF16)
    before = carry_ref[...] + jnp.dot(earlier, routed.astype(BF16), preferred_element_type=F32)
    r1 = jnp.sum(jnp.where(lane == i1, before, 0.0), axis=-1, keepdims=True)
    r2 = jnp.sum(jnp.where(lane == i2, before, 0.0), axis=-1, keepdims=True)
    carry_ref[...] += jnp.sum(routed, axis=0, keepdims=True)
    cnt_ref[...] = jnp.broadcast_to(carry_ref[...], cnt_ref.shape).astype(jnp.int32)
    idx_ref[...] = jnp.where(lane == 0, i1, jnp.where(lane == 1, i2, jnp.where(
        lane == 2, r1, jnp.where(lane == 3, r2, 0.0)))).astype(jnp.int32)


def _router(x, g, shift, scale, w_router, n_rows, seq, n_batch):
    d = x.shape[1]
    tm = 256
    tpb = seq // tm
    grp = lambda i: (_group_of(i, tpb, n_batch), 0, 0)
    wr = jnp.pad(w_router, ((0, 0), (0, LANE - N_EXPERTS)))
    return pl.pallas_call(
        _router_kernel,
        grid=(n_rows // tm,),
        in_specs=[pl.BlockSpec((tm, d), lambda i: (i, 0)),
                  pl.BlockSpec((1, d), lambda i: (0, 0)),
                  pl.BlockSpec((1, 1, d), grp),
                  pl.BlockSpec((1, 1, d), grp),
                  pl.BlockSpec((d, LANE), lambda i: (0, 0))],
        out_specs=[pl.BlockSpec((tm, d), lambda i: (i, 0)),
                   pl.BlockSpec((tm, LANE), lambda i: (i, 0)),
                   pl.BlockSpec((tm, LANE), lambda i: (i, 0)),
                   pl.BlockSpec((8, LANE), lambda i: (0, 0))],
        out_shape=[jax.ShapeDtypeStruct((n_rows, d), F32),
                   jax.ShapeDtypeStruct((n_rows, LANE), jnp.int32),
                   jax.ShapeDtypeStruct((n_rows, LANE), F32),
                   jax.ShapeDtypeStruct((8, LANE), jnp.int32)],
        scratch_shapes=[pltpu.VMEM((1, LANE), F32)],
        compiler_params=_params(1),
        name="router",
    )(x, g.reshape(1, d), shift, scale, wr)


def _row_copy(src_hbm, buf_ref, sem, src_row, r):
    return pltpu.make_async_copy(src_hbm.at[pl.ds(src_row, 1)], buf_ref.at[pl.ds(r, 1)], sem)


def _dispatch_kernel(src_ref, nu_ref, h_hbm, o_ref, buf_ref, sem):
    t = pl.program_id(0)
    tm = buf_ref.shape[0]

    @pl.when(t < nu_ref[0])
    def _():
        base = t * tm

        def start(r, carry):
            _row_copy(h_hbm, buf_ref, sem, src_ref[base + r], r).start()
            return carry

        def wait(r, carry):
            _row_copy(h_hbm, buf_ref, sem, src_ref[base + r], r).wait()
            return carry

        lax.fori_loop(0, tm, start, 0, unroll=8)
        lax.fori_loop(0, tm, wait, 0, unroll=8)
        o_ref[...] = buf_ref[...].astype(o_ref.dtype)

    @pl.when(t >= nu_ref[0])
    def _():
        o_ref[...] = jnp.zeros_like(o_ref)


def _dispatch(h, src_token, n_used):
    d = h.shape[1]
    tm = MOE_TILE
    n_pad = src_token.shape[0]
    return pl.pallas_call(
        _dispatch_kernel,
        grid_spec=pltpu.PrefetchScalarGridSpec(
            num_scalar_prefetch=2,
            grid=(n_pad // tm,),
            in_specs=[pl.BlockSpec(memory_space=pl.ANY)],
            out_specs=pl.BlockSpec((tm, d), lambda t, src, nu: (t, 0)),
            scratch_shapes=[pltpu.VMEM((tm, d), F32), pltpu.SemaphoreType.DMA]),
        out_shape=jax.ShapeDtypeStruct((n_pad, d), BF16),
        compiler_params=_params(1),
        name="moe_dispatch",
    )(src_token, n_used, h)


def _new_expert(te_ref, i):
    return jnp.logical_or(i == 0, te_ref[i] != te_ref[jnp.maximum(i - 1, 0)])


def _moe_up_kernel(te_ref, nu_ref, a_ref, wg_ref, wu_ref, o_ref, wgb_ref, wub_ref):
    i = pl.program_id(1)

    @pl.when(_new_expert(te_ref, i))
    def _():
        wgb_ref[...] = wg_ref[...].astype(BF16)
        wub_ref[...] = wu_ref[...].astype(BF16)

    @pl.when(i < nu_ref[0])
    def _():
        a = a_ref[...]
        g = jnp.dot(a, wgb_ref[...], preferred_element_type=F32)
        u = jnp.dot(a, wub_ref[...], preferred_element_type=F32)
        o_ref[...] = (_silu(g) * u).astype(o_ref.dtype)

    @pl.when(i >= nu_ref[0])
    def _():
        o_ref[...] = jnp.zeros_like(o_ref)


def _moe_down_kernel(te_ref, nu_ref, a_ref, w_ref, o_ref, wb_ref):
    i = pl.program_id(1)

    @pl.when(_new_expert(te_ref, i))
    def _():
        wb_ref[...] = w_ref[...].astype(BF16)

    @pl.when(i < nu_ref[0])
    def _():
        o_ref[...] = jnp.dot(a_ref[...], wb_ref[...], preferred_element_type=F32).astype(o_ref.dtype)

    @pl.when(i >= nu_ref[0])
    def _():
        o_ref[...] = jnp.zeros_like(o_ref)


def _moe_grouped(a, weights, tile_expert, n_used, kernel, out_dtype, name):
    n_rows, k = a.shape
    n = weights[0].shape[2]
    tm, tn = MOE_TILE, COL_TILE
    a_map = lambda j, i, te, nu: (jnp.maximum(jnp.minimum(i, nu[0] - 1), 0), 0)
    w_spec = pl.BlockSpec((None, k, tn), lambda j, i, te, nu: (te[i], 0, j))
    return pl.pallas_call(
        kernel,
        grid_spec=pltpu.PrefetchScalarGridSpec(
            num_scalar_prefetch=2,
            grid=(n // tn, n_rows // tm),
            in_specs=[pl.BlockSpec((tm, k), a_map)] + [w_spec] * len(weights),
            out_specs=pl.BlockSpec((tm, tn), lambda j, i, te, nu: (i, j)),
            scratch_shapes=[pltpu.VMEM((k, tn), BF16)] * len(weights)),
        out_shape=jax.ShapeDtypeStruct((n_rows, n), out_dtype),
        compiler_params=_params(2),
        name=name,
    )(tile_expert, n_used, a, *weights)


def _combine_kernel(x_ref, ya_ref, yb_ref, wt_ref, g_ref, o_ref):
    wt = wt_ref[...]
    y = wt[:, 0:1] * ya_ref[...].astype(F32) + wt[:, 1:2] * yb_ref[...].astype(F32)
    o_ref[...] = x_ref[...] + g_ref[0] * y


def _moe_combine(x, ya, yb, wt, gate, n_rows, seq, n_batch):
    d = x.shape[1]
    tm = 256
    tpb = seq // tm
    row = pl.BlockSpec((tm, d), lambda i: (i, 0))
    return pl.pallas_call(
        _combine_kernel,
        grid=(n_rows // tm,),
        in_specs=[row, row, row,
                  pl.BlockSpec((tm, LANE), lambda i: (i, 0)),
                  pl.BlockSpec((1, 1, d), lambda i: (_group_of(i, tpb, n_batch), 0, 0))],
        out_specs=row,
        out_shape=jax.ShapeDtypeStruct((n_rows, d), F32),
        compiler_params=_params(1),
        name="moe_combine",
    )(x, ya, yb, wt, gate)


def _moe_ffn(x, g, shift, scale, gate, w_router, wg, wu, wd, n_rows, seq, n_batch):
    h, idx, wt, cnt = _router(x, g, shift, scale, w_router, n_rows, seq, n_batch)
    tm = MOE_TILE
    n_tiles = 2 * n_rows // tm + N_EXPERTS
    counts = cnt[0, :N_EXPERTS]
    tiles = (counts + tm - 1) // tm
    tile_end = jnp.cumsum(tiles)
    tile_start = tile_end - tiles
    expert = jnp.concatenate([idx[:, 0], idx[:, 1]])
    rank = jnp.concatenate([idx[:, 2], idx[:, 3]])
    token = jnp.concatenate([jnp.arange(n_rows, dtype=jnp.int32)] * 2)
    one_hot = (expert[:, None] == jnp.arange(N_EXPERTS, dtype=jnp.int32)[None, :]).astype(jnp.int32)
    dest = jnp.sum(tile_start[None, :] * one_hot, axis=1) * tm + rank
    src_token = jnp.zeros((n_tiles * tm,), jnp.int32).at[dest].set(token)
    n_used = tile_end[-1:].astype(jnp.int32)
    tile_id = jnp.minimum(jnp.arange(n_tiles, dtype=jnp.int32), n_used[0] - 1)
    tile_expert = jnp.sum((tile_end[None, :] <= tile_id[:, None]).astype(jnp.int32), axis=1)

    a_sorted = _dispatch(h, src_token, n_used)
    u = _moe_grouped(a_sorted, (wg, wu), tile_expert, n_used, _moe_up_kernel, BF16, "moe_up")
    y = _moe_grouped(u, (wd,), tile_expert, n_used, _moe_down_kernel, BF16, "moe_down")
    ya = jnp.take(y, dest[:n_rows], axis=0, mode="clip")
    yb = jnp.take(y, dest[n_rows:], axis=0, mode="clip")
    return _moe_combine(x, ya, yb, wt, gate, n_rows, seq, n_batch)


def _rope_tables(n_batch, seq, n_ctx_rows):
    t = jnp.arange(seq)
    pos = jnp.stack([t // GRID_W, t % GRID_W], axis=-1).astype(F32)
    inv_freq = jnp.exp(-jnp.log(ROPE_BASE) * jnp.arange(0, ROPE_AXIS_DIM, 2, dtype=F32) / ROPE_AXIS_DIM)
    ang = pos[..., None] * inv_freq
    cos, sin = jnp.cos(ang), jnp.sin(ang)
    cos = jnp.concatenate([cos[:, 0], cos[:, 0], cos[:, 1], cos[:, 1]], axis=-1)
    sin = jnp.concatenate([-sin[:, 0], sin[:, 0], -sin[:, 1], sin[:, 1]], axis=-1)
    cos = jnp.concatenate([jnp.tile(cos, (n_batch, 1)), jnp.ones((n_ctx_rows, HEAD_DIM), F32)], axis=0)
    sin = jnp.concatenate([jnp.tile(sin, (n_batch, 1)), jnp.zeros((n_ctx_rows, HEAD_DIM), F32)], axis=0)
    return cos, sin


def kernel(x, c, ctx, c_ctx, w_mod, b_mod, norm1_g, norm2_g, w_in, attn_q_norm_g, attn_k_norm_g, ml_conv_w,
           ml_gate_b, ml_norm_g, hg_lb_logits, hg_norm_g, w_branch, w_out, ffn_w_gate, ffn_w_up, ffn_w_down,
           moe_w_router, moe_w_gate, moe_w_up, moe_w_down):
    n_batch, seq, d = x.shape
    ctx_len = ctx.shape[1]
    depth = w_mod.shape[0]
    m_lat = n_batch * seq
    m_ctx = n_batch * ctx_len
    m_all = m_lat + m_ctx
    assert d == D_MODEL and w_in.shape[2] == _N_IN
    assert seq % ROW_TILE == 0 and m_ctx % ROW_TILE == 0 and seq % ctx_len == 0
    assert seq % ML_CHUNK == 0 and ctx_len % ML_CHUNK == 0 and ctx_len % HG_CHUNK == 0

    xa = jnp.concatenate([x.reshape(m_lat, d), ctx.reshape(m_ctx, d)], axis=0)
    cs = jnp.concatenate([c, c_ctx[None], jnp.zeros((8 - n_batch - 1, d), F32)], axis=0)
    mod = _modulation(cs, w_mod, b_mod)
    cos, sin = _rope_tables(n_batch, seq, m_ctx)
    lb_all = jnp.cumsum(jax.nn.softmax(hg_lb_logits.astype(F32), axis=0), axis=0)
    lb_all = lb_all - lb_all[:1]

    for l in range(depth):
        need_ctx = l < depth - 1
        n_rows = m_all if need_ctx else m_lat
        mods = [mod[l, :n_batch + 1, i * d:(i + 1) * d].reshape(n_batch + 1, 1, d) for i in range(N_MOD)]

        h = _norm_mod(xa, norm1_g[l], mods[0], mods[1], m_all, seq, n_batch)
        proj_h = _proj_in(h, w_in, l, *H_TILES, BF16)
        proj_f = _proj_in(h, w_in, l, *F_TILES, F32)

        q_rot, k_rot = _att_prep(proj_h, cos, sin, attn_q_norm_g[l], attn_k_norm_g[l])
        ya = _attention(q_rot, k_rot, proj_h, n_batch, seq, ctx_len, True)

        gates = proj_f[:, F_ML_G:F_ML_G + 4 * ML_HEADS].reshape(m_all, 4, ML_HEADS)
        gates_c = gates.transpose(2, 0, 1)
        gates_r = gates.transpose(2, 1, 0)
        ym, ym_c = _mlstm(proj_h, gates_c, gates_r, ml_conv_w[l], ml_gate_b[l], ml_norm_g[l],
                          n_batch, seq, ctx_len, need_ctx)
        yh, yh_c = _hgrn2(proj_h, proj_f, lb_all[l], hg_norm_g[l], n_batch, seq, ctx_len, need_ctx)
        ys_ctx = None
        if need_ctx:
            ya_c = _attention(q_rot, k_rot, proj_h, n_batch, seq, ctx_len, False)
            ys_ctx = (ya_c, ym_c, yh_c)

        merged = _branch_merge((ya, ym, yh), ys_ctx, w_branch, l, proj_h, n_rows)
        xa = _matmul_residual(merged, w_out, l, xa, mods[2], n_rows, seq, n_batch, "proj_out")

        if l % 2 == 0:
            h2 = _norm_mod(xa, norm2_g[l], mods[3], mods[4], n_rows, seq, n_batch)
            u = _glu(h2, ffn_w_gate, ffn_w_up, l // 2)
            xa = _matmul_residual(u, ffn_w_down, l // 2, xa, mods[5], n_rows, seq, n_batch, "ffn_down")
        else:
            xa = _moe_ffn(xa, norm2_g[l], mods[3], mods[4], mods[5], moe_w_router[l // 2],
                          moe_w_gate[l // 2], moe_w_up[l // 2], moe_w_down[l // 2], n_rows, seq, n_batch)
    return xa[:m_lat].reshape(n_batch, seq, d)
```

```python
import functools

import jax
import jax.numpy as jnp
from jax import lax
from jax.experimental import pallas as pl
from jax.experimental.pallas import tpu as pltpu

F32 = jnp.float32
BF16 = jnp.bfloat16

D_MODEL = 2048
GRID_W = 64
N_MOD = 6
ATT_HEADS = 8
ATT_KV_HEADS = 2
HEAD_DIM = 128
ATT_W = ATT_HEADS * HEAD_DIM
KV_W = ATT_KV_HEADS * HEAD_DIM
ROPE_AXIS_DIM = HEAD_DIM // 2
ROPE_BASE = 10000.0
ML_HEADS = 4
ML_DK = 256
ML_W = ML_HEADS * ML_DK
HG_HEADS = 8
HG_DK = 128
HG_W = HG_HEADS * HG_DK
D_FF = 5632
N_EXPERTS = 8
EPS = 1e-6
NEG_INIT = -1e30
LOG2E = 1.4426950408889634

_O_ATT_Q = 0
_O_ML_GATES = ATT_W + 2 * KV_W + 4 * ML_W
_O_HG_Q = _O_ML_GATES + 4 * ML_HEADS
_O_HG_FF = _O_HG_Q + HG_W
_O_HG_I = _O_HG_FF + 2 * HG_W
_O_MERGE = _O_HG_I + 2 * HG_W
_N_IN = _O_MERGE + 3 * D_MODEL

H_ATT_Q = 0
H_ATT_K = 1024
H_ATT_V = 1280
H_ML_Q = 1536
H_ML_K = 2560
H_ML_V = 3584
H_ML_O = 4608
H_HG_Q = 5632
H_HG_I = 6656
H_HG_G = 7680
H_MERGE = 8704
H_WIDTH = 14848
F_HG_FF = 0
F_HG_FB = 1024
F_ML_G = 2048

LANE = 128
ROW_TILE = 512
COL_TILE = 512
PROJ_ROW_TILES = 4
W_MISALIGN = _O_HG_Q % LANE


def _source_tiles(segments):
    blocks, shifted = [], []
    for off, width in segments:
        for start in range(off, off + width, COL_TILE):
            assert start % COL_TILE in (0, W_MISALIGN)
            blocks.append(start // COL_TILE)
            shifted.append(int(start % COL_TILE != 0))
    return blocks, shifted


H_TILES = _source_tiles([(0, _O_ML_GATES), (_O_HG_Q, HG_W), (_O_HG_I, 2 * HG_W + 3 * D_MODEL)])
F_TILES = _source_tiles([(_O_HG_FF, 2 * HG_W), (_O_ML_GATES, COL_TILE)])
VMEM_LIMIT = 56 * 1024 * 1024

ML_CHUNK = 256
HG_CHUNK = 64
HG_LEAF = 8
HG_UNROLL = 2
ATT_TQ = 256
MOE_TILE = 512

_NT = (((1,), (1,)), ((), ()))
_TN = (((0,), (0,)), ((), ()))


def _params(n_axes):
    return pltpu.CompilerParams(dimension_semantics=("arbitrary",) * n_axes,
                                vmem_limit_bytes=VMEM_LIMIT)


def _silu(x):
    return x * jax.nn.sigmoid(x)


def _log_sigmoid(x):
    return jnp.minimum(x, 0.0) - jnp.log1p(jnp.exp(-jnp.abs(x)))


def _mod_kernel(c_ref, w_ref, b_ref, o_ref):
    a = _silu(c_ref[...])
    o_ref[...] = jnp.dot(a.astype(BF16), w_ref[...].astype(BF16), preferred_element_type=F32) + b_ref[...]


def _modulation(cs, w_mod, b_mod):
    depth, d, n = w_mod.shape
    tn = 1024
    return pl.pallas_call(
        _mod_kernel,
        grid=(depth, n // tn),
        in_specs=[pl.BlockSpec((8, d), lambda l, j: (0, 0)),
                  pl.BlockSpec((None, d, tn), lambda l, j: (l, 0, j)),
                  pl.BlockSpec((None, 1, tn), lambda l, j: (l, 0, j))],
        out_specs=pl.BlockSpec((None, 8, tn), lambda l, j: (l, 0, j)),
        out_shape=jax.ShapeDtypeStruct((depth, 8, n), F32),
        compiler_params=_params(2),
        name="modulation",
    )(cs, w_mod, b_mod.reshape(depth, 1, n))


def _norm_kernel(x_ref, g_ref, sh_ref, sc_ref, o_ref):
    x = x_ref[...]
    y = x * lax.rsqrt(jnp.mean(x * x, axis=-1, keepdims=True) + EPS)
    y = y * g_ref[...]
    o_ref[...] = (y * (1.0 + sc_ref[0]) + sh_ref[0]).astype(o_ref.dtype)


def _group_of(i, tiles_per_batch, n_batch):
    return jnp.minimum(i // tiles_per_batch, n_batch)


def _norm_mod(x, g, shift, scale, n_rows, seq, n_batch):
    d = x.shape[1]
    tm = 256
    tpb = seq // tm
    grp = lambda i: (_group_of(i, tpb, n_batch), 0, 0)
    return pl.pallas_call(
        _norm_kernel,
        grid=(n_rows // tm,),
        in_specs=[pl.BlockSpec((tm, d), lambda i: (i, 0)),
                  pl.BlockSpec((1, d), lambda i: (0, 0)),
                  pl.BlockSpec((1, 1, d), grp),
                  pl.BlockSpec((1, 1, d), grp)],
        out_specs=pl.BlockSpec((tm, d), lambda i: (i, 0)),
        out_shape=jax.ShapeDtypeStruct((n_rows, d), BF16),
        compiler_params=_params(1),
        name="norm_mod",
    )(x, g.reshape(1, d), shift, scale)


def _proj_in_kernel(blk_ref, shift_ref, a_ref, wa_ref, wb_ref, o_ref, w_scr):
    j = pl.program_id(0)

    @pl.when(pl.program_id(1) == 0)
    def _():
        @pl.when(shift_ref[j] == 0)
        def _():
            w_scr[...] = wa_ref[...].T.astype(BF16)

        @pl.when(shift_ref[j] != 0)
        def _():
            w = jnp.concatenate([wa_ref[W_MISALIGN:, :], wb_ref[...]], axis=0)
            w_scr[...] = w.T.astype(BF16)

    o_ref[...] = jnp.dot(a_ref[...], w_scr[...], preferred_element_type=F32).astype(o_ref.dtype)


def _proj_in(a, w_in_t, layer, src_blocks, shifted, out_dtype):
    m, k = a.shape
    n_src = w_in_t.shape[1]
    tm, tn = m // PROJ_ROW_TILES, COL_TILE
    n_tiles = len(src_blocks)
    last_blk = n_src // W_MISALIGN - 1
    per = tn // W_MISALIGN
    return pl.pallas_call(
        _proj_in_kernel,
        grid_spec=pltpu.PrefetchScalarGridSpec(
            num_scalar_prefetch=2,
            grid=(n_tiles, m // tm),
            in_specs=[pl.BlockSpec((tm, k), lambda j, i, blk, sh: (i, 0)),
                      pl.BlockSpec((None, tn, k), lambda j, i, blk, sh: (layer, blk[j], 0)),
                      pl.BlockSpec((None, W_MISALIGN, k), lambda j, i, blk, sh: (
                          layer, jnp.minimum(per * (blk[j] + 1), last_blk), 0))],
            out_specs=pl.BlockSpec((tm, tn), lambda j, i, blk, sh: (i, j)),
            scratch_shapes=[pltpu.VMEM((k, tn), BF16)]),
        out_shape=jax.ShapeDtypeStruct((m, n_tiles * tn), out_dtype),
        compiler_params=_params(2),
        name="proj_in",
    )(jnp.asarray(src_blocks, jnp.int32), jnp.asarray(shifted, jnp.int32), a, w_in_t, w_in_t)


def _mm_res_kernel(a_ref, w_ref, r_ref, g_ref, o_ref, wb_ref):
    @pl.when(pl.program_id(1) == 0)
    def _():
        wb_ref[...] = w_ref[...].astype(BF16)

    y = jnp.dot(a_ref[...], wb_ref[...], preferred_element_type=F32)
    o_ref[...] = r_ref[...] + g_ref[0] * y


def _matmul_residual(a, w, layer, res, gate, n_rows, seq, n_batch, name):
    k = a.shape[1]
    n = w.shape[2]
    tm, tn = ROW_TILE, COL_TILE
    tpb = seq // tm
    return pl.pallas_call(
        _mm_res_kernel,
        grid=(n // tn, n_rows // tm),
        in_specs=[pl.BlockSpec((tm, k), lambda j, i: (i, 0)),
                  pl.BlockSpec((None, k, tn), lambda j, i: (layer, 0, j)),
                  pl.BlockSpec((tm, tn), lambda j, i: (i, j)),
                  pl.BlockSpec((1, 1, tn), lambda j, i: (_group_of(i, tpb, n_batch), 0, j))],
        out_specs=pl.BlockSpec((tm, tn), lambda j, i: (i, j)),
        out_shape=jax.ShapeDtypeStruct((n_rows, n), F32),
        scratch_shapes=[pltpu.VMEM((k, tn), BF16)],
        compiler_params=_params(2),
        name=name,
    )(a, w, res, gate)


def _glu_kernel(a_ref, wg_ref, wu_ref, o_ref, wgb_ref, wub_ref):
    @pl.when(pl.program_id(1) == 0)
    def _():
        wgb_ref[...] = wg_ref[...].astype(BF16)
        wub_ref[...] = wu_ref[...].astype(BF16)

    a = a_ref[...]
    g = jnp.dot(a, wgb_ref[...], preferred_element_type=F32)
    u = jnp.dot(a, wub_ref[...], preferred_element_type=F32)
    o_ref[...] = (_silu(g) * u).astype(o_ref.dtype)


def _glu(a, wg, wu, layer):
    m, k = a.shape
    n = wg.shape[2]
    tm, tn = ROW_TILE, COL_TILE
    return pl.pallas_call(
        _glu_kernel,
        grid=(n // tn, m // tm),
        in_specs=[pl.BlockSpec((tm, k), lambda j, i: (i, 0)),
                  pl.BlockSpec((None, k, tn), lambda j, i: (layer, 0, j)),
                  pl.BlockSpec((None, k, tn), lambda j, i: (layer, 0, j))],
        out_specs=pl.BlockSpec((tm, tn), lambda j, i: (i, j)),
        out_shape=jax.ShapeDtypeStruct((m, n), BF16),
        scratch_shapes=[pltpu.VMEM((k, tn), BF16), pltpu.VMEM((k, tn), BF16)],
        compiler_params=_params(2),
        name="ffn_up",
    )(a, wg, wu)


def _merge_kernel(*refs, n_lat_tiles, with_ctx):
    if with_ctx:
        lat_refs, ctx_refs, rest = refs[0:3], refs[3:6], refs[6:]
    else:
        lat_refs, ctx_refs, rest = refs[0:3], None, refs[3:]
    w_ref, g0_ref, g1_ref, g2_ref, o_ref, wb_ref = rest

    @pl.when(pl.program_id(1) == 0)
    def _():
        wb_ref[...] = w_ref[...].astype(BF16)

    is_ctx = pl.program_id(1) >= n_lat_tiles
    acc = None
    for b, g_ref in enumerate((g0_ref, g1_ref, g2_ref)):
        y = lat_refs[b][...]
        if with_ctx:
            y = jnp.where(is_ctx, ctx_refs[b][...], y)
        term = jax.nn.sigmoid(g_ref[...].astype(F32)) * jnp.dot(y, wb_ref[b], preferred_element_type=F32)
        acc = term if acc is None else acc + term
    o_ref[...] = acc.astype(o_ref.dtype)


def _branch_merge(ys_lat, ys_ctx, w_branch, layer, proj_h, n_rows):
    k = ys_lat[0].shape[1]
    n = w_branch.shape[3]
    tm, tn = ROW_TILE, COL_TILE
    n_lat_tiles = ys_lat[0].shape[0] // tm
    with_ctx = ys_ctx is not None
    gate_spec = lambda b: pl.BlockSpec((tm, tn), lambda j, i: (i, (H_MERGE + b * n) // tn + j))
    lat_spec = pl.BlockSpec((tm, k), lambda j, i: (jnp.minimum(i, n_lat_tiles - 1), 0))
    ctx_spec = pl.BlockSpec((tm, k), lambda j, i: (jnp.maximum(i - n_lat_tiles, 0), 0))
    in_specs = [lat_spec] * 3 + ([ctx_spec] * 3 if with_ctx else [])
    in_specs += [pl.BlockSpec((None, 3, k, tn), lambda j, i: (layer, 0, 0, j)),
                 gate_spec(0), gate_spec(1), gate_spec(2)]
    args = list(ys_lat) + (list(ys_ctx) if with_ctx else [])
    return pl.pallas_call(
        functools.partial(_merge_kernel, n_lat_tiles=n_lat_tiles, with_ctx=with_ctx),
        grid=(n // tn, n_rows // tm),
        in_specs=in_specs,
        out_specs=pl.BlockSpec((tm, tn), lambda j, i: (i, j)),
        out_shape=jax.ShapeDtypeStruct((n_rows, n), BF16),
        scratch_shapes=[pltpu.VMEM((3, k, tn), BF16)],
        compiler_params=_params(2),
        name="branch_merge",
    )(*args, w_branch, proj_h, proj_h, proj_h)


def _head_rms(x, g):
    return x * lax.rsqrt(jnp.mean(x * x, axis=-1, keepdims=True) + EPS) * g


def _rope(x, cos, sin, low_half):
    partner = jnp.where(low_half, pltpu.roll(x, 96, axis=1), pltpu.roll(x, 32, axis=1))
    return x * cos + partner * sin


def _att_prep_kernel(q_ref, k_ref, cos_ref, sin_ref, qg_ref, kg_ref, qo_ref, ko_ref):
    cos = cos_ref[...]
    sin = sin_ref[...]
    lane = lax.broadcasted_iota(jnp.int32, cos.shape, 1)
    low_half = jnp.bitwise_and(lane, 63) < 32
    scale = HEAD_DIM ** -0.5 * LOG2E
    for h in range(ATT_HEADS):
        sl = slice(h * HEAD_DIM, (h + 1) * HEAD_DIM)
        x = _head_rms(q_ref[:, sl].astype(F32), qg_ref[...])
        qo_ref[:, sl] = (_rope(x, cos, sin, low_half) * scale).astype(qo_ref.dtype)
    for h in range(ATT_KV_HEADS):
        sl = slice(h * HEAD_DIM, (h + 1) * HEAD_DIM)
        x = _head_rms(k_ref[:, sl].astype(F32), kg_ref[...])
        ko_ref[:, sl] = _rope(x, cos, sin, low_half).astype(ko_ref.dtype)


def _att_prep(proj_h, cos, sin, qg, kg):
    m = proj_h.shape[0]
    tm = ROW_TILE
    return pl.pallas_call(
        _att_prep_kernel,
        grid=(m // tm,),
        in_specs=[pl.BlockSpec((tm, ATT_W), lambda i: (i, H_ATT_Q // ATT_W)),
                  pl.BlockSpec((tm, KV_W), lambda i: (i, H_ATT_K // KV_W)),
                  pl.BlockSpec((tm, HEAD_DIM), lambda i: (i, 0)),
                  pl.BlockSpec((tm, HEAD_DIM), lambda i: (i, 0)),
                  pl.BlockSpec((1, HEAD_DIM), lambda i: (0, 0)),
                  pl.BlockSpec((1, HEAD_DIM), lambda i: (0, 0))],
        out_specs=[pl.BlockSpec((tm, ATT_W), lambda i: (i, 0)),
                   pl.BlockSpec((tm, KV_W), lambda i: (i, 0))],
        out_shape=[jax.ShapeDtypeStruct((m, ATT_W), BF16),
                   jax.ShapeDtypeStruct((m, KV_W), BF16)],
        compiler_params=_params(1),
        name="att_prep",
    )(proj_h, proj_h, cos, sin, qg.reshape(1, HEAD_DIM), kg.reshape(1, HEAD_DIM))


def _att_kernel(*refs, with_latent):
    if with_latent:
        q_ref, kc_ref, vc_ref, kl_ref, vl_ref, o_ref = refs
    else:
        q_ref, kc_ref, vc_ref, o_ref = refs
    grp = ATT_HEADS // ATT_KV_HEADS
    for g in range(grp):
        sl = slice(g * HEAD_DIM, (g + 1) * HEAD_DIM)
        q = q_ref[:, sl]
        sc = lax.dot_general(q, kc_ref[...], _NT, preferred_element_type=F32)
        m = jnp.max(sc, axis=-1, keepdims=True)
        if with_latent:
            sl_ = lax.dot_general(q, kl_ref[...], _NT, preferred_element_type=F32)
            m = jnp.maximum(m, jnp.max(sl_, axis=-1, keepdims=True))
        pc = jnp.exp2(sc - m)
        den = jnp.sum(pc, axis=-1, keepdims=True)
        acc = jnp.dot(pc.astype(BF16), vc_ref[...], preferred_element_type=F32)
        if with_latent:
            pl_ = jnp.exp2(sl_ - m)
            den += jnp.sum(pl_, axis=-1, keepdims=True)
            acc += jnp.dot(pl_.astype(BF16), vl_ref[...], preferred_element_type=F32)
        o_ref[:, sl] = (acc / den).astype(o_ref.dtype)


def _attention(q_rot, k_rot, proj_h, n_batch, seq, ctx_len, latent_queries):
    grp_w = ATT_W // ATT_KV_HEADS
    ctx_blk0 = n_batch * seq // ctx_len
    v_col = H_ATT_V // HEAD_DIM
    if latent_queries:
        tq = ATT_TQ
        nq = seq // tq
        q_map = lambda b, k, i: (b * nq + i, k)
        n_out = n_batch * seq
    else:
        tq = ctx_len
        nq = 1
        q_map = lambda b, k, i: (ctx_blk0 + b, k)
        n_out = n_batch * ctx_len
    in_specs = [pl.BlockSpec((tq, grp_w), q_map),
                pl.BlockSpec((ctx_len, HEAD_DIM), lambda b, k, i: (ctx_blk0 + b, k)),
                pl.BlockSpec((ctx_len, HEAD_DIM), lambda b, k, i: (ctx_blk0 + b, v_col + k))]
    args = [q_rot, k_rot, proj_h]
    if latent_queries:
        in_specs += [pl.BlockSpec((seq, HEAD_DIM), lambda b, k, i: (b, k)),
                     pl.BlockSpec((seq, HEAD_DIM), lambda b, k, i: (b, v_col + k))]
        args += [k_rot, proj_h]
    return pl.pallas_call(
        functools.partial(_att_kernel, with_latent=latent_queries),
        grid=(n_batch, ATT_KV_HEADS, nq),
        in_specs=in_specs,
        out_specs=pl.BlockSpec((tq, grp_w), lambda b, k, i: (b * nq + i, k)),
        out_shape=jax.ShapeDtypeStruct((n_out, ATT_W), BF16),
        compiler_params=_params(3),
        name="attention_lat" if latent_queries else "attention_ctx",
    )(*args)


def _conv3(x_ref, w, c, n_chunks, chunk):
    r0 = pl.multiple_of(c * chunk, chunk)
    x = x_ref[pl.ds(r0, chunk), :].astype(F32)
    prev_blk = x_ref[pl.ds(pl.multiple_of(jnp.maximum(r0 - 16, 0), 16), 16), :].astype(F32)
    next_blk = x_ref[pl.ds(pl.multiple_of(jnp.minimum(r0 + chunk, (n_chunks - 1) * chunk), 16), 16), :].astype(F32)
    prev_row = jnp.where(c > 0, prev_blk[15:16, :], 0.0)
    next_row = jnp.where(c < n_chunks - 1, next_blk[0:1, :], 0.0)
    row = lax.broadcasted_iota(jnp.int32, x.shape, 0)
    x_m1 = jnp.where(row == 0, prev_row, pltpu.roll(x, 1, axis=0))
    x_p1 = jnp.where(row == chunk - 1, next_row, pltpu.roll(x, chunk - 1, axis=0))
    return x_m1 * w[0:1, :] + x * w[1:2, :] + x_p1 * w[2:3, :]


def _ml_chunks(chains):
    n_tok = chains[0][0].shape[0]
    ti = lax.broadcasted_iota(jnp.int32, (n_tok, n_tok), 0)
    si = lax.broadcasted_iota(jnp.int32, (n_tok, n_tok), 1)
    gate = []
    for q, k, v, lfr, lir, lfc, lic, c_ref, n_ref, m_st, rev in chains:
        causal = (si >= ti) if rev else (si <= ti)
        causal_t = (ti >= si) if rev else (ti <= si)
        fc_col = jnp.sum(jnp.where(causal, lfr, 0.0), axis=1, keepdims=True)
        fc_row = jnp.sum(jnp.where(causal_t, lfc, 0.0), axis=0, keepdims=True)
        f_tot = jnp.sum(lfr, axis=1, keepdims=True)
        dmat = jnp.where(causal, fc_col - fc_row + lir, -jnp.inf)
        inter = fc_col + m_st
        m_t = jnp.maximum(inter, jnp.max(dmat, axis=1, keepdims=True))
        w = jnp.exp(dmat - m_t)
        a = jnp.exp(inter - m_t)
        g_col = f_tot - fc_col + lic
        g_row = f_tot - fc_row + lir
        m_new = jnp.maximum(f_tot + m_st, jnp.max(g_row, axis=1, keepdims=True))
        decay = jnp.exp(f_tot + m_st - m_new)
        kw = k.astype(F32) * jnp.exp(g_col - m_new)
        gate.append((w, a, m_t, m_new, decay, kw))
    scs = [lax.dot_general(ch[0], ch[1], _NT, preferred_element_type=F32) * g[0]
           for ch, g in zip(chains, gate)]
    nums = [jnp.dot(sc.astype(BF16), ch[2], preferred_element_type=F32) for ch, sc in zip(chains, scs)]
    carried = [jnp.dot(ch[0], ch[7][...].astype(BF16), preferred_element_type=F32) for ch in chains]
    hs = []
    for ch, g, sc, num, car in zip(chains, gate, scs, nums, carried):
        q, n_ref = ch[0], ch[8]
        w, a, m_t, m_new, decay, kw = g
        den = jnp.sum(sc, axis=1, keepdims=True)
        den += a * jnp.sum(q.astype(F32) * n_ref[...], axis=1, keepdims=True)
        hs.append((num + a * car) / jnp.maximum(jnp.abs(den), jnp.exp(-m_t)))
    for ch, g in zip(chains, gate):
        v, c_ref, n_ref = ch[2], ch[7], ch[8]
        w, a, m_t, m_new, decay, kw = g
        c_ref[...] = decay * c_ref[...] + lax.dot_general(kw.astype(BF16), v, _TN, preferred_element_type=F32)
        n_ref[...] = decay * n_ref[...] + jnp.sum(kw, axis=0, keepdims=True)
    return hs, [g[3] for g in gate]


def _ml_kernel(*refs, seq, ctx_len, need_ctx):
    (ql_ref, kl_ref, vl_ref, ol_ref, qc_ref, kc_ref, vc_ref, oc_ref,
     wq_ref, wk_ref, gcl_ref, grl_ref, gcc_ref, grc_ref, bc_ref, br_ref, ng_ref) = refs[:17]
    if need_ctx:
        yl_ref, yc_ref = refs[17:19]
        scratch = refs[19:]
    else:
        yl_ref = refs[17]
        yc_ref = None
        scratch = refs[18:]
    qsl_ref, ksl_ref, qsc_ref, ksc_ref, hl_ref, hc_ref, cf_ref, nf_ref, cb_ref, nb_ref = scratch
    chunk = ML_CHUNK
    k_scale = ML_DK ** -0.5
    wq = wq_ref[0]
    wk = wk_ref[0]

    def conv_pass(src_q, src_k, dst_q, dst_k, n_chunks):
        def body(c, carry):
            r0 = pl.multiple_of(c * chunk, chunk)
            dst_q[pl.ds(r0, chunk), :] = _conv3(src_q, wq, c, n_chunks, chunk).astype(BF16)
            dst_k[pl.ds(r0, chunk), :] = (_conv3(src_k, wk, c, n_chunks, chunk) * k_scale).astype(BF16)
            return carry
        lax.fori_loop(0, n_chunks, body, 0)

    conv_pass(qc_ref, kc_ref, qsc_ref, ksc_ref, ctx_len // chunk)
    conv_pass(ql_ref, kl_ref, qsl_ref, ksl_ref, seq // chunk)

    bias_c = bc_ref[0]
    bias_r = br_ref[0]
    norm_g = ng_ref[0]

    def run(q_s, k_s, v_s, o_s, gc_s, gr_s, h_s, y_s, n_chunks, m_states):
        def chain(c, rev, c_r, n_r, m_st):
            d = 1 if rev else 0
            rows = pl.ds(pl.multiple_of(c * chunk, chunk), chunk)
            gc = gc_s[0, rows, :] + bias_c
            gr = gr_s[0, :, rows] + bias_r
            lic = gc[:, d:d + 1]
            lfc = _log_sigmoid(gc[:, 2 + d:3 + d])
            lir = gr[d:d + 1, :]
            lfr = _log_sigmoid(gr[2 + d:3 + d, :])
            return rows, (q_s[rows, :], k_s[rows, :], v_s[rows, :], lfr, lir, lfc, lic, c_r, n_r, m_st, rev)

        def finish(rows, h_sum):
            if y_s is not None:
                y = _head_rms(h_sum, norm_g) * jax.nn.sigmoid(o_s[rows, :].astype(F32))
                y_s[rows, :] = y.astype(y_s.dtype)

        def step(ci, m_states, phase):
            rows_f, chain_f = chain(ci, False, cf_ref, nf_ref, m_states[0])
            rows_b, chain_b = chain(n_chunks - 1 - ci, True, cb_ref, nb_ref, m_states[1])
            (h_f, h_b), m_new = _ml_chunks([chain_f, chain_b])
            if phase == 0:
                h_s[rows_f, :] = h_f
                h_s[rows_b, :] = h_b
            elif phase == 1:
                finish(rows_f, h_f + h_b)
            else:
                finish(rows_f, h_s[rows_f, :] + h_f)
                finish(rows_b, h_s[rows_b, :] + h_b)
            return tuple(m_new)

        half = n_chunks // 2
        m_states = lax.fori_loop(0, half, lambda ci, m: step(ci, m, 0), m_states)
        if n_chunks % 2:
            m_states = step(half, m_states, 1)
        return lax.fori_loop(n_chunks - half, n_chunks, lambda ci, m: step(ci, m, 2), m_states)

    for ref in (cf_ref, nf_ref, cb_ref, nb_ref):
        ref[...] = jnp.zeros_like(ref)
    m0 = jnp.full((1, 1), NEG_INIT, F32)
    m1 = run(qsc_ref, ksc_ref, vc_ref, oc_ref, gcc_ref, grc_ref, hc_ref, yc_ref, ctx_len // chunk, (m0, m0))
    run(qsl_ref, ksl_ref, vl_ref, ol_ref, gcl_ref, grl_ref, hl_ref, yl_ref, seq // chunk, m1)


def _mlstm(proj_h, gates_c, gates_r, conv_w, gate_b, norm_g, n_batch, seq, ctx_len, need_ctx):
    w = ML_DK
    cb0 = n_batch * seq // ctx_len
    col = lambda off: off // w
    lat = lambda off: pl.BlockSpec((seq, w), lambda b, h: (b, col(off) + h))
    ctx = lambda off: pl.BlockSpec((ctx_len, w), lambda b, h: (cb0 + b, col(off) + h))
    conv = conv_w.reshape(3, 2 * ML_HEADS, w).transpose(1, 0, 2)
    gb = gate_b.reshape(4, ML_HEADS).T
    in_specs = [lat(H_ML_Q), lat(H_ML_K), lat(H_ML_V), lat(H_ML_O),
                ctx(H_ML_Q), ctx(H_ML_K), ctx(H_ML_V), ctx(H_ML_O),
                pl.BlockSpec((1, 3, w), lambda b, h: (h, 0, 0)),
                pl.BlockSpec((1, 3, w), lambda b, h: (ML_HEADS + h, 0, 0)),
                pl.BlockSpec((1, seq, 4), lambda b, h: (h, b, 0)),
                pl.BlockSpec((1, 4, seq), lambda b, h: (h, 0, b)),
                pl.BlockSpec((1, ctx_len, 4), lambda b, h: (h, cb0 + b, 0)),
                pl.BlockSpec((1, 4, ctx_len), lambda b, h: (h, 0, cb0 + b)),
                pl.BlockSpec((1, 1, 4), lambda b, h: (h, 0, 0)),
                pl.BlockSpec((1, 4, 1), lambda b, h: (h, 0, 0)),
                pl.BlockSpec((1, 1, w), lambda b, h: (h, 0, 0))]
    out_specs = [pl.BlockSpec((seq, w), lambda b, h: (b, h))]
    out_shape = [jax.ShapeDtypeStruct((n_batch * seq, ML_W), BF16)]
    if need_ctx:
        out_specs.append(pl.BlockSpec((ctx_len, w), lambda b, h: (b, h)))
        out_shape.append(jax.ShapeDtypeStruct((n_batch * ctx_len, ML_W), BF16))
    outs = pl.pallas_call(
        functools.partial(_ml_kernel, seq=seq, ctx_len=ctx_len, need_ctx=need_ctx),
        grid=(n_batch, ML_HEADS),
        in_specs=in_specs,
        out_specs=out_specs,
        out_shape=out_shape,
        scratch_shapes=[pltpu.VMEM((seq, w), BF16), pltpu.VMEM((seq, w), BF16),
                        pltpu.VMEM((ctx_len, w), BF16), pltpu.VMEM((ctx_len, w), BF16),
                        pltpu.VMEM((seq, w), F32), pltpu.VMEM((ctx_len, w), F32),
                        pltpu.VMEM((w, w), F32), pltpu.VMEM((1, w), F32),
                        pltpu.VMEM((w, w), F32), pltpu.VMEM((1, w), F32)],
        compiler_params=_params(2),
        name="mlstm",
    )(proj_h, proj_h, proj_h, proj_h, proj_h, proj_h, proj_h, proj_h,
      conv, conv, gates_c, gates_r, gates_c, gates_r,
      gb.reshape(ML_HEADS, 1, 4), gb.reshape(ML_HEADS, 4, 1), norm_g.reshape(ML_HEADS, 1, w))
    return outs if need_ctx else (outs[0], None)


def _hg_level_tables(chunk, rev):
    levels = []
    n = HG_LEAF
    while 2 * n <= chunk:
        refs = []
        for start in range(0, chunk, 2 * n):
            refs.append((start + n - 1) if rev else (start + n))
        levels.append((n, refs))
        n *= 2
    return levels


def _hg_chunks(chains, loglb, log1mlb, one_m_lb):
    chunk = chains[0][1].shape[0]
    nb = chunk // HG_LEAF
    t_idx = lax.broadcasted_iota(jnp.int32, (1, HG_LEAF, 1), 1)

    qs, ks, lks, vfs, gcums = [], [], [], [], []
    for qraw, z, v, st_ref, masks, tri, rev in chains:
        e = jnp.exp(-jnp.abs(z))
        one_pe = 1.0 + e
        lsig = jnp.minimum(z, 0.0) - jnp.log(one_pe)
        a = log1mlb + lsig
        lf = jnp.maximum(loglb, a) + jnp.log(1.0 + jnp.exp(-jnp.abs(loglb - a)))
        ks.append(one_m_lb * (jnp.where(z >= 0.0, e, 1.0) / one_pe))
        lks.append((a - z) * LOG2E)
        gcums.append(jnp.dot(tri, lf * LOG2E, precision=lax.Precision.HIGHEST, preferred_element_type=F32))
        x = qraw.astype(F32)
        qs.append(x / (1.0 + jnp.exp(-x)))
        vfs.append(v.astype(F32))

    atts = [jnp.zeros((chunk, chunk), F32) for _ in chains]
    for lvl, (n, _) in enumerate(_hg_level_tables(chunk, False)):
        for c, (qraw, z, v, st_ref, masks, tri, rev) in enumerate(chains):
            refs = _hg_level_tables(chunk, rev)[lvl][1]
            gref = jnp.concatenate(
                [jnp.broadcast_to(gcums[c][r:r + 1, :], (2 * n, HG_DK)) for r in refs], axis=0)
            dec = jnp.exp2(-jnp.abs(gcums[c] - gref))
            p = lax.dot_general((qs[c] * dec).astype(BF16), (ks[c] * dec).astype(BF16), _NT,
                                preferred_element_type=F32)
            atts[c] += jnp.where(masks[lvl], p, 0.0)
    outs = [jnp.dot(atts[c].astype(BF16), chain[2], preferred_element_type=F32)
            for c, chain in enumerate(chains)]

    q_decs, k_decs, tot_decs = [], [], []
    for c, (qraw, z, v, st_ref, masks, tri, rev) in enumerate(chains):
        g_tot = gcums[c][0:1, :] if rev else gcums[c][chunk - 1:chunk, :]
        q_decs.append((qs[c] * jnp.exp2(gcums[c])).astype(BF16))
        k_decs.append((ks[c] * jnp.exp2(g_tot - gcums[c])).astype(BF16))
        tot_decs.append(jnp.exp2(g_tot))

    for c, (qraw, z, v, st_ref, masks, tri, rev) in enumerate(chains):
        g3 = gcums[c].reshape(nb, HG_LEAF, HG_DK)
        h3 = (lks[c] - gcums[c]).reshape(nb, HG_LEAF, HG_DK)
        q3 = qs[c].reshape(nb, HG_LEAF, HG_DK)
        v3 = vfs[c].reshape(nb, HG_LEAF, HG_DK)
        leaf = jnp.zeros((nb, HG_LEAF, HG_DK), F32)
        for s in range(HG_LEAF):
            col = jnp.sum(q3 * jnp.exp2(g3 + h3[:, s:s + 1, :]), axis=2, keepdims=True)
            valid = (t_idx <= s) if rev else (t_idx >= s)
            leaf += jnp.where(valid, col, 0.0) * v3[:, s:s + 1, :]
        outs[c] += leaf.reshape(chunk, HG_DK)

    for c, (qraw, z, v, st_ref, masks, tri, rev) in enumerate(chains):
        st = st_ref[...]
        outs[c] += lax.dot_general(q_decs[c], st.astype(BF16), _NT, preferred_element_type=F32)
        st_ref[...] = st * tot_decs[c] + lax.dot_general(v, k_decs[c], _TN, preferred_element_type=F32)
    return outs


def _hg_kernel(*refs, seq, ctx_len, need_ctx):
    (ql_ref, il_ref, gl_ref, ffl_ref, fbl_ref, qc_ref, ic_ref, gc_ref, ffc_ref, fbc_ref,
     llb_ref, l1m_ref, oml_ref, ng_ref) = refs[:14]
    if need_ctx:
        yl_ref, yc_ref = refs[14:16]
        scratch = refs[16:]
    else:
        yl_ref = refs[14]
        yc_ref = None
        scratch = refs[15:]
    ol_ref, oc_ref, stf_ref, stb_ref = scratch
    chunk = HG_CHUNK
    loglb = llb_ref[0]
    log1mlb = l1m_ref[0]
    one_m_lb = oml_ref[0]
    norm_g = ng_ref[0]
    ti = lax.broadcasted_iota(jnp.int32, (chunk, chunk), 0)
    si = lax.broadcasted_iota(jnp.int32, (chunk, chunk), 1)

    def direction_consts(rev):
        tri = ((si >= ti) if rev else (si <= ti)).astype(F32)
        masks = []
        for n, _ in _hg_level_tables(chunk, rev):
            shift = (2 * n).bit_length() - 1
            same = jnp.right_shift(ti, shift) == jnp.right_shift(si, shift)
            t_late = jnp.bitwise_and(ti, 2 * n - 1) >= n
            s_late = jnp.bitwise_and(si, 2 * n - 1) >= n
            if rev:
                masks.append(same & jnp.logical_not(t_late) & s_late)
            else:
                masks.append(same & t_late & jnp.logical_not(s_late))
        return tri, masks

    tri_f, masks_f = direction_consts(False)
    tri_b, masks_b = direction_consts(True)

    def run(q_s, i_s, g_s, zf_s, zb_s, o_s, y_s, n_chunks):
        def step(pi, second):
            rows = []
            chains = []
            for u in range(HG_UNROLL):
                c = pi * HG_UNROLL + u
                r_f = pl.ds(pl.multiple_of(c * chunk, chunk), chunk)
                rows.append(r_f)
                chains.append((q_s[r_f, :], zf_s[r_f, :], i_s[r_f, :], stf_ref, masks_f, tri_f, False))
            for u in range(HG_UNROLL):
                c = n_chunks - 1 - (pi * HG_UNROLL + u)
                r_b = pl.ds(pl.multiple_of(c * chunk, chunk), chunk)
                rows.append(r_b)
                chains.append((q_s[r_b, :], zb_s[r_b, :], i_s[r_b, :], stb_ref, masks_b, tri_b, True))
            outs = _hg_chunks(chains, loglb, log1mlb, one_m_lb)
            for r, o in zip(rows, outs):
                if not second:
                    o_s[r, :] = o
                elif y_s is not None:
                    y = _head_rms(o_s[r, :] + o, norm_g) * _silu(g_s[r, :].astype(F32))
                    y_s[r, :] = y.astype(y_s.dtype)

        def first(pi, carry):
            step(pi, False)
            return carry

        def second(pi, carry):
            step(pi, True)
            return carry

        n_steps = n_chunks // HG_UNROLL
        lax.fori_loop(0, n_steps // 2, first, 0)
        lax.fori_loop(n_steps // 2, n_steps, second, 0)

    stf_ref[...] = jnp.zeros_like(stf_ref)
    stb_ref[...] = jnp.zeros_like(stb_ref)
    run(qc_ref, ic_ref, gc_ref, ffc_ref, fbc_ref, oc_ref, yc_ref, ctx_len // chunk)
    run(ql_ref, il_ref, gl_ref, ffl_ref, fbl_ref, ol_ref, yl_ref, seq // chunk)


def _hgrn2(proj_h, proj_f, lb, norm_g, n_batch, seq, ctx_len, need_ctx):
    w = HG_DK
    cb0 = n_batch * seq // ctx_len
    lat = lambda off: pl.BlockSpec((seq, w), lambda b, h: (b, off // w + h))
    ctx = lambda off: pl.BlockSpec((ctx_len, w), lambda b, h: (cb0 + b, off // w + h))
    vec = pl.BlockSpec((1, 1, w), lambda b, h: (h, 0, 0))
    lb3 = lb.reshape(HG_HEADS, 1, w)
    in_specs = [lat(H_HG_Q), lat(H_HG_I), lat(H_HG_G), lat(F_HG_FF), lat(F_HG_FB),
                ctx(H_HG_Q), ctx(H_HG_I), ctx(H_HG_G), ctx(F_HG_FF), ctx(F_HG_FB),
                vec, vec, vec, vec]
    out_specs = [pl.BlockSpec((seq, w), lambda b, h: (b, h))]
    out_shape = [jax.ShapeDtypeStruct((n_batch * seq, HG_W), BF16)]
    if need_ctx:
        out_specs.append(pl.BlockSpec((ctx_len, w), lambda b, h: (b, h)))
        out_shape.append(jax.ShapeDtypeStruct((n_batch * ctx_len, HG_W), BF16))
    outs = pl.pallas_call(
        functools.partial(_hg_kernel, seq=seq, ctx_len=ctx_len, need_ctx=need_ctx),
        grid=(n_batch, HG_HEADS),
        in_specs=in_specs,
        out_specs=out_specs,
        out_shape=out_shape,
        scratch_shapes=[pltpu.VMEM((seq, w), F32), pltpu.VMEM((ctx_len, w), F32),
                        pltpu.VMEM((w, w), F32), pltpu.VMEM((w, w), F32)],
        compiler_params=_params(2),
        name="hgrn2",
    )(proj_h, proj_h, proj_h, proj_f, proj_f, proj_h, proj_h, proj_h, proj_f, proj_f,
      jnp.log(lb3), jnp.log1p(-lb3), 1.0 - lb3, norm_g.reshape(HG_HEADS, 1, w))
    return outs if need_ctx else (outs[0], None)


def _router_kernel(x_ref, g_ref, sh_ref, sc_ref, wr_ref, h_ref, idx_ref, wt_ref, cnt_ref, carry_ref):
    @pl.when(pl.program_id(0) == 0)
    def _():
        carry_ref[...] = jnp.zeros_like(carry_ref)

    x = x_ref[...]
    y = x * lax.rsqrt(jnp.mean(x * x, axis=-1, keepdims=True) + EPS) * g_ref[...]
    h = y * (1.0 + sc_ref[0]) + sh_ref[0]
    h_ref[...] = h.astype(h_ref.dtype)
    logits = jnp.dot(h, wr_ref[...], precision=lax.Precision.HIGHEST, preferred_element_type=F32)
    lane = lax.broadcasted_iota(jnp.int32, logits.shape, 1).astype(F32)
    lg = jnp.where(lane < N_EXPERTS, logits, -jnp.inf)
    m1 = jnp.max(lg, axis=-1, keepdims=True)
    i1 = jnp.min(jnp.where(lg == m1, lane, float(LANE)), axis=-1, keepdims=True)
    lg2 = jnp.where(lane == i1, -jnp.inf, lg)
    m2 = jnp.max(lg2, axis=-1, keepdims=True)
    i2 = jnp.min(jnp.where(lg2 == m2, lane, float(LANE)), axis=-1, keepdims=True)
    e = jnp.exp(m2 - m1)
    w1 = 1.0 / (1.0 + e)
    w2 = e / (1.0 + e)
    wt_ref[...] = jnp.where(lane == 0, w1, jnp.where(lane == 1, w2, 0.0))
    n_tok = x.shape[0]
    routed = jnp.where(jnp.logical_or(lane == i1, lane == i2), 1.0, 0.0)
    ti = lax.broadcasted_iota(jnp.int32, (n_tok, n_tok), 0)
    ui = lax.broadcasted_iota(jnp.int32, (n_tok, n_tok), 1)
    earlier = jnp.where(ui < ti, 1.0, 0.0).astype(BF16)
    before = carry_ref[...] + jnp.dot(earlier, routed.astype(BF16), preferred_element_type=F32)
    r1 = jnp.sum(jnp.where(lane == i1, before, 0.0), axis=-1, keepdims=True)
    r2 = jnp.sum(jnp.where(lane == i2, before, 0.0), axis=-1, keepdims=True)
    carry_ref[...] += jnp.sum(routed, axis=0, keepdims=True)
    cnt_ref[...] = jnp.broadcast_to(carry_ref[...], cnt_ref.shape).astype(jnp.int32)
    idx_ref[...] = jnp.where(lane == 0, i1, jnp.where(lane == 1, i2, jnp.where(
        lane == 2, r1, jnp.where(lane == 3, r2, 0.0)))).astype(jnp.int32)


def _router(x, g, shift, scale, w_router, n_rows, seq, n_batch):
    d = x.shape[1]
    tm = 256
    tpb = seq // tm
    grp = lambda i: (_group_of(i, tpb, n_batch), 0, 0)
    wr = jnp.pad(w_router, ((0, 0), (0, LANE - N_EXPERTS)))
    return pl.pallas_call(
        _router_kernel,
        grid=(n_rows // tm,),
        in_specs=[pl.BlockSpec((tm, d), lambda i: (i, 0)),
                  pl.BlockSpec((1, d), lambda i: (0, 0)),
                  pl.BlockSpec((1, 1, d), grp),
                  pl.BlockSpec((1, 1, d), grp),
                  pl.BlockSpec((d, LANE), lambda i: (0, 0))],
        out_specs=[pl.BlockSpec((tm, d), lambda i: (i, 0)),
                   pl.BlockSpec((tm, LANE), lambda i: (i, 0)),
                   pl.BlockSpec((tm, LANE), lambda i: (i, 0)),
                   pl.BlockSpec((8, LANE), lambda i: (0, 0))],
        out_shape=[jax.ShapeDtypeStruct((n_rows, d), F32),
                   jax.ShapeDtypeStruct((n_rows, LANE), jnp.int32),
                   jax.ShapeDtypeStruct((n_rows, LANE), F32),
                   jax.ShapeDtypeStruct((8, LANE), jnp.int32)],
        scratch_shapes=[pltpu.VMEM((1, LANE), F32)],
        compiler_params=_params(1),
        name="router",
    )(x, g.reshape(1, d), shift, scale, wr)


def _row_copy(src_hbm, buf_ref, sem, src_row, r):
    return pltpu.make_async_copy(src_hbm.at[pl.ds(src_row, 1)], buf_ref.at[pl.ds(r, 1)], sem)


def _dispatch_kernel(src_ref, nu_ref, h_hbm, o_ref, buf_ref, sem):
    t = pl.program_id(0)
    tm = buf_ref.shape[0]

    @pl.when(t < nu_ref[0])
    def _():
        base = t * tm

        def start(r, carry):
            _row_copy(h_hbm, buf_ref, sem, src_ref[base + r], r).start()
            return carry

        def wait(r, carry):
            _row_copy(h_hbm, buf_ref, sem, src_ref[base + r], r).wait()
            return carry

        lax.fori_loop(0, tm, start, 0, unroll=8)
        lax.fori_loop(0, tm, wait, 0, unroll=8)
        o_ref[...] = buf_ref[...].astype(o_ref.dtype)

    @pl.when(t >= nu_ref[0])
    def _():
        o_ref[...] = jnp.zeros_like(o_ref)


def _dispatch(h, src_token, n_used):
    d = h.shape[1]
    tm = MOE_TILE
    n_pad = src_token.shape[0]
    return pl.pallas_call(
        _dispatch_kernel,
        grid_spec=pltpu.PrefetchScalarGridSpec(
            num_scalar_prefetch=2,
            grid=(n_pad // tm,),
            in_specs=[pl.BlockSpec(memory_space=pl.ANY)],
            out_specs=pl.BlockSpec((tm, d), lambda t, src, nu: (t, 0)),
            scratch_shapes=[pltpu.VMEM((tm, d), F32), pltpu.SemaphoreType.DMA]),
        out_shape=jax.ShapeDtypeStruct((n_pad, d), BF16),
        compiler_params=_params(1),
        name="moe_dispatch",
    )(src_token, n_used, h)


def _new_expert(te_ref, i):
    return jnp.logical_or(i == 0, te_ref[i] != te_ref[jnp.maximum(i - 1, 0)])


def _moe_kernel(te_ref, nu_ref, nxt_ref, a_ref, *refs, n_w, glu):
    w_hbm = refs[:n_w]
    o_ref = refs[n_w]
    wbuf, wbf, sem, slot_ref = refs[n_w + 1:]
    j, i = pl.program_id(0), pl.program_id(1)
    n_j, n_i = pl.num_programs(0), pl.num_programs(1)
    tn = o_ref.shape[1]

    def weight_copies(slot, expert, col_tile):
        cols = pl.ds(pl.multiple_of(col_tile * tn, tn), tn)
        return [pltpu.make_async_copy(w_hbm[x].at[expert, :, cols], wbuf.at[slot, x], sem.at[slot, x])
                for x in range(n_w)]

    @pl.when(jnp.logical_and(j == 0, i == 0))
    def _():
        slot_ref[0] = 0
        for cp in weight_copies(0, te_ref[0], 0):
            cp.start()

    @pl.when(_new_expert(te_ref, i))
    def _():
        slot = slot_ref[0]
        for cp in weight_copies(slot, te_ref[i], j):
            cp.wait()
        nxt = nxt_ref[i]
        same_col = nxt < n_i

        @pl.when(jnp.logical_or(same_col, j + 1 < n_j))
        def _():
            nxt_expert = te_ref[jnp.where(same_col, nxt, 0)]
            for cp in weight_copies(1 - slot, nxt_expert, jnp.where(same_col, j, j + 1)):
                cp.start()

        for x in range(n_w):
            wbf[x] = wbuf[slot, x].astype(BF16)
        slot_ref[0] = 1 - slot

    @pl.when(i < nu_ref[0])
    def _():
        a = a_ref[...]
        if glu:
            g = jnp.dot(a, wbf[0], preferred_element_type=F32)
            u = jnp.dot(a, wbf[1], preferred_element_type=F32)
            o_ref[...] = (_silu(g) * u).astype(o_ref.dtype)
        else:
            o_ref[...] = jnp.dot(a, wbf[0], preferred_element_type=F32).astype(o_ref.dtype)

    @pl.when(i >= nu_ref[0])
    def _():
        o_ref[...] = jnp.zeros_like(o_ref)


def _moe_grouped(a, weights, tile_expert, n_used, next_run, glu, out_dtype, name):
    n_rows, k = a.shape
    n = weights[0].shape[2]
    n_w = len(weights)
    tm, tn = MOE_TILE, COL_TILE
    a_map = lambda j, i, te, nu, nxt: (jnp.maximum(jnp.minimum(i, nu[0] - 1), 0), 0)
    return pl.pallas_call(
        functools.partial(_moe_kernel, n_w=n_w, glu=glu),
        grid_spec=pltpu.PrefetchScalarGridSpec(
            num_scalar_prefetch=3,
            grid=(n // tn, n_rows // tm),
            in_specs=[pl.BlockSpec((tm, k), a_map)] + [pl.BlockSpec(memory_space=pl.ANY)] * n_w,
            out_specs=pl.BlockSpec((tm, tn), lambda j, i, te, nu, nxt: (i, j)),
            scratch_shapes=[pltpu.VMEM((2, n_w, k, tn), F32), pltpu.VMEM((n_w, k, tn), BF16),
                            pltpu.SemaphoreType.DMA((2, n_w)), pltpu.SMEM((1,), jnp.int32)]),
        out_shape=jax.ShapeDtypeStruct((n_rows, n), out_dtype),
        compiler_params=_params(2),
        name=name,
    )(tile_expert, n_used, next_run, a, *weights)


def _combine_kernel(x_ref, ya_ref, yb_ref, wt_ref, g_ref, o_ref):
    wt = wt_ref[...]
    y = wt[:, 0:1] * ya_ref[...].astype(F32) + wt[:, 1:2] * yb_ref[...].astype(F32)
    o_ref[...] = x_ref[...] + g_ref[0] * y


def _moe_combine(x, ya, yb, wt, gate, n_rows, seq, n_batch):
    d = x.shape[1]
    tm = 256
    tpb = seq // tm
    row = pl.BlockSpec((tm, d), lambda i: (i, 0))
    return pl.pallas_call(
        _combine_kernel,
        grid=(n_rows // tm,),
        in_specs=[row, row, row,
                  pl.BlockSpec((tm, LANE), lambda i: (i, 0)),
                  pl.BlockSpec((1, 1, d), lambda i: (_group_of(i, tpb, n_batch), 0, 0))],
        out_specs=row,
        out_shape=jax.ShapeDtypeStruct((n_rows, d), F32),
        compiler_params=_params(1),
        name="moe_combine",
    )(x, ya, yb, wt, gate)


def _moe_ffn(x, g, shift, scale, gate, w_router, wg, wu, wd, n_rows, seq, n_batch):
    h, idx, wt, cnt = _router(x, g, shift, scale, w_router, n_rows, seq, n_batch)
    tm = MOE_TILE
    n_tiles = 2 * n_rows // tm + N_EXPERTS
    counts = cnt[0, :N_EXPERTS]
    tiles = (counts + tm - 1) // tm
    tile_end = jnp.cumsum(tiles)
    tile_start = tile_end - tiles
    expert = jnp.concatenate([idx[:, 0], idx[:, 1]])
    rank = jnp.concatenate([idx[:, 2], idx[:, 3]])
    token = jnp.concatenate([jnp.arange(n_rows, dtype=jnp.int32)] * 2)
    one_hot = (expert[:, None] == jnp.arange(N_EXPERTS, dtype=jnp.int32)[None, :]).astype(jnp.int32)
    dest = jnp.sum(tile_start[None, :] * one_hot, axis=1) * tm + rank
    src_token = jnp.zeros((n_tiles * tm,), jnp.int32).at[dest].set(token)
    n_used = tile_end[-1:].astype(jnp.int32)
    tile_id = jnp.minimum(jnp.arange(n_tiles, dtype=jnp.int32), n_used[0] - 1)
    tile_expert = jnp.sum((tile_end[None, :] <= tile_id[:, None]).astype(jnp.int32), axis=1)

    a_sorted = _dispatch(h, src_token, n_used)
    later = jnp.arange(n_tiles, dtype=jnp.int32)
    other = jnp.logical_and(later[None, :] > later[:, None], tile_expert[None, :] != tile_expert[:, None])
    next_run = jnp.min(jnp.where(other, later[None, :], n_tiles), axis=1).astype(jnp.int32)
    u = _moe_grouped(a_sorted, (wg, wu), tile_expert, n_used, next_run, True, BF16, "moe_up")
    y = _moe_grouped(u, (wd,), tile_expert, n_used, next_run, False, BF16, "moe_down")
    ya = jnp.take(y, dest[:n_rows], axis=0, mode="clip")
    yb = jnp.take(y, dest[n_rows:], axis=0, mode="clip")
    return _moe_combine(x, ya, yb, wt, gate, n_rows, seq, n_batch)


def _rope_tables(n_batch, seq, n_ctx_rows):
    t = jnp.arange(seq)
    pos = jnp.stack([t // GRID_W, t % GRID_W], axis=-1).astype(F32)
    inv_freq = jnp.exp(-jnp.log(ROPE_BASE) * jnp.arange(0, ROPE_AXIS_DIM, 2, dtype=F32) / ROPE_AXIS_DIM)
    ang = pos[..., None] * inv_freq
    cos, sin = jnp.cos(ang), jnp.sin(ang)
    cos = jnp.concatenate([cos[:, 0], cos[:, 0], cos[:, 1], cos[:, 1]], axis=-1)
    sin = jnp.concatenate([-sin[:, 0], sin[:, 0], -sin[:, 1], sin[:, 1]], axis=-1)
    cos = jnp.concatenate([jnp.tile(cos, (n_batch, 1)), jnp.ones((n_ctx_rows, HEAD_DIM), F32)], axis=0)
    sin = jnp.concatenate([jnp.tile(sin, (n_batch, 1)), jnp.zeros((n_ctx_rows, HEAD_DIM), F32)], axis=0)
    return cos, sin


def kernel(x, c, ctx, c_ctx, w_mod, b_mod, norm1_g, norm2_g, w_in, attn_q_norm_g, attn_k_norm_g, ml_conv_w,
           ml_gate_b, ml_norm_g, hg_lb_logits, hg_norm_g, w_branch, w_out, ffn_w_gate, ffn_w_up, ffn_w_down,
           moe_w_router, moe_w_gate, moe_w_up, moe_w_down):
    n_batch, seq, d = x.shape
    ctx_len = ctx.shape[1]
    depth = w_mod.shape[0]
    m_lat = n_batch * seq
    m_ctx = n_batch * ctx_len
    m_all = m_lat + m_ctx
    assert d == D_MODEL and w_in.shape[2] == _N_IN
    assert seq % ROW_TILE == 0 and m_ctx % ROW_TILE == 0 and seq % ctx_len == 0
    assert seq % ML_CHUNK == 0 and ctx_len % ML_CHUNK == 0 and ctx_len % HG_CHUNK == 0

    xa = jnp.concatenate([x.reshape(m_lat, d), ctx.reshape(m_ctx, d)], axis=0)
    cs = jnp.concatenate([c, c_ctx[None], jnp.zeros((8 - n_batch - 1, d), F32)], axis=0)
    mod = _modulation(cs, w_mod, b_mod)
    cos, sin = _rope_tables(n_batch, seq, m_ctx)
    lb_all = jnp.cumsum(jax.nn.softmax(hg_lb_logits.astype(F32), axis=0), axis=0)
    lb_all = lb_all - lb_all[:1]
    w_in_t = jnp.swapaxes(w_in, 1, 2)

    for l in range(depth):
        need_ctx = l < depth - 1
        n_rows = m_all if need_ctx else m_lat
        mods = [mod[l, :n_batch + 1, i * d:(i + 1) * d].reshape(n_batch + 1, 1, d) for i in range(N_MOD)]

        h = _norm_mod(xa, norm1_g[l], mods[0], mods[1], m_all, seq, n_batch)
        proj_h = _proj_in(h, w_in_t, l, *H_TILES, BF16)
        proj_f = _proj_in(h, w_in_t, l, *F_TILES, F32)

        q_rot, k_rot = _att_prep(proj_h, cos, sin, attn_q_norm_g[l], attn_k_norm_g[l])
        ya = _attention(q_rot, k_rot, proj_h, n_batch, seq, ctx_len, True)

        gates = proj_f[:, F_ML_G:F_ML_G + 4 * ML_HEADS].reshape(m_all, 4, ML_HEADS)
        gates_c = gates.transpose(2, 0, 1)
        gates_r = gates.transpose(2, 1, 0)
        ym, ym_c = _mlstm(proj_h, gates_c, gates_r, ml_conv_w[l], ml_gate_b[l], ml_norm_g[l],
                          n_batch, seq, ctx_len, need_ctx)
        yh, yh_c = _hgrn2(proj_h, proj_f, lb_all[l], hg_norm_g[l], n_batch, seq, ctx_len, need_ctx)
        ys_ctx = None
        if need_ctx:
            ya_c = _attention(q_rot, k_rot, proj_h, n_batch, seq, ctx_len, False)
            ys_ctx = (ya_c, ym_c, yh_c)

        merged = _branch_merge((ya, ym, yh), ys_ctx, w_branch, l, proj_h, n_rows)
        xa = _matmul_residual(merged, w_out, l, xa, mods[2], n_rows, seq, n_batch, "proj_out")

        if l % 2 == 0:
            h2 = _norm_mod(xa, norm2_g[l], mods[3], mods[4], n_rows, seq, n_batch)
            u = _glu(h2, ffn_w_gate, ffn_w_up, l // 2)
            xa = _matmul_residual(u, ffn_w_down, l // 2, xa, mods[5], n_rows, seq, n_batch, "ffn_down")
        else:
            xa = _moe_ffn(xa, norm2_g[l], mods[3], mods[4], mods[5], moe_w_router[l // 2],
                          moe_w_gate[l // 2], moe_w_up[l // 2], moe_w_down[l // 2], n_rows, seq, n_batch)
    return xa[:m_lat].reshape(n_batch, seq, d)
```

```python
import functools

import jax
import jax.numpy as jnp
from jax import lax
from jax.experimental import pallas as pl
from jax.experimental.pallas import tpu as pltpu

F32 = jnp.float32
BF16 = jnp.bfloat16

D_MODEL = 2048
GRID_W = 64
N_MOD = 6
ATT_HEADS = 8
ATT_KV_HEADS = 2
HEAD_DIM = 128
ATT_W = ATT_HEADS * HEAD_DIM
KV_W = ATT_KV_HEADS * HEAD_DIM
ROPE_AXIS_DIM = HEAD_DIM // 2
ROPE_BASE = 10000.0
ML_HEADS = 4
ML_DK = 256
ML_W = ML_HEADS * ML_DK
HG_HEADS = 8
HG_DK = 128
HG_W = HG_HEADS * HG_DK
D_FF = 5632
N_EXPERTS = 8
EPS = 1e-6
NEG_INIT = -1e30
LOG2E = 1.4426950408889634

_O_ATT_Q = 0
_O_ML_GATES = ATT_W + 2 * KV_W + 4 * ML_W
_O_HG_Q = _O_ML_GATES + 4 * ML_HEADS
_O_HG_FF = _O_HG_Q + HG_W
_O_HG_I = _O_HG_FF + 2 * HG_W
_O_MERGE = _O_HG_I + 2 * HG_W
_N_IN = _O_MERGE + 3 * D_MODEL

H_ATT_Q = 0
H_ATT_K = 1024
H_ATT_V = 1280
H_ML_Q = 1536
H_ML_K = 2560
H_ML_V = 3584
H_ML_O = 4608
H_HG_Q = 5632
H_HG_I = 6656
H_HG_G = 7680
H_MERGE = 8704
H_WIDTH = 14848
F_HG_FF = 0
F_HG_FB = 1024
F_ML_G = 2048

LANE = 128
ROW_TILE = 512
COL_TILE = 512
PROJ_ROW_TILES = 4
W_MISALIGN = _O_HG_Q % LANE


def _source_tiles(segments):
    blocks, shifted = [], []
    for off, width in segments:
        for start in range(off, off + width, COL_TILE):
            assert start % COL_TILE in (0, W_MISALIGN)
            blocks.append(start // COL_TILE)
            shifted.append(int(start % COL_TILE != 0))
    return blocks, shifted


H_TILES = _source_tiles([(0, _O_ML_GATES), (_O_HG_Q, HG_W), (_O_HG_I, 2 * HG_W + 3 * D_MODEL)])
F_TILES = _source_tiles([(_O_HG_FF, 2 * HG_W), (_O_ML_GATES, COL_TILE)])
VMEM_LIMIT = 56 * 1024 * 1024

ML_CHUNK = 256
HG_CHUNK = 64
HG_LEAF = 8
HG_UNROLL = 2
ATT_TQ = 256
MOE_TILE = 512
MOE_SUB_ROWS = 128

_NT = (((1,), (1,)), ((), ()))
_TN = (((0,), (0,)), ((), ()))


def _params(n_axes):
    return pltpu.CompilerParams(dimension_semantics=("arbitrary",) * n_axes,
                                vmem_limit_bytes=VMEM_LIMIT)


def _silu(x):
    return x * jax.nn.sigmoid(x)


def _log_sigmoid(x):
    return jnp.minimum(x, 0.0) - jnp.log1p(jnp.exp(-jnp.abs(x)))


def _mod_kernel(c_ref, w_ref, b_ref, o_ref):
    a = _silu(c_ref[...])
    o_ref[...] = jnp.dot(a.astype(BF16), w_ref[...].astype(BF16), preferred_element_type=F32) + b_ref[...]


def _modulation(cs, w_mod, b_mod):
    depth, d, n = w_mod.shape
    tn = 1024
    return pl.pallas_call(
        _mod_kernel,
        grid=(depth, n // tn),
        in_specs=[pl.BlockSpec((8, d), lambda l, j: (0, 0)),
                  pl.BlockSpec((None, d, tn), lambda l, j: (l, 0, j)),
                  pl.BlockSpec((None, 1, tn), lambda l, j: (l, 0, j))],
        out_specs=pl.BlockSpec((None, 8, tn), lambda l, j: (l, 0, j)),
        out_shape=jax.ShapeDtypeStruct((depth, 8, n), F32),
        compiler_params=_params(2),
        name="modulation",
    )(cs, w_mod, b_mod.reshape(depth, 1, n))


def _norm_kernel(x_ref, g_ref, sh_ref, sc_ref, o_ref):
    x = x_ref[...]
    y = x * lax.rsqrt(jnp.mean(x * x, axis=-1, keepdims=True) + EPS)
    y = y * g_ref[...]
    o_ref[...] = (y * (1.0 + sc_ref[0]) + sh_ref[0]).astype(o_ref.dtype)


def _group_of(i, tiles_per_batch, n_batch):
    return jnp.minimum(i // tiles_per_batch, n_batch)


def _norm_mod(x, g, shift, scale, n_rows, seq, n_batch):
    d = x.shape[1]
    tm = 256
    tpb = seq // tm
    grp = lambda i: (_group_of(i, tpb, n_batch), 0, 0)
    return pl.pallas_call(
        _norm_kernel,
        grid=(n_rows // tm,),
        in_specs=[pl.BlockSpec((tm, d), lambda i: (i, 0)),
                  pl.BlockSpec((1, d), lambda i: (0, 0)),
                  pl.BlockSpec((1, 1, d), grp),
                  pl.BlockSpec((1, 1, d), grp)],
        out_specs=pl.BlockSpec((tm, d), lambda i: (i, 0)),
        out_shape=jax.ShapeDtypeStruct((n_rows, d), BF16),
        compiler_params=_params(1),
        name="norm_mod",
    )(x, g.reshape(1, d), shift, scale)


def _proj_in_kernel(blk_ref, shift_ref, a_ref, wa_ref, wb_ref, o_ref, w_scr):
    j = pl.program_id(0)

    @pl.when(pl.program_id(1) == 0)
    def _():
        @pl.when(shift_ref[j] == 0)
        def _():
            w_scr[...] = wa_ref[...].T.astype(BF16)

        @pl.when(shift_ref[j] != 0)
        def _():
            w = jnp.concatenate([wa_ref[W_MISALIGN:, :], wb_ref[...]], axis=0)
            w_scr[...] = w.T.astype(BF16)

    o_ref[...] = jnp.dot(a_ref[...], w_scr[...], preferred_element_type=F32).astype(o_ref.dtype)


def _proj_in(a, w_in_t, layer, src_blocks, shifted, out_dtype):
    m, k = a.shape
    n_src = w_in_t.shape[1]
    tm, tn = m // PROJ_ROW_TILES, COL_TILE
    n_tiles = len(src_blocks)
    last_blk = n_src // W_MISALIGN - 1
    per = tn // W_MISALIGN
    return pl.pallas_call(
        _proj_in_kernel,
        grid_spec=pltpu.PrefetchScalarGridSpec(
            num_scalar_prefetch=2,
            grid=(n_tiles, m // tm),
            in_specs=[pl.BlockSpec((tm, k), lambda j, i, blk, sh: (i, 0)),
                      pl.BlockSpec((None, tn, k), lambda j, i, blk, sh: (layer, blk[j], 0)),
                      pl.BlockSpec((None, W_MISALIGN, k), lambda j, i, blk, sh: (
                          layer, jnp.minimum(per * (blk[j] + 1), last_blk), 0))],
            out_specs=pl.BlockSpec((tm, tn), lambda j, i, blk, sh: (i, j)),
            scratch_shapes=[pltpu.VMEM((k, tn), BF16)]),
        out_shape=jax.ShapeDtypeStruct((m, n_tiles * tn), out_dtype),
        compiler_params=_params(2),
        name="proj_in",
    )(jnp.asarray(src_blocks, jnp.int32), jnp.asarray(shifted, jnp.int32), a, w_in_t, w_in_t)


def _mm_res_kernel(a_ref, w_ref, r_ref, g_ref, o_ref, wb_ref):
    @pl.when(pl.program_id(1) == 0)
    def _():
        wb_ref[...] = w_ref[...].astype(BF16)

    y = jnp.dot(a_ref[...], wb_ref[...], preferred_element_type=F32)
    o_ref[...] = r_ref[...] + g_ref[0] * y


def _matmul_residual(a, w, layer, res, gate, n_rows, seq, n_batch, name):
    k = a.shape[1]
    n = w.shape[2]
    tm, tn = ROW_TILE, COL_TILE
    tpb = seq // tm
    return pl.pallas_call(
        _mm_res_kernel,
        grid=(n // tn, n_rows // tm),
        in_specs=[pl.BlockSpec((tm, k), lambda j, i: (i, 0)),
                  pl.BlockSpec((None, k, tn), lambda j, i: (layer, 0, j)),
                  pl.BlockSpec((tm, tn), lambda j, i: (i, j)),
                  pl.BlockSpec((1, 1, tn), lambda j, i: (_group_of(i, tpb, n_batch), 0, j))],
        out_specs=pl.BlockSpec((tm, tn), lambda j, i: (i, j)),
        out_shape=jax.ShapeDtypeStruct((n_rows, n), F32),
        scratch_shapes=[pltpu.VMEM((k, tn), BF16)],
        compiler_params=_params(2),
        name=name,
    )(a, w, res, gate)


def _glu_kernel(a_ref, wg_ref, wu_ref, o_ref, wgb_ref, wub_ref):
    @pl.when(pl.program_id(1) == 0)
    def _():
        wgb_ref[...] = wg_ref[...].astype(BF16)
        wub_ref[...] = wu_ref[...].astype(BF16)

    a = a_ref[...]
    g = jnp.dot(a, wgb_ref[...], preferred_element_type=F32)
    u = jnp.dot(a, wub_ref[...], preferred_element_type=F32)
    o_ref[...] = (_silu(g) * u).astype(o_ref.dtype)


def _glu(a, wg, wu, layer):
    m, k = a.shape
    n = wg.shape[2]
    tm, tn = ROW_TILE, COL_TILE
    return pl.pallas_call(
        _glu_kernel,
        grid=(n // tn, m // tm),
        in_specs=[pl.BlockSpec((tm, k), lambda j, i: (i, 0)),
                  pl.BlockSpec((None, k, tn), lambda j, i: (layer, 0, j)),
                  pl.BlockSpec((None, k, tn), lambda j, i: (layer, 0, j))],
        out_specs=pl.BlockSpec((tm, tn), lambda j, i: (i, j)),
        out_shape=jax.ShapeDtypeStruct((m, n), BF16),
        scratch_shapes=[pltpu.VMEM((k, tn), BF16), pltpu.VMEM((k, tn), BF16)],
        compiler_params=_params(2),
        name="ffn_up",
    )(a, wg, wu)


def _merge_kernel(*refs, n_lat_tiles, with_ctx):
    if with_ctx:
        lat_refs, ctx_refs, rest = refs[0:3], refs[3:6], refs[6:]
    else:
        lat_refs, ctx_refs, rest = refs[0:3], None, refs[3:]
    w_ref, g0_ref, g1_ref, g2_ref, o_ref, wb_ref = rest

    @pl.when(pl.program_id(1) == 0)
    def _():
        wb_ref[...] = w_ref[...].astype(BF16)

    is_ctx = pl.program_id(1) >= n_lat_tiles
    acc = None
    for b, g_ref in enumerate((g0_ref, g1_ref, g2_ref)):
        y = lat_refs[b][...]
        if with_ctx:
            y = jnp.where(is_ctx, ctx_refs[b][...], y)
        term = jax.nn.sigmoid(g_ref[...].astype(F32)) * jnp.dot(y, wb_ref[b], preferred_element_type=F32)
        acc = term if acc is None else acc + term
    o_ref[...] = acc.astype(o_ref.dtype)


def _branch_merge(ys_lat, ys_ctx, w_branch, layer, proj_h, n_rows):
    k = ys_lat[0].shape[1]
    n = w_branch.shape[3]
    tm, tn = ROW_TILE, COL_TILE
    n_lat_tiles = ys_lat[0].shape[0] // tm
    with_ctx = ys_ctx is not None
    gate_spec = lambda b: pl.BlockSpec((tm, tn), lambda j, i: (i, (H_MERGE + b * n) // tn + j))
    lat_spec = pl.BlockSpec((tm, k), lambda j, i: (jnp.minimum(i, n_lat_tiles - 1), 0))
    ctx_spec = pl.BlockSpec((tm, k), lambda j, i: (jnp.maximum(i - n_lat_tiles, 0), 0))
    in_specs = [lat_spec] * 3 + ([ctx_spec] * 3 if with_ctx else [])
    in_specs += [pl.BlockSpec((None, 3, k, tn), lambda j, i: (layer, 0, 0, j)),
                 gate_spec(0), gate_spec(1), gate_spec(2)]
    args = list(ys_lat) + (list(ys_ctx) if with_ctx else [])
    return pl.pallas_call(
        functools.partial(_merge_kernel, n_lat_tiles=n_lat_tiles, with_ctx=with_ctx),
        grid=(n // tn, n_rows // tm),
        in_specs=in_specs,
        out_specs=pl.BlockSpec((tm, tn), lambda j, i: (i, j)),
        out_shape=jax.ShapeDtypeStruct((n_rows, n), BF16),
        scratch_shapes=[pltpu.VMEM((3, k, tn), BF16)],
        compiler_params=_params(2),
        name="branch_merge",
    )(*args, w_branch, proj_h, proj_h, proj_h)


def _head_rms(x, g):
    return x * lax.rsqrt(jnp.mean(x * x, axis=-1, keepdims=True) + EPS) * g


def _rope(x, cos, sin, low_half):
    partner = jnp.where(low_half, pltpu.roll(x, 96, axis=1), pltpu.roll(x, 32, axis=1))
    return x * cos + partner * sin


def _att_prep_kernel(q_ref, k_ref, cos_ref, sin_ref, qg_ref, kg_ref, qo_ref, ko_ref):
    cos = cos_ref[...]
    sin = sin_ref[...]
    lane = lax.broadcasted_iota(jnp.int32, cos.shape, 1)
    low_half = jnp.bitwise_and(lane, 63) < 32
    scale = HEAD_DIM ** -0.5 * LOG2E
    for h in range(ATT_HEADS):
        sl = slice(h * HEAD_DIM, (h + 1) * HEAD_DIM)
        x = _head_rms(q_ref[:, sl].astype(F32), qg_ref[...])
        qo_ref[:, sl] = (_rope(x, cos, sin, low_half) * scale).astype(qo_ref.dtype)
    for h in range(ATT_KV_HEADS):
        sl = slice(h * HEAD_DIM, (h + 1) * HEAD_DIM)
        x = _head_rms(k_ref[:, sl].astype(F32), kg_ref[...])
        ko_ref[:, sl] = _rope(x, cos, sin, low_half).astype(ko_ref.dtype)


def _att_prep(proj_h, cos, sin, qg, kg):
    m = proj_h.shape[0]
    tm = ROW_TILE
    return pl.pallas_call(
        _att_prep_kernel,
        grid=(m // tm,),
        in_specs=[pl.BlockSpec((tm, ATT_W), lambda i: (i, H_ATT_Q // ATT_W)),
                  pl.BlockSpec((tm, KV_W), lambda i: (i, H_ATT_K // KV_W)),
                  pl.BlockSpec((tm, HEAD_DIM), lambda i: (i, 0)),
                  pl.BlockSpec((tm, HEAD_DIM), lambda i: (i, 0)),
                  pl.BlockSpec((1, HEAD_DIM), lambda i: (0, 0)),
                  pl.BlockSpec((1, HEAD_DIM), lambda i: (0, 0))],
        out_specs=[pl.BlockSpec((tm, ATT_W), lambda i: (i, 0)),
                   pl.BlockSpec((tm, KV_W), lambda i: (i, 0))],
        out_shape=[jax.ShapeDtypeStruct((m, ATT_W), BF16),
                   jax.ShapeDtypeStruct((m, KV_W), BF16)],
        compiler_params=_params(1),
        name="att_prep",
    )(proj_h, proj_h, cos, sin, qg.reshape(1, HEAD_DIM), kg.reshape(1, HEAD_DIM))


def _att_kernel(*refs, with_latent):
    if with_latent:
        q_ref, kc_ref, vc_ref, kl_ref, vl_ref, o_ref = refs
    else:
        q_ref, kc_ref, vc_ref, o_ref = refs
    grp = ATT_HEADS // ATT_KV_HEADS
    for g in range(grp):
        sl = slice(g * HEAD_DIM, (g + 1) * HEAD_DIM)
        q = q_ref[:, sl]
        sc = lax.dot_general(q, kc_ref[...], _NT, preferred_element_type=F32)
        m = jnp.max(sc, axis=-1, keepdims=True)
        if with_latent:
            sl_ = lax.dot_general(q, kl_ref[...], _NT, preferred_element_type=F32)
            m = jnp.maximum(m, jnp.max(sl_, axis=-1, keepdims=True))
        pc = jnp.exp2(sc - m)
        den = jnp.sum(pc, axis=-1, keepdims=True)
        acc = jnp.dot(pc.astype(BF16), vc_ref[...], preferred_element_type=F32)
        if with_latent:
            pl_ = jnp.exp2(sl_ - m)
            den += jnp.sum(pl_, axis=-1, keepdims=True)
            acc += jnp.dot(pl_.astype(BF16), vl_ref[...], preferred_element_type=F32)
        o_ref[:, sl] = (acc / den).astype(o_ref.dtype)


def _attention(q_rot, k_rot, proj_h, n_batch, seq, ctx_len, latent_queries):
    grp_w = ATT_W // ATT_KV_HEADS
    ctx_blk0 = n_batch * seq // ctx_len
    v_col = H_ATT_V // HEAD_DIM
    if latent_queries:
        tq = ATT_TQ
        nq = seq // tq
        q_map = lambda b, k, i: (b * nq + i, k)
        n_out = n_batch * seq
    else:
        tq = ctx_len
        nq = 1
        q_map = lambda b, k, i: (ctx_blk0 + b, k)
        n_out = n_batch * ctx_len
    in_specs = [pl.BlockSpec((tq, grp_w), q_map),
                pl.BlockSpec((ctx_len, HEAD_DIM), lambda b, k, i: (ctx_blk0 + b, k)),
                pl.BlockSpec((ctx_len, HEAD_DIM), lambda b, k, i: (ctx_blk0 + b, v_col + k))]
    args = [q_rot, k_rot, proj_h]
    if latent_queries:
        in_specs += [pl.BlockSpec((seq, HEAD_DIM), lambda b, k, i: (b, k)),
                     pl.BlockSpec((seq, HEAD_DIM), lambda b, k, i: (b, v_col + k))]
        args += [k_rot, proj_h]
    return pl.pallas_call(
        functools.partial(_att_kernel, with_latent=latent_queries),
        grid=(n_batch, ATT_KV_HEADS, nq),
        in_specs=in_specs,
        out_specs=pl.BlockSpec((tq, grp_w), lambda b, k, i: (b * nq + i, k)),
        out_shape=jax.ShapeDtypeStruct((n_out, ATT_W), BF16),
        compiler_params=_params(3),
        name="attention_lat" if latent_queries else "attention_ctx",
    )(*args)


def _conv3(x_ref, w, c, n_chunks, chunk):
    r0 = pl.multiple_of(c * chunk, chunk)
    x = x_ref[pl.ds(r0, chunk), :].astype(F32)
    prev_blk = x_ref[pl.ds(pl.multiple_of(jnp.maximum(r0 - 16, 0), 16), 16), :].astype(F32)
    next_blk = x_ref[pl.ds(pl.multiple_of(jnp.minimum(r0 + chunk, (n_chunks - 1) * chunk), 16), 16), :].astype(F32)
    prev_row = jnp.where(c > 0, prev_blk[15:16, :], 0.0)
    next_row = jnp.where(c < n_chunks - 1, next_blk[0:1, :], 0.0)
    row = lax.broadcasted_iota(jnp.int32, x.shape, 0)
    x_m1 = jnp.where(row == 0, prev_row, pltpu.roll(x, 1, axis=0))
    x_p1 = jnp.where(row == chunk - 1, next_row, pltpu.roll(x, chunk - 1, axis=0))
    return x_m1 * w[0:1, :] + x * w[1:2, :] + x_p1 * w[2:3, :]


def _ml_chunks(chains):
    n_tok = chains[0][0].shape[0]
    ti = lax.broadcasted_iota(jnp.int32, (n_tok, n_tok), 0)
    si = lax.broadcasted_iota(jnp.int32, (n_tok, n_tok), 1)
    gate = []
    for q, k, v, lfr, lir, lfc, lic, c_ref, n_ref, m_st, rev in chains:
        causal = (si >= ti) if rev else (si <= ti)
        causal_t = (ti >= si) if rev else (ti <= si)
        fc_col = jnp.sum(jnp.where(causal, lfr, 0.0), axis=1, keepdims=True)
        fc_row = jnp.sum(jnp.where(causal_t, lfc, 0.0), axis=0, keepdims=True)
        f_tot = jnp.sum(lfr, axis=1, keepdims=True)
        dmat = jnp.where(causal, fc_col - fc_row + lir, -jnp.inf)
        inter = fc_col + m_st
        m_t = jnp.maximum(inter, jnp.max(dmat, axis=1, keepdims=True))
        w = jnp.exp(dmat - m_t)
        a = jnp.exp(inter - m_t)
        g_col = f_tot - fc_col + lic
        g_row = f_tot - fc_row + lir
        m_new = jnp.maximum(f_tot + m_st, jnp.max(g_row, axis=1, keepdims=True))
        decay = jnp.exp(f_tot + m_st - m_new)
        kw = k.astype(F32) * jnp.exp(g_col - m_new)
        gate.append((w, a, m_t, m_new, decay, kw))
    scs = [lax.dot_general(ch[0], ch[1], _NT, preferred_element_type=F32) * g[0]
           for ch, g in zip(chains, gate)]
    nums = [jnp.dot(sc.astype(BF16), ch[2], preferred_element_type=F32) for ch, sc in zip(chains, scs)]
    carried = [jnp.dot(ch[0], ch[7][...].astype(BF16), preferred_element_type=F32) for ch in chains]
    hs = []
    for ch, g, sc, num, car in zip(chains, gate, scs, nums, carried):
        q, n_ref = ch[0], ch[8]
        w, a, m_t, m_new, decay, kw = g
        den = jnp.sum(sc, axis=1, keepdims=True)
        den += a * jnp.sum(q.astype(F32) * n_ref[...], axis=1, keepdims=True)
        hs.append((num + a * car) / jnp.maximum(jnp.abs(den), jnp.exp(-m_t)))
    for ch, g in zip(chains, gate):
        v, c_ref, n_ref = ch[2], ch[7], ch[8]
        w, a, m_t, m_new, decay, kw = g
        c_ref[...] = decay * c_ref[...] + lax.dot_general(kw.astype(BF16), v, _TN, preferred_element_type=F32)
        n_ref[...] = decay * n_ref[...] + jnp.sum(kw, axis=0, keepdims=True)
    return hs, [g[3] for g in gate]


def _ml_kernel(*refs, seq, ctx_len, need_ctx):
    (ql_ref, kl_ref, vl_ref, ol_ref, qc_ref, kc_ref, vc_ref, oc_ref,
     wq_ref, wk_ref, gcl_ref, grl_ref, gcc_ref, grc_ref, bc_ref, br_ref, ng_ref) = refs[:17]
    if need_ctx:
        yl_ref, yc_ref = refs[17:19]
        scratch = refs[19:]
    else:
        yl_ref = refs[17]
        yc_ref = None
        scratch = refs[18:]
    qsl_ref, ksl_ref, qsc_ref, ksc_ref, hl_ref, hc_ref, cf_ref, nf_ref, cb_ref, nb_ref = scratch
    chunk = ML_CHUNK
    k_scale = ML_DK ** -0.5
    wq = wq_ref[0]
    wk = wk_ref[0]

    def conv_pass(src_q, src_k, dst_q, dst_k, n_chunks):
        def body(c, carry):
            r0 = pl.multiple_of(c * chunk, chunk)
            dst_q[pl.ds(r0, chunk), :] = _conv3(src_q, wq, c, n_chunks, chunk).astype(BF16)
            dst_k[pl.ds(r0, chunk), :] = (_conv3(src_k, wk, c, n_chunks, chunk) * k_scale).astype(BF16)
            return carry
        lax.fori_loop(0, n_chunks, body, 0)

    conv_pass(qc_ref, kc_ref, qsc_ref, ksc_ref, ctx_len // chunk)
    conv_pass(ql_ref, kl_ref, qsl_ref, ksl_ref, seq // chunk)

    bias_c = bc_ref[0]
    bias_r = br_ref[0]
    norm_g = ng_ref[0]

    def run(q_s, k_s, v_s, o_s, gc_s, gr_s, h_s, y_s, n_chunks, m_states):
        def chain(c, rev, c_r, n_r, m_st):
            d = 1 if rev else 0
            rows = pl.ds(pl.multiple_of(c * chunk, chunk), chunk)
            gc = gc_s[0, rows, :] + bias_c
            gr = gr_s[0, :, rows] + bias_r
            lic = gc[:, d:d + 1]
            lfc = _log_sigmoid(gc[:, 2 + d:3 + d])
            lir = gr[d:d + 1, :]
            lfr = _log_sigmoid(gr[2 + d:3 + d, :])
            return rows, (q_s[rows, :], k_s[rows, :], v_s[rows, :], lfr, lir, lfc, lic, c_r, n_r, m_st, rev)

        def finish(rows, h_sum):
            if y_s is not None:
                y = _head_rms(h_sum, norm_g) * jax.nn.sigmoid(o_s[rows, :].astype(F32))
                y_s[rows, :] = y.astype(y_s.dtype)

        def step(ci, m_states, phase):
            rows_f, chain_f = chain(ci, False, cf_ref, nf_ref, m_states[0])
            rows_b, chain_b = chain(n_chunks - 1 - ci, True, cb_ref, nb_ref, m_states[1])
            (h_f, h_b), m_new = _ml_chunks([chain_f, chain_b])
            if phase == 0:
                h_s[rows_f, :] = h_f
                h_s[rows_b, :] = h_b
            elif phase == 1:
                finish(rows_f, h_f + h_b)
            else:
                finish(rows_f, h_s[rows_f, :] + h_f)
                finish(rows_b, h_s[rows_b, :] + h_b)
            return tuple(m_new)

        half = n_chunks // 2
        m_states = lax.fori_loop(0, half, lambda ci, m: step(ci, m, 0), m_states)
        if n_chunks % 2:
            m_states = step(half, m_states, 1)
        return lax.fori_loop(n_chunks - half, n_chunks, lambda ci, m: step(ci, m, 2), m_states)

    for ref in (cf_ref, nf_ref, cb_ref, nb_ref):
        ref[...] = jnp.zeros_like(ref)
    m0 = jnp.full((1, 1), NEG_INIT, F32)
    m1 = run(qsc_ref, ksc_ref, vc_ref, oc_ref, gcc_ref, grc_ref, hc_ref, yc_ref, ctx_len // chunk, (m0, m0))
    run(qsl_ref, ksl_ref, vl_ref, ol_ref, gcl_ref, grl_ref, hl_ref, yl_ref, seq // chunk, m1)


def _mlstm(proj_h, gates_c, gates_r, conv_w, gate_b, norm_g, n_batch, seq, ctx_len, need_ctx):
    w = ML_DK
    cb0 = n_batch * seq // ctx_len
    col = lambda off: off // w
    lat = lambda off: pl.BlockSpec((seq, w), lambda b, h: (b, col(off) + h))
    ctx = lambda off: pl.BlockSpec((ctx_len, w), lambda b, h: (cb0 + b, col(off) + h))
    conv = conv_w.reshape(3, 2 * ML_HEADS, w).transpose(1, 0, 2)
    gb = gate_b.reshape(4, ML_HEADS).T
    in_specs = [lat(H_ML_Q), lat(H_ML_K), lat(H_ML_V), lat(H_ML_O),
                ctx(H_ML_Q), ctx(H_ML_K), ctx(H_ML_V), ctx(H_ML_O),
                pl.BlockSpec((1, 3, w), lambda b, h: (h, 0, 0)),
                pl.BlockSpec((1, 3, w), lambda b, h: (ML_HEADS + h, 0, 0)),
                pl.BlockSpec((1, seq, 4), lambda b, h: (h, b, 0)),
                pl.BlockSpec((1, 4, seq), lambda b, h: (h, 0, b)),
                pl.BlockSpec((1, ctx_len, 4), lambda b, h: (h, cb0 + b, 0)),
                pl.BlockSpec((1, 4, ctx_len), lambda b, h: (h, 0, cb0 + b)),
                pl.BlockSpec((1, 1, 4), lambda b, h: (h, 0, 0)),
                pl.BlockSpec((1, 4, 1), lambda b, h: (h, 0, 0)),
                pl.BlockSpec((1, 1, w), lambda b, h: (h, 0, 0))]
    out_specs = [pl.BlockSpec((seq, w), lambda b, h: (b, h))]
    out_shape = [jax.ShapeDtypeStruct((n_batch * seq, ML_W), BF16)]
    if need_ctx:
        out_specs.append(pl.BlockSpec((ctx_len, w), lambda b, h: (b, h)))
        out_shape.append(jax.ShapeDtypeStruct((n_batch * ctx_len, ML_W), BF16))
    outs = pl.pallas_call(
        functools.partial(_ml_kernel, seq=seq, ctx_len=ctx_len, need_ctx=need_ctx),
        grid=(n_batch, ML_HEADS),
        in_specs=in_specs,
        out_specs=out_specs,
        out_shape=out_shape,
        scratch_shapes=[pltpu.VMEM((seq, w), BF16), pltpu.VMEM((seq, w), BF16),
                        pltpu.VMEM((ctx_len, w), BF16), pltpu.VMEM((ctx_len, w), BF16),
                        pltpu.VMEM((seq, w), F32), pltpu.VMEM((ctx_len, w), F32),
                        pltpu.VMEM((w, w), F32), pltpu.VMEM((1, w), F32),
                        pltpu.VMEM((w, w), F32), pltpu.VMEM((1, w), F32)],
        compiler_params=_params(2),
        name="mlstm",
    )(proj_h, proj_h, proj_h, proj_h, proj_h, proj_h, proj_h, proj_h,
      conv, conv, gates_c, gates_r, gates_c, gates_r,
      gb.reshape(ML_HEADS, 1, 4), gb.reshape(ML_HEADS, 4, 1), norm_g.reshape(ML_HEADS, 1, w))
    return outs if need_ctx else (outs[0], None)


def _hg_level_tables(chunk, rev):
    levels = []
    n = HG_LEAF
    while 2 * n <= chunk:
        refs = []
        for start in range(0, chunk, 2 * n):
            refs.append((start + n - 1) if rev else (start + n))
        levels.append((n, refs))
        n *= 2
    return levels


def _tri_cumsum(tri, x):
    hi = x.astype(BF16)
    rest = x - hi.astype(F32)
    mid = rest.astype(BF16)
    lo = (rest - mid.astype(F32)).astype(BF16)
    out = jnp.dot(tri, hi, preferred_element_type=F32)
    out += jnp.dot(tri, mid, preferred_element_type=F32)
    return out + jnp.dot(tri, lo, preferred_element_type=F32)


def _hg_chunks(chains, loglb, log1mlb, one_m_lb):
    chunk = chains[0][1].shape[0]
    nb = chunk // HG_LEAF
    t_idx = lax.broadcasted_iota(jnp.int32, (1, HG_LEAF, 1), 1)

    qs, ks, lks, vfs, gcums = [], [], [], [], []
    for qraw, z, v, st_ref, masks, tri, rev in chains:
        e = jnp.exp(-jnp.abs(z))
        one_pe = 1.0 + e
        lsig = jnp.minimum(z, 0.0) - jnp.log(one_pe)
        a = log1mlb + lsig
        lf = jnp.maximum(loglb, a) + jnp.log(1.0 + jnp.exp(-jnp.abs(loglb - a)))
        ks.append(one_m_lb * (jnp.where(z >= 0.0, e, 1.0) / one_pe))
        lks.append((a - z) * LOG2E)
        gcums.append(_tri_cumsum(tri, lf * LOG2E))
        x = qraw.astype(F32)
        qs.append(x / (1.0 + jnp.exp(-x)))
        vfs.append(v.astype(F32))

    atts = [jnp.zeros((chunk, chunk), F32) for _ in chains]
    for lvl, (n, _) in enumerate(_hg_level_tables(chunk, False)):
        for c, (qraw, z, v, st_ref, masks, tri, rev) in enumerate(chains):
            refs = _hg_level_tables(chunk, rev)[lvl][1]
            gref = jnp.concatenate(
                [jnp.broadcast_to(gcums[c][r:r + 1, :], (2 * n, HG_DK)) for r in refs], axis=0)
            dec = jnp.exp2(-jnp.abs(gcums[c] - gref))
            p = lax.dot_general((qs[c] * dec).astype(BF16), (ks[c] * dec).astype(BF16), _NT,
                                preferred_element_type=F32)
            atts[c] += jnp.where(masks[lvl], p, 0.0)
    outs = [jnp.dot(atts[c].astype(BF16), chain[2], preferred_element_type=F32)
            for c, chain in enumerate(chains)]

    q_decs, k_decs, tot_decs = [], [], []
    for c, (qraw, z, v, st_ref, masks, tri, rev) in enumerate(chains):
        g_tot = gcums[c][0:1, :] if rev else gcums[c][chunk - 1:chunk, :]
        q_decs.append((qs[c] * jnp.exp2(gcums[c])).astype(BF16))
        k_decs.append((ks[c] * jnp.exp2(g_tot - gcums[c])).astype(BF16))
        tot_decs.append(jnp.exp2(g_tot))

    for c, (qraw, z, v, st_ref, masks, tri, rev) in enumerate(chains):
        g3 = gcums[c].reshape(nb, HG_LEAF, HG_DK)
        h3 = (lks[c] - gcums[c]).reshape(nb, HG_LEAF, HG_DK)
        q3 = qs[c].reshape(nb, HG_LEAF, HG_DK)
        v3 = vfs[c].reshape(nb, HG_LEAF, HG_DK)
        leaf = jnp.zeros((nb, HG_LEAF, HG_DK), F32)
        for s in range(HG_LEAF):
            col = jnp.sum(q3 * jnp.exp2(g3 + h3[:, s:s + 1, :]), axis=2, keepdims=True)
            valid = (t_idx <= s) if rev else (t_idx >= s)
            leaf += jnp.where(valid, col, 0.0) * v3[:, s:s + 1, :]
        outs[c] += leaf.reshape(chunk, HG_DK)

    for c, (qraw, z, v, st_ref, masks, tri, rev) in enumerate(chains):
        st = st_ref[...]
        outs[c] += lax.dot_general(q_decs[c], st.astype(BF16), _NT, preferred_element_type=F32)
        st_ref[...] = st * tot_decs[c] + lax.dot_general(v, k_decs[c], _TN, preferred_element_type=F32)
    return outs


def _hg_kernel(*refs, seq, ctx_len, need_ctx):
    (ql_ref, il_ref, gl_ref, ffl_ref, fbl_ref, qc_ref, ic_ref, gc_ref, ffc_ref, fbc_ref,
     llb_ref, l1m_ref, oml_ref, ng_ref) = refs[:14]
    if need_ctx:
        yl_ref, yc_ref = refs[14:16]
        scratch = refs[16:]
    else:
        yl_ref = refs[14]
        yc_ref = None
        scratch = refs[15:]
    ol_ref, oc_ref, stf_ref, stb_ref = scratch
    chunk = HG_CHUNK
    loglb = llb_ref[0]
    log1mlb = l1m_ref[0]
    one_m_lb = oml_ref[0]
    norm_g = ng_ref[0]
    ti = lax.broadcasted_iota(jnp.int32, (chunk, chunk), 0)
    si = lax.broadcasted_iota(jnp.int32, (chunk, chunk), 1)

    def direction_consts(rev):
        tri = jnp.where((si >= ti) if rev else (si <= ti), 1.0, 0.0).astype(BF16)
        masks = []
        for n, _ in _hg_level_tables(chunk, rev):
            shift = (2 * n).bit_length() - 1
            same = jnp.right_shift(ti, shift) == jnp.right_shift(si, shift)
            t_late = jnp.bitwise_and(ti, 2 * n - 1) >= n
            s_late = jnp.bitwise_and(si, 2 * n - 1) >= n
            if rev:
                masks.append(same & jnp.logical_not(t_late) & s_late)
            else:
                masks.append(same & t_late & jnp.logical_not(s_late))
        return tri, masks

    tri_f, masks_f = direction_consts(False)
    tri_b, masks_b = direction_consts(True)

    def run(q_s, i_s, g_s, zf_s, zb_s, o_s, y_s, n_chunks):
        def step(pi, second):
            rows = []
            chains = []
            for u in range(HG_UNROLL):
                c = pi * HG_UNROLL + u
                r_f = pl.ds(pl.multiple_of(c * chunk, chunk), chunk)
                rows.append(r_f)
                chains.append((q_s[r_f, :], zf_s[r_f, :], i_s[r_f, :], stf_ref, masks_f, tri_f, False))
            for u in range(HG_UNROLL):
                c = n_chunks - 1 - (pi * HG_UNROLL + u)
                r_b = pl.ds(pl.multiple_of(c * chunk, chunk), chunk)
                rows.append(r_b)
                chains.append((q_s[r_b, :], zb_s[r_b, :], i_s[r_b, :], stb_ref, masks_b, tri_b, True))
            outs = _hg_chunks(chains, loglb, log1mlb, one_m_lb)
            for r, o in zip(rows, outs):
                if not second:
                    o_s[r, :] = o
                elif y_s is not None:
                    y = _head_rms(o_s[r, :] + o, norm_g) * _silu(g_s[r, :].astype(F32))
                    y_s[r, :] = y.astype(y_s.dtype)

        def first(pi, carry):
            step(pi, False)
            return carry

        def second(pi, carry):
            step(pi, True)
            return carry

        n_steps = n_chunks // HG_UNROLL
        lax.fori_loop(0, n_steps // 2, first, 0)
        lax.fori_loop(n_steps // 2, n_steps, second, 0)

    stf_ref[...] = jnp.zeros_like(stf_ref)
    stb_ref[...] = jnp.zeros_like(stb_ref)
    run(qc_ref, ic_ref, gc_ref, ffc_ref, fbc_ref, oc_ref, yc_ref, ctx_len // chunk)
    run(ql_ref, il_ref, gl_ref, ffl_ref, fbl_ref, ol_ref, yl_ref, seq // chunk)


def _hgrn2(proj_h, proj_f, lb, norm_g, n_batch, seq, ctx_len, need_ctx):
    w = HG_DK
    cb0 = n_batch * seq // ctx_len
    lat = lambda off: pl.BlockSpec((seq, w), lambda b, h: (b, off // w + h))
    ctx = lambda off: pl.BlockSpec((ctx_len, w), lambda b, h: (cb0 + b, off // w + h))
    vec = pl.BlockSpec((1, 1, w), lambda b, h: (h, 0, 0))
    lb3 = lb.reshape(HG_HEADS, 1, w)
    in_specs = [lat(H_HG_Q), lat(H_HG_I), lat(H_HG_G), lat(F_HG_FF), lat(F_HG_FB),
                ctx(H_HG_Q), ctx(H_HG_I), ctx(H_HG_G), ctx(F_HG_FF), ctx(F_HG_FB),
                vec, vec, vec, vec]
    out_specs = [pl.BlockSpec((seq, w), lambda b, h: (b, h))]
    out_shape = [jax.ShapeDtypeStruct((n_batch * seq, HG_W), BF16)]
    if need_ctx:
        out_specs.append(pl.BlockSpec((ctx_len, w), lambda b, h: (b, h)))
        out_shape.append(jax.ShapeDtypeStruct((n_batch * ctx_len, HG_W), BF16))
    outs = pl.pallas_call(
        functools.partial(_hg_kernel, seq=seq, ctx_len=ctx_len, need_ctx=need_ctx),
        grid=(n_batch, HG_HEADS),
        in_specs=in_specs,
        out_specs=out_specs,
        out_shape=out_shape,
        scratch_shapes=[pltpu.VMEM((seq, w), F32), pltpu.VMEM((ctx_len, w), F32),
                        pltpu.VMEM((w, w), F32), pltpu.VMEM((w, w), F32)],
        compiler_params=_params(2),
        name="hgrn2",
    )(proj_h, proj_h, proj_h, proj_f, proj_f, proj_h, proj_h, proj_h, proj_f, proj_f,
      jnp.log(lb3), jnp.log1p(-lb3), 1.0 - lb3, norm_g.reshape(HG_HEADS, 1, w))
    return outs if need_ctx else (outs[0], None)


def _router_kernel(x_ref, g_ref, sh_ref, sc_ref, wr_ref, h_ref, idx_ref, wt_ref, cnt_ref, carry_ref):
    @pl.when(pl.program_id(0) == 0)
    def _():
        carry_ref[...] = jnp.zeros_like(carry_ref)

    x = x_ref[...]
    y = x * lax.rsqrt(jnp.mean(x * x, axis=-1, keepdims=True) + EPS) * g_ref[...]
    h = y * (1.0 + sc_ref[0]) + sh_ref[0]
    h_ref[...] = h.astype(h_ref.dtype)
    logits = jnp.dot(h, wr_ref[...], precision=lax.Precision.HIGHEST, preferred_element_type=F32)
    lane = lax.broadcasted_iota(jnp.int32, logits.shape, 1).astype(F32)
    lg = jnp.where(lane < N_EXPERTS, logits, -jnp.inf)
    m1 = jnp.max(lg, axis=-1, keepdims=True)
    i1 = jnp.min(jnp.where(lg == m1, lane, float(LANE)), axis=-1, keepdims=True)
    lg2 = jnp.where(lane == i1, -jnp.inf, lg)
    m2 = jnp.max(lg2, axis=-1, keepdims=True)
    i2 = jnp.min(jnp.where(lg2 == m2, lane, float(LANE)), axis=-1, keepdims=True)
    e = jnp.exp(m2 - m1)
    w1 = 1.0 / (1.0 + e)
    w2 = e / (1.0 + e)
    wt_ref[...] = jnp.where(lane == 0, w1, jnp.where(lane == 1, w2, 0.0))
    n_tok = x.shape[0]
    routed = jnp.where(jnp.logical_or(lane == i1, lane == i2), 1.0, 0.0)
    ti = lax.broadcasted_iota(jnp.int32, (n_tok, n_tok), 0)
    ui = lax.broadcasted_iota(jnp.int32, (n_tok, n_tok), 1)
    earlier = jnp.where(ui < ti, 1.0, 0.0).astype(BF16)
    before = carry_ref[...] + jnp.dot(earlier, routed.astype(BF16), preferred_element_type=F32)
    r1 = jnp.sum(jnp.where(lane == i1, before, 0.0), axis=-1, keepdims=True)
    r2 = jnp.sum(jnp.where(lane == i2, before, 0.0), axis=-1, keepdims=True)
    carry_ref[...] += jnp.sum(routed, axis=0, keepdims=True)
    cnt_ref[...] = jnp.broadcast_to(carry_ref[...], cnt_ref.shape).astype(jnp.int32)
    idx_ref[...] = jnp.where(lane == 0, i1, jnp.where(lane == 1, i2, jnp.where(
        lane == 2, r1, jnp.where(lane == 3, r2, 0.0)))).astype(jnp.int32)


def _router(x, g, shift, scale, w_router, n_rows, seq, n_batch):
    d = x.shape[1]
    tm = 256
    tpb = seq // tm
    grp = lambda i: (_group_of(i, tpb, n_batch), 0, 0)
    wr = jnp.pad(w_router, ((0, 0), (0, LANE - N_EXPERTS)))
    return pl.pallas_call(
        _router_kernel,
        grid=(n_rows // tm,),
        in_specs=[pl.BlockSpec((tm, d), lambda i: (i, 0)),
                  pl.BlockSpec((1, d), lambda i: (0, 0)),
                  pl.BlockSpec((1, 1, d), grp),
                  pl.BlockSpec((1, 1, d), grp),
                  pl.BlockSpec((d, LANE), lambda i: (0, 0))],
        out_specs=[pl.BlockSpec((tm, d), lambda i: (i, 0)),
                   pl.BlockSpec((tm, LANE), lambda i: (i, 0)),
                   pl.BlockSpec((tm, LANE), lambda i: (i, 0)),
                   pl.BlockSpec((8, LANE), lambda i: (0, 0))],
        out_shape=[jax.ShapeDtypeStruct((n_rows, d), F32),
                   jax.ShapeDtypeStruct((n_rows, LANE), jnp.int32),
                   jax.ShapeDtypeStruct((n_rows, LANE), F32),
                   jax.ShapeDtypeStruct((8, LANE), jnp.int32)],
        scratch_shapes=[pltpu.VMEM((1, LANE), F32)],
        compiler_params=_params(1),
        name="router",
    )(x, g.reshape(1, d), shift, scale, wr)


def _row_copy(src_hbm, buf_ref, sem, src_row, r):
    return pltpu.make_async_copy(src_hbm.at[pl.ds(src_row, 1)], buf_ref.at[pl.ds(r, 1)], sem)


def _dispatch_kernel(src_ref, nu_ref, h_hbm, o_ref, buf_ref, sem):
    t = pl.program_id(0)
    tm = buf_ref.shape[1]
    slot = t % 2

    def tile_copies(tile, slot, fn):
        def body(r, carry):
            fn(_row_copy(h_hbm, buf_ref.at[slot], sem.at[slot], src_ref[tile * tm + r], r))
            return carry
        lax.fori_loop(0, tm, body, 0, unroll=8)

    @pl.when(jnp.logical_and(t == 0, nu_ref[0] > 0))
    def _():
        tile_copies(0, 0, lambda cp: cp.start())

    @pl.when(t + 1 < nu_ref[0])
    def _():
        tile_copies(t + 1, 1 - slot, lambda cp: cp.start())

    @pl.when(t < nu_ref[0])
    def _():
        tile_copies(t, slot, lambda cp: cp.wait())
        o_ref[...] = buf_ref[slot].astype(o_ref.dtype)

    @pl.when(t >= nu_ref[0])
    def _():
        o_ref[...] = jnp.zeros_like(o_ref)


def _dispatch(h, src_token, n_used):
    d = h.shape[1]
    tm = MOE_TILE
    n_pad = src_token.shape[0]
    return pl.pallas_call(
        _dispatch_kernel,
        grid_spec=pltpu.PrefetchScalarGridSpec(
            num_scalar_prefetch=2,
            grid=(n_pad // tm,),
            in_specs=[pl.BlockSpec(memory_space=pl.ANY)],
            out_specs=pl.BlockSpec((tm, d), lambda t, src, nu: (t, 0)),
            scratch_shapes=[pltpu.VMEM((2, tm, d), F32), pltpu.SemaphoreType.DMA((2,))]),
        out_shape=jax.ShapeDtypeStruct((n_pad, d), BF16),
        compiler_params=_params(1),
        name="moe_dispatch",
    )(src_token, n_used, h)


def _new_expert(te_ref, i):
    return jnp.logical_or(i == 0, te_ref[i] != te_ref[jnp.maximum(i - 1, 0)])


def _moe_kernel(te_ref, nu_ref, nxt_ref, valid_ref, a_ref, *refs, n_w, glu):
    w_hbm = refs[:n_w]
    o_ref = refs[n_w]
    wbuf, wbf, sem, slot_ref = refs[n_w + 1:]
    j, i = pl.program_id(0), pl.program_id(1)
    n_j, n_i = pl.num_programs(0), pl.num_programs(1)
    tn = o_ref.shape[1]

    def weight_copies(slot, expert, col_tile):
        cols = pl.ds(pl.multiple_of(col_tile * tn, tn), tn)
        return [pltpu.make_async_copy(w_hbm[x].at[expert, :, cols], wbuf.at[slot, x], sem.at[slot, x])
                for x in range(n_w)]

    @pl.when(jnp.logical_and(j == 0, i == 0))
    def _():
        slot_ref[0] = 0
        for cp in weight_copies(0, te_ref[0], 0):
            cp.start()

    @pl.when(_new_expert(te_ref, i))
    def _():
        slot = slot_ref[0]
        for cp in weight_copies(slot, te_ref[i], j):
            cp.wait()
        nxt = nxt_ref[i]
        same_col = nxt < n_i

        @pl.when(jnp.logical_or(same_col, j + 1 < n_j))
        def _():
            nxt_expert = te_ref[jnp.where(same_col, nxt, 0)]
            for cp in weight_copies(1 - slot, nxt_expert, jnp.where(same_col, j, j + 1)):
                cp.start()

        for x in range(n_w):
            wbf[x] = wbuf[slot, x].astype(BF16)
        slot_ref[0] = 1 - slot

    def compute(rows):
        a = a_ref[rows, :]
        if glu:
            g = jnp.dot(a, wbf[0], preferred_element_type=F32)
            u = jnp.dot(a, wbf[1], preferred_element_type=F32)
            o_ref[rows, :] = (_silu(g) * u).astype(o_ref.dtype)
        else:
            o_ref[rows, :] = jnp.dot(a, wbf[0], preferred_element_type=F32).astype(o_ref.dtype)

    tm = o_ref.shape[0]
    valid = valid_ref[i]

    @pl.when(valid == tm)
    def _():
        compute(slice(None))

    @pl.when(valid < tm)
    def _():
        for r in range(tm // MOE_SUB_ROWS):
            rows = slice(r * MOE_SUB_ROWS, (r + 1) * MOE_SUB_ROWS)

            @pl.when(r * MOE_SUB_ROWS < valid)
            def _():
                compute(rows)

            @pl.when(r * MOE_SUB_ROWS >= valid)
            def _():
                o_ref[rows, :] = jnp.zeros((MOE_SUB_ROWS, o_ref.shape[1]), o_ref.dtype)


def _moe_grouped(a, weights, tile_expert, n_used, next_run, tile_valid, glu, out_dtype, name):
    n_rows, k = a.shape
    n = weights[0].shape[2]
    n_w = len(weights)
    tm, tn = MOE_TILE, COL_TILE
    a_map = lambda j, i, te, nu, nxt, tv: (jnp.maximum(jnp.minimum(i, nu[0] - 1), 0), 0)
    return pl.pallas_call(
        functools.partial(_moe_kernel, n_w=n_w, glu=glu),
        grid_spec=pltpu.PrefetchScalarGridSpec(
            num_scalar_prefetch=4,
            grid=(n // tn, n_rows // tm),
            in_specs=[pl.BlockSpec((tm, k), a_map)] + [pl.BlockSpec(memory_space=pl.ANY)] * n_w,
            out_specs=pl.BlockSpec((tm, tn), lambda j, i, te, nu, nxt, tv: (i, j)),
            scratch_shapes=[pltpu.VMEM((2, n_w, k, tn), F32), pltpu.VMEM((n_w, k, tn), BF16),
                            pltpu.SemaphoreType.DMA((2, n_w)), pltpu.SMEM((1,), jnp.int32)]),
        out_shape=jax.ShapeDtypeStruct((n_rows, n), out_dtype),
        compiler_params=_params(2),
        name=name,
    )(tile_expert, n_used, next_run, tile_valid, a, *weights)


def _combine_kernel(x_ref, ya_ref, yb_ref, wt_ref, g_ref, o_ref):
    wt = wt_ref[...]
    y = wt[:, 0:1] * ya_ref[...].astype(F32) + wt[:, 1:2] * yb_ref[...].astype(F32)
    o_ref[...] = x_ref[...] + g_ref[0] * y


def _moe_combine(x, ya, yb, wt, gate, n_rows, seq, n_batch):
    d = x.shape[1]
    tm = 256
    tpb = seq // tm
    row = pl.BlockSpec((tm, d), lambda i: (i, 0))
    return pl.pallas_call(
        _combine_kernel,
        grid=(n_rows // tm,),
        in_specs=[row, row, row,
                  pl.BlockSpec((tm, LANE), lambda i: (i, 0)),
                  pl.BlockSpec((1, 1, d), lambda i: (_group_of(i, tpb, n_batch), 0, 0))],
        out_specs=row,
        out_shape=jax.ShapeDtypeStruct((n_rows, d), F32),
        compiler_params=_params(1),
        name="moe_combine",
    )(x, ya, yb, wt, gate)


def _moe_ffn(x, g, shift, scale, gate, w_router, wg, wu, wd, n_rows, seq, n_batch):
    h, idx, wt, cnt = _router(x, g, shift, scale, w_router, n_rows, seq, n_batch)
    tm = MOE_TILE
    n_tiles = 2 * n_rows // tm + N_EXPERTS
    counts = cnt[0, :N_EXPERTS]
    tiles = (counts + tm - 1) // tm
    tile_end = jnp.cumsum(tiles)
    tile_start = tile_end - tiles
    expert = jnp.concatenate([idx[:, 0], idx[:, 1]])
    rank = jnp.concatenate([idx[:, 2], idx[:, 3]])
    token = jnp.concatenate([jnp.arange(n_rows, dtype=jnp.int32)] * 2)
    one_hot = (expert[:, None] == jnp.arange(N_EXPERTS, dtype=jnp.int32)[None, :]).astype(jnp.int32)
    dest = jnp.sum(tile_start[None, :] * one_hot, axis=1) * tm + rank
    src_token = jnp.zeros((n_tiles * tm,), jnp.int32).at[dest].set(token)
    n_used = tile_end[-1:].astype(jnp.int32)
    tile_id = jnp.minimum(jnp.arange(n_tiles, dtype=jnp.int32), n_used[0] - 1)
    tile_expert = jnp.sum((tile_end[None, :] <= tile_id[:, None]).astype(jnp.int32), axis=1)

    a_sorted = _dispatch(h, src_token, n_used)
    later = jnp.arange(n_tiles, dtype=jnp.int32)
    other = jnp.logical_and(later[None, :] > later[:, None], tile_expert[None, :] != tile_expert[:, None])
    next_run = jnp.min(jnp.where(other, later[None, :], n_tiles), axis=1).astype(jnp.int32)
    sel = (tile_expert[:, None] == jnp.arange(N_EXPERTS, dtype=jnp.int32)[None, :]).astype(jnp.int32)
    rows_left = jnp.sum(sel * counts[None, :], axis=1) - (later - jnp.sum(sel * tile_start[None, :], axis=1)) * tm
    tile_valid = jnp.where(later < n_used[0], jnp.clip(rows_left, 0, tm), 0).astype(jnp.int32)
    u = _moe_grouped(a_sorted, (wg, wu), tile_expert, n_used, next_run, tile_valid, True, BF16, "moe_up")
    y = _moe_grouped(u, (wd,), tile_expert, n_used, next_run, tile_valid, False, BF16, "moe_down")
    ya = jnp.take(y, dest[:n_rows], axis=0, mode="clip")
    yb = jnp.take(y, dest[n_rows:], axis=0, mode="clip")
    return _moe_combine(x, ya, yb, wt, gate, n_rows, seq, n_batch)


def _rope_tables(n_batch, seq, n_ctx_rows):
    t = jnp.arange(seq)
    pos = jnp.stack([t // GRID_W, t % GRID_W], axis=-1).astype(F32)
    inv_freq = jnp.exp(-jnp.log(ROPE_BASE) * jnp.arange(0, ROPE_AXIS_DIM, 2, dtype=F32) / ROPE_AXIS_DIM)
    ang = pos[..., None] * inv_freq
    cos, sin = jnp.cos(ang), jnp.sin(ang)
    cos = jnp.concatenate([cos[:, 0], cos[:, 0], cos[:, 1], cos[:, 1]], axis=-1)
    sin = jnp.concatenate([-sin[:, 0], sin[:, 0], -sin[:, 1], sin[:, 1]], axis=-1)
    cos = jnp.concatenate([jnp.tile(cos, (n_batch, 1)), jnp.ones((n_ctx_rows, HEAD_DIM), F32)], axis=0)
    sin = jnp.concatenate([jnp.tile(sin, (n_batch, 1)), jnp.zeros((n_ctx_rows, HEAD_DIM), F32)], axis=0)
    return cos, sin


def kernel(x, c, ctx, c_ctx, w_mod, b_mod, norm1_g, norm2_g, w_in, attn_q_norm_g, attn_k_norm_g, ml_conv_w,
           ml_gate_b, ml_norm_g, hg_lb_logits, hg_norm_g, w_branch, w_out, ffn_w_gate, ffn_w_up, ffn_w_down,
           moe_w_router, moe_w_gate, moe_w_up, moe_w_down):
    n_batch, seq, d = x.shape
    ctx_len = ctx.shape[1]
    depth = w_mod.shape[0]
    m_lat = n_batch * seq
    m_ctx = n_batch * ctx_len
    m_all = m_lat + m_ctx
    assert d == D_MODEL and w_in.shape[2] == _N_IN
    assert seq % ROW_TILE == 0 and m_ctx % ROW_TILE == 0 and seq % ctx_len == 0
    assert seq % ML_CHUNK == 0 and ctx_len % ML_CHUNK == 0 and ctx_len % HG_CHUNK == 0

    xa = jnp.concatenate([x.reshape(m_lat, d), ctx.reshape(m_ctx, d)], axis=0)
    cs = jnp.concatenate([c, c_ctx[None], jnp.zeros((8 - n_batch - 1, d), F32)], axis=0)
    mod = _modulation(cs, w_mod, b_mod)
    cos, sin = _rope_tables(n_batch, seq, m_ctx)
    lb_all = jnp.cumsum(jax.nn.softmax(hg_lb_logits.astype(F32), axis=0), axis=0)
    lb_all = lb_all - lb_all[:1]
    w_in_t = jnp.swapaxes(w_in, 1, 2)

    for l in range(depth):
        need_ctx = l < depth - 1
        n_rows = m_all if need_ctx else m_lat
        mods = [mod[l, :n_batch + 1, i * d:(i + 1) * d].reshape(n_batch + 1, 1, d) for i in range(N_MOD)]

        h = _norm_mod(xa, norm1_g[l], mods[0], mods[1], m_all, seq, n_batch)
        proj_h = _proj_in(h, w_in_t, l, *H_TILES, BF16)
        proj_f = _proj_in(h, w_in_t, l, *F_TILES, F32)

        q_rot, k_rot = _att_prep(proj_h, cos, sin, attn_q_norm_g[l], attn_k_norm_g[l])
        ya = _attention(q_rot, k_rot, proj_h, n_batch, seq, ctx_len, True)

        gates = proj_f[:, F_ML_G:F_ML_G + 4 * ML_HEADS].reshape(m_all, 4, ML_HEADS)
        gates_c = gates.transpose(2, 0, 1)
        gates_r = gates.transpose(2, 1, 0)
        ym, ym_c = _mlstm(proj_h, gates_c, gates_r, ml_conv_w[l], ml_gate_b[l], ml_norm_g[l],
                          n_batch, seq, ctx_len, need_ctx)
        yh, yh_c = _hgrn2(proj_h, proj_f, lb_all[l], hg_norm_g[l], n_batch, seq, ctx_len, need_ctx)
        ys_ctx = None
        if need_ctx:
            ya_c = _attention(q_rot, k_rot, proj_h, n_batch, seq, ctx_len, False)
            ys_ctx = (ya_c, ym_c, yh_c)

        merged = _branch_merge((ya, ym, yh), ys_ctx, w_branch, l, proj_h, n_rows)
        xa = _matmul_residual(merged, w_out, l, xa, mods[2], n_rows, seq, n_batch, "proj_out")

        if l % 2 == 0:
            h2 = _norm_mod(xa, norm2_g[l], mods[3], mods[4], n_rows, seq, n_batch)
            u = _glu(h2, ffn_w_gate, ffn_w_up, l // 2)
            xa = _matmul_residual(u, ffn_w_down, l // 2, xa, mods[5], n_rows, seq, n_batch, "ffn_down")
        else:
            xa = _moe_ffn(xa, norm2_g[l], mods[3], mods[4], mods[5], moe_w_router[l // 2],
                          moe_w_gate[l // 2], moe_w_up[l // 2], moe_w_down[l // 2], n_rows, seq, n_batch)
    return xa[:m_lat].reshape(n_batch, seq, d)
```

```python
import functools

import jax
import jax.numpy as jnp
from jax import lax
from jax.experimental import pallas as pl
from jax.experimental.pallas import tpu as pltpu

F32 = jnp.float32
BF16 = jnp.bfloat16

D_MODEL = 2048
GRID_W = 64
N_MOD = 6
ATT_HEADS = 8
ATT_KV_HEADS = 2
HEAD_DIM = 128
ATT_W = ATT_HEADS * HEAD_DIM
KV_W = ATT_KV_HEADS * HEAD_DIM
ROPE_AXIS_DIM = HEAD_DIM // 2
ROPE_BASE = 10000.0
ML_HEADS = 4
ML_DK = 256
ML_W = ML_HEADS * ML_DK
HG_HEADS = 8
HG_DK = 128
HG_W = HG_HEADS * HG_DK
D_FF = 5632
N_EXPERTS = 8
EPS = 1e-6
NEG_INIT = -1e30
LOG2E = 1.4426950408889634

_O_ATT_Q = 0
_O_ML_GATES = ATT_W + 2 * KV_W + 4 * ML_W
_O_HG_Q = _O_ML_GATES + 4 * ML_HEADS
_O_HG_FF = _O_HG_Q + HG_W
_O_HG_I = _O_HG_FF + 2 * HG_W
_O_MERGE = _O_HG_I + 2 * HG_W
_N_IN = _O_MERGE + 3 * D_MODEL

H_ATT_Q = 0
H_ATT_K = 1024
H_ATT_V = 1280
H_ML_Q = 1536
H_ML_K = 2560
H_ML_V = 3584
H_ML_O = 4608
H_HG_Q = 5632
H_HG_I = 6656
H_HG_G = 7680
H_MERGE = 8704
H_WIDTH = 14848
F_HG_FF = 0
F_HG_FB = 1024
F_ML_G = 2048

LANE = 128
ROW_TILE = 512
COL_TILE = 512
PROJ_ROW_TILES = 4
W_MISALIGN = _O_HG_Q % LANE


def _source_tiles(segments):
    blocks, shifted = [], []
    for off, width in segments:
        for start in range(off, off + width, COL_TILE):
            assert start % COL_TILE in (0, W_MISALIGN)
            blocks.append(start // COL_TILE)
            shifted.append(int(start % COL_TILE != 0))
    return blocks, shifted


H_TILES = _source_tiles([(0, _O_ML_GATES), (_O_HG_Q, HG_W), (_O_HG_I, 2 * HG_W + 3 * D_MODEL)])
F_TILES = _source_tiles([(_O_HG_FF, 2 * HG_W), (_O_ML_GATES, COL_TILE)])
VMEM_LIMIT = 56 * 1024 * 1024

ML_CHUNK = 256
HG_CHUNK = 64
HG_LEAF = 8
HG_UNROLL = 2
ATT_TQ = 512
MOE_TILE = 512
MOE_SUB_ROWS = 128

_NT = (((1,), (1,)), ((), ()))
_TN = (((0,), (0,)), ((), ()))


def _params(n_axes):
    return pltpu.CompilerParams(dimension_semantics=("arbitrary",) * n_axes,
                                vmem_limit_bytes=VMEM_LIMIT)


def _silu(x):
    return x * jax.nn.sigmoid(x)


def _log_sigmoid(x):
    return jnp.minimum(x, 0.0) - jnp.log1p(jnp.exp(-jnp.abs(x)))


def _mod_kernel(c_ref, w_ref, b_ref, o_ref):
    a = _silu(c_ref[...])
    o_ref[...] = jnp.dot(a.astype(BF16), w_ref[...].astype(BF16), preferred_element_type=F32) + b_ref[...]


def _modulation(cs, w_mod, b_mod):
    depth, d, n = w_mod.shape
    tn = 1024
    return pl.pallas_call(
        _mod_kernel,
        grid=(depth, n // tn),
        in_specs=[pl.BlockSpec((8, d), lambda l, j: (0, 0)),
                  pl.BlockSpec((None, d, tn), lambda l, j: (l, 0, j)),
                  pl.BlockSpec((None, 1, tn), lambda l, j: (l, 0, j))],
        out_specs=pl.BlockSpec((None, 8, tn), lambda l, j: (l, 0, j)),
        out_shape=jax.ShapeDtypeStruct((depth, 8, n), F32),
        compiler_params=_params(2),
        name="modulation",
    )(cs, w_mod, b_mod.reshape(depth, 1, n))


def _norm_kernel(x_ref, g_ref, sh_ref, sc_ref, o_ref):
    x = x_ref[...]
    y = x * lax.rsqrt(jnp.mean(x * x, axis=-1, keepdims=True) + EPS)
    y = y * g_ref[...]
    o_ref[...] = (y * (1.0 + sc_ref[0]) + sh_ref[0]).astype(o_ref.dtype)


def _group_of(i, tiles_per_batch, n_batch):
    return jnp.minimum(i // tiles_per_batch, n_batch)


def _norm_mod(x, g, shift, scale, n_rows, seq, n_batch):
    d = x.shape[1]
    tm = 256
    tpb = seq // tm
    grp = lambda i: (_group_of(i, tpb, n_batch), 0, 0)
    return pl.pallas_call(
        _norm_kernel,
        grid=(n_rows // tm,),
        in_specs=[pl.BlockSpec((tm, d), lambda i: (i, 0)),
                  pl.BlockSpec((1, d), lambda i: (0, 0)),
                  pl.BlockSpec((1, 1, d), grp),
                  pl.BlockSpec((1, 1, d), grp)],
        out_specs=pl.BlockSpec((tm, d), lambda i: (i, 0)),
        out_shape=jax.ShapeDtypeStruct((n_rows, d), BF16),
        compiler_params=_params(1),
        name="norm_mod",
    )(x, g.reshape(1, d), shift, scale)


def _proj_in_kernel(blk_ref, shift_ref, a_ref, wa_ref, wb_ref, o_ref, w_scr):
    j = pl.program_id(0)

    @pl.when(pl.program_id(1) == 0)
    def _():
        @pl.when(shift_ref[j] == 0)
        def _():
            w_scr[...] = wa_ref[...].T.astype(BF16)

        @pl.when(shift_ref[j] != 0)
        def _():
            w = jnp.concatenate([wa_ref[W_MISALIGN:, :], wb_ref[...]], axis=0)
            w_scr[...] = w.T.astype(BF16)

    o_ref[...] = jnp.dot(a_ref[...], w_scr[...], preferred_element_type=F32).astype(o_ref.dtype)


def _proj_in(a, w_in_t, layer, src_blocks, shifted, out_dtype):
    m, k = a.shape
    n_src = w_in_t.shape[1]
    tm, tn = m // PROJ_ROW_TILES, COL_TILE
    n_tiles = len(src_blocks)
    last_blk = n_src // W_MISALIGN - 1
    per = tn // W_MISALIGN
    return pl.pallas_call(
        _proj_in_kernel,
        grid_spec=pltpu.PrefetchScalarGridSpec(
            num_scalar_prefetch=2,
            grid=(n_tiles, m // tm),
            in_specs=[pl.BlockSpec((tm, k), lambda j, i, blk, sh: (i, 0)),
                      pl.BlockSpec((None, tn, k), lambda j, i, blk, sh: (layer, blk[j], 0)),
                      pl.BlockSpec((None, W_MISALIGN, k), lambda j, i, blk, sh: (
                          layer, jnp.minimum(per * (blk[j] + 1), last_blk), 0))],
            out_specs=pl.BlockSpec((tm, tn), lambda j, i, blk, sh: (i, j)),
            scratch_shapes=[pltpu.VMEM((k, tn), BF16)]),
        out_shape=jax.ShapeDtypeStruct((m, n_tiles * tn), out_dtype),
        compiler_params=_params(2),
        name="proj_in",
    )(jnp.asarray(src_blocks, jnp.int32), jnp.asarray(shifted, jnp.int32), a, w_in_t, w_in_t)


def _mm_res_kernel(a_ref, w_ref, r_ref, g_ref, o_ref, wb_ref):
    @pl.when(pl.program_id(1) == 0)
    def _():
        wb_ref[...] = w_ref[...].astype(BF16)

    y = jnp.dot(a_ref[...], wb_ref[...], preferred_element_type=F32)
    o_ref[...] = r_ref[...] + g_ref[0] * y


def _matmul_residual(a, w, layer, res, gate, n_rows, seq, n_batch, name):
    k = a.shape[1]
    n = w.shape[2]
    tm, tn = ROW_TILE, COL_TILE
    tpb = seq // tm
    return pl.pallas_call(
        _mm_res_kernel,
        grid=(n // tn, n_rows // tm),
        in_specs=[pl.BlockSpec((tm, k), lambda j, i: (i, 0)),
                  pl.BlockSpec((None, k, tn), lambda j, i: (layer, 0, j)),
                  pl.BlockSpec((tm, tn), lambda j, i: (i, j)),
                  pl.BlockSpec((1, 1, tn), lambda j, i: (_group_of(i, tpb, n_batch), 0, j))],
        out_specs=pl.BlockSpec((tm, tn), lambda j, i: (i, j)),
        out_shape=jax.ShapeDtypeStruct((n_rows, n), F32),
        scratch_shapes=[pltpu.VMEM((k, tn), BF16)],
        compiler_params=_params(2),
        name=name,
    )(a, w, res, gate)


def _glu_kernel(a_ref, wg_ref, wu_ref, o_ref, wgb_ref, wub_ref):
    @pl.when(pl.program_id(1) == 0)
    def _():
        wgb_ref[...] = wg_ref[...].astype(BF16)
        wub_ref[...] = wu_ref[...].astype(BF16)

    a = a_ref[...]
    g = jnp.dot(a, wgb_ref[...], preferred_element_type=F32)
    u = jnp.dot(a, wub_ref[...], preferred_element_type=F32)
    o_ref[...] = (_silu(g) * u).astype(o_ref.dtype)


def _glu(a, wg, wu, layer):
    m, k = a.shape
    n = wg.shape[2]
    tm, tn = ROW_TILE, COL_TILE
    return pl.pallas_call(
        _glu_kernel,
        grid=(n // tn, m // tm),
        in_specs=[pl.BlockSpec((tm, k), lambda j, i: (i, 0)),
                  pl.BlockSpec((None, k, tn), lambda j, i: (layer, 0, j)),
                  pl.BlockSpec((None, k, tn), lambda j, i: (layer, 0, j))],
        out_specs=pl.BlockSpec((tm, tn), lambda j, i: (i, j)),
        out_shape=jax.ShapeDtypeStruct((m, n), BF16),
        scratch_shapes=[pltpu.VMEM((k, tn), BF16), pltpu.VMEM((k, tn), BF16)],
        compiler_params=_params(2),
        name="ffn_up",
    )(a, wg, wu)


def _merge_kernel(*refs, n_lat_tiles, with_ctx):
    if with_ctx:
        lat_refs, ctx_refs, rest = refs[0:3], refs[3:6], refs[6:]
    else:
        lat_refs, ctx_refs, rest = refs[0:3], None, refs[3:]
    w_ref, g0_ref, g1_ref, g2_ref, o_ref, wb_ref = rest

    @pl.when(pl.program_id(1) == 0)
    def _():
        wb_ref[...] = w_ref[...].astype(BF16)

    is_ctx = pl.program_id(1) >= n_lat_tiles
    acc = None
    for b, g_ref in enumerate((g0_ref, g1_ref, g2_ref)):
        y = lat_refs[b][...]
        if with_ctx:
            y = jnp.where(is_ctx, ctx_refs[b][...], y)
        term = jax.nn.sigmoid(g_ref[...].astype(F32)) * jnp.dot(y, wb_ref[b], preferred_element_type=F32)
        acc = term if acc is None else acc + term
    o_ref[...] = acc.astype(o_ref.dtype)


def _branch_merge(ys_lat, ys_ctx, w_branch, layer, proj_h, n_rows):
    k = ys_lat[0].shape[1]
    n = w_branch.shape[3]
    tm, tn = ROW_TILE, COL_TILE
    n_lat_tiles = ys_lat[0].shape[0] // tm
    with_ctx = ys_ctx is not None
    gate_spec = lambda b: pl.BlockSpec((tm, tn), lambda j, i: (i, (H_MERGE + b * n) // tn + j))
    lat_spec = pl.BlockSpec((tm, k), lambda j, i: (jnp.minimum(i, n_lat_tiles - 1), 0))
    ctx_spec = pl.BlockSpec((tm, k), lambda j, i: (jnp.maximum(i - n_lat_tiles, 0), 0))
    in_specs = [lat_spec] * 3 + ([ctx_spec] * 3 if with_ctx else [])
    in_specs += [pl.BlockSpec((None, 3, k, tn), lambda j, i: (layer, 0, 0, j)),
                 gate_spec(0), gate_spec(1), gate_spec(2)]
    args = list(ys_lat) + (list(ys_ctx) if with_ctx else [])
    return pl.pallas_call(
        functools.partial(_merge_kernel, n_lat_tiles=n_lat_tiles, with_ctx=with_ctx),
        grid=(n // tn, n_rows // tm),
        in_specs=in_specs,
        out_specs=pl.BlockSpec((tm, tn), lambda j, i: (i, j)),
        out_shape=jax.ShapeDtypeStruct((n_rows, n), BF16),
        scratch_shapes=[pltpu.VMEM((3, k, tn), BF16)],
        compiler_params=_params(2),
        name="branch_merge",
    )(*args, w_branch, proj_h, proj_h, proj_h)


def _head_rms(x, g):
    return x * lax.rsqrt(jnp.mean(x * x, axis=-1, keepdims=True) + EPS) * g


def _rope(x, cos, sin, low_half):
    partner = jnp.where(low_half, pltpu.roll(x, 96, axis=1), pltpu.roll(x, 32, axis=1))
    return x * cos + partner * sin


def _att_prep_kernel(q_ref, k_ref, cos_ref, sin_ref, qg_ref, kg_ref, qo_ref, ko_ref):
    cos = cos_ref[...]
    sin = sin_ref[...]
    lane = lax.broadcasted_iota(jnp.int32, cos.shape, 1)
    low_half = jnp.bitwise_and(lane, 63) < 32
    scale = HEAD_DIM ** -0.5 * LOG2E
    for h in range(ATT_HEADS):
        sl = slice(h * HEAD_DIM, (h + 1) * HEAD_DIM)
        x = _head_rms(q_ref[:, sl].astype(F32), qg_ref[...])
        qo_ref[:, sl] = (_rope(x, cos, sin, low_half) * scale).astype(qo_ref.dtype)
    for h in range(ATT_KV_HEADS):
        sl = slice(h * HEAD_DIM, (h + 1) * HEAD_DIM)
        x = _head_rms(k_ref[:, sl].astype(F32), kg_ref[...])
        ko_ref[:, sl] = _rope(x, cos, sin, low_half).astype(ko_ref.dtype)


def _att_prep(proj_h, cos, sin, qg, kg):
    m = proj_h.shape[0]
    tm = ROW_TILE
    return pl.pallas_call(
        _att_prep_kernel,
        grid=(m // tm,),
        in_specs=[pl.BlockSpec((tm, ATT_W), lambda i: (i, H_ATT_Q // ATT_W)),
                  pl.BlockSpec((tm, KV_W), lambda i: (i, H_ATT_K // KV_W)),
                  pl.BlockSpec((tm, HEAD_DIM), lambda i: (i, 0)),
                  pl.BlockSpec((tm, HEAD_DIM), lambda i: (i, 0)),
                  pl.BlockSpec((1, HEAD_DIM), lambda i: (0, 0)),
                  pl.BlockSpec((1, HEAD_DIM), lambda i: (0, 0))],
        out_specs=[pl.BlockSpec((tm, ATT_W), lambda i: (i, 0)),
                   pl.BlockSpec((tm, KV_W), lambda i: (i, 0))],
        out_shape=[jax.ShapeDtypeStruct((m, ATT_W), BF16),
                   jax.ShapeDtypeStruct((m, KV_W), BF16)],
        compiler_params=_params(1),
        name="att_prep",
    )(proj_h, proj_h, cos, sin, qg.reshape(1, HEAD_DIM), kg.reshape(1, HEAD_DIM))


def _att_kernel(*refs, with_latent):
    if with_latent:
        q_ref, kc_ref, vc_ref, kl_ref, vl_ref, o_ref = refs
    else:
        q_ref, kc_ref, vc_ref, o_ref = refs
    grp = ATT_HEADS // ATT_KV_HEADS
    for g in range(grp):
        sl = slice(g * HEAD_DIM, (g + 1) * HEAD_DIM)
        q = q_ref[:, sl]
        sc = lax.dot_general(q, kc_ref[...], _NT, preferred_element_type=F32)
        m = jnp.max(sc, axis=-1, keepdims=True)
        if with_latent:
            sl_ = lax.dot_general(q, kl_ref[...], _NT, preferred_element_type=F32)
            m = jnp.maximum(m, jnp.max(sl_, axis=-1, keepdims=True))
        pc = jnp.exp2(sc - m)
        den = jnp.sum(pc, axis=-1, keepdims=True)
        acc = jnp.dot(pc.astype(BF16), vc_ref[...], preferred_element_type=F32)
        if with_latent:
            pl_ = jnp.exp2(sl_ - m)
            den += jnp.sum(pl_, axis=-1, keepdims=True)
            acc += jnp.dot(pl_.astype(BF16), vl_ref[...], preferred_element_type=F32)
        o_ref[:, sl] = (acc / den).astype(o_ref.dtype)


def _attention(q_rot, k_rot, proj_h, n_batch, seq, ctx_len, latent_queries):
    grp_w = ATT_W // ATT_KV_HEADS
    ctx_blk0 = n_batch * seq // ctx_len
    v_col = H_ATT_V // HEAD_DIM
    if latent_queries:
        tq = ATT_TQ
        nq = seq // tq
        q_map = lambda b, k, i: (b * nq + i, k)
        n_out = n_batch * seq
    else:
        tq = ctx_len
        nq = 1
        q_map = lambda b, k, i: (ctx_blk0 + b, k)
        n_out = n_batch * ctx_len
    in_specs = [pl.BlockSpec((tq, grp_w), q_map),
                pl.BlockSpec((ctx_len, HEAD_DIM), lambda b, k, i: (ctx_blk0 + b, k)),
                pl.BlockSpec((ctx_len, HEAD_DIM), lambda b, k, i: (ctx_blk0 + b, v_col + k))]
    args = [q_rot, k_rot, proj_h]
    if latent_queries:
        in_specs += [pl.BlockSpec((seq, HEAD_DIM), lambda b, k, i: (b, k)),
                     pl.BlockSpec((seq, HEAD_DIM), lambda b, k, i: (b, v_col + k))]
        args += [k_rot, proj_h]
    return pl.pallas_call(
        functools.partial(_att_kernel, with_latent=latent_queries),
        grid=(n_batch, ATT_KV_HEADS, nq),
        in_specs=in_specs,
        out_specs=pl.BlockSpec((tq, grp_w), lambda b, k, i: (b * nq + i, k)),
        out_shape=jax.ShapeDtypeStruct((n_out, ATT_W), BF16),
        compiler_params=_params(3),
        name="attention_lat" if latent_queries else "attention_ctx",
    )(*args)


def _conv3(x_ref, w, c, n_chunks, chunk):
    r0 = pl.multiple_of(c * chunk, chunk)
    x = x_ref[pl.ds(r0, chunk), :].astype(F32)
    prev_blk = x_ref[pl.ds(pl.multiple_of(jnp.maximum(r0 - 16, 0), 16), 16), :].astype(F32)
    next_blk = x_ref[pl.ds(pl.multiple_of(jnp.minimum(r0 + chunk, (n_chunks - 1) * chunk), 16), 16), :].astype(F32)
    prev_row = jnp.where(c > 0, prev_blk[15:16, :], 0.0)
    next_row = jnp.where(c < n_chunks - 1, next_blk[0:1, :], 0.0)
    row = lax.broadcasted_iota(jnp.int32, x.shape, 0)
    x_m1 = jnp.where(row == 0, prev_row, pltpu.roll(x, 1, axis=0))
    x_p1 = jnp.where(row == chunk - 1, next_row, pltpu.roll(x, chunk - 1, axis=0))
    return x_m1 * w[0:1, :] + x * w[1:2, :] + x_p1 * w[2:3, :]


def _ml_chunks(chains):
    n_tok = chains[0][0].shape[0]
    ti = lax.broadcasted_iota(jnp.int32, (n_tok, n_tok), 0)
    si = lax.broadcasted_iota(jnp.int32, (n_tok, n_tok), 1)
    gate = []
    for q, k, v, lfr, lir, lfc, lic, c_ref, n_ref, m_st, rev in chains:
        causal = (si >= ti) if rev else (si <= ti)
        causal_t = (ti >= si) if rev else (ti <= si)
        fc_col = jnp.sum(jnp.where(causal, lfr, 0.0), axis=1, keepdims=True)
        fc_row = jnp.sum(jnp.where(causal_t, lfc, 0.0), axis=0, keepdims=True)
        f_tot = jnp.sum(lfr, axis=1, keepdims=True)
        dmat = jnp.where(causal, fc_col - fc_row + lir, -jnp.inf)
        inter = fc_col + m_st
        m_t = jnp.maximum(inter, jnp.max(dmat, axis=1, keepdims=True))
        w = jnp.exp(dmat - m_t)
        a = jnp.exp(inter - m_t)
        g_col = f_tot - fc_col + lic
        g_row = f_tot - fc_row + lir
        m_new = jnp.maximum(f_tot + m_st, jnp.max(g_row, axis=1, keepdims=True))
        decay = jnp.exp(f_tot + m_st - m_new)
        kw = k.astype(F32) * jnp.exp(g_col - m_new)
        gate.append((w, a, m_t, m_new, decay, kw))
    scs = [lax.dot_general(ch[0], ch[1], _NT, preferred_element_type=F32) * g[0]
           for ch, g in zip(chains, gate)]
    nums = [jnp.dot(sc.astype(BF16), ch[2], preferred_element_type=F32) for ch, sc in zip(chains, scs)]
    carried = [jnp.dot(ch[0], ch[7][...].astype(BF16), preferred_element_type=F32) for ch in chains]
    hs = []
    for ch, g, sc, num, car in zip(chains, gate, scs, nums, carried):
        q, n_ref = ch[0], ch[8]
        w, a, m_t, m_new, decay, kw = g
        den = jnp.sum(sc, axis=1, keepdims=True)
        den += a * jnp.sum(q.astype(F32) * n_ref[...], axis=1, keepdims=True)
        hs.append((num + a * car) / jnp.maximum(jnp.abs(den), jnp.exp(-m_t)))
    for ch, g in zip(chains, gate):
        v, c_ref, n_ref = ch[2], ch[7], ch[8]
        w, a, m_t, m_new, decay, kw = g
        c_ref[...] = decay * c_ref[...] + lax.dot_general(kw.astype(BF16), v, _TN, preferred_element_type=F32)
        n_ref[...] = decay * n_ref[...] + jnp.sum(kw, axis=0, keepdims=True)
    return hs, [g[3] for g in gate]


def _ml_kernel(*refs, seq, ctx_len, need_ctx):
    (ql_ref, kl_ref, vl_ref, ol_ref, qc_ref, kc_ref, vc_ref, oc_ref,
     wq_ref, wk_ref, gcl_ref, grl_ref, gcc_ref, grc_ref, bc_ref, br_ref, ng_ref) = refs[:17]
    if need_ctx:
        yl_ref, yc_ref = refs[17:19]
        scratch = refs[19:]
    else:
        yl_ref = refs[17]
        yc_ref = None
        scratch = refs[18:]
    qsl_ref, ksl_ref, qsc_ref, ksc_ref, hl_ref, hc_ref, cf_ref, nf_ref, cb_ref, nb_ref = scratch
    chunk = ML_CHUNK
    k_scale = ML_DK ** -0.5
    wq = wq_ref[0]
    wk = wk_ref[0]

    def conv_pass(src_q, src_k, dst_q, dst_k, n_chunks):
        def body(c, carry):
            r0 = pl.multiple_of(c * chunk, chunk)
            dst_q[pl.ds(r0, chunk), :] = _conv3(src_q, wq, c, n_chunks, chunk).astype(BF16)
            dst_k[pl.ds(r0, chunk), :] = (_conv3(src_k, wk, c, n_chunks, chunk) * k_scale).astype(BF16)
            return carry
        lax.fori_loop(0, n_chunks, body, 0)

    conv_pass(qc_ref, kc_ref, qsc_ref, ksc_ref, ctx_len // chunk)
    conv_pass(ql_ref, kl_ref, qsl_ref, ksl_ref, seq // chunk)

    bias_c = bc_ref[0]
    bias_r = br_ref[0]
    norm_g = ng_ref[0]

    def run(q_s, k_s, v_s, o_s, gc_s, gr_s, h_s, y_s, n_chunks, m_states):
        def chain(c, rev, c_r, n_r, m_st):
            d = 1 if rev else 0
            rows = pl.ds(pl.multiple_of(c * chunk, chunk), chunk)
            gc = gc_s[0, rows, :] + bias_c
            gr = gr_s[0, :, rows] + bias_r
            lic = gc[:, d:d + 1]
            lfc = _log_sigmoid(gc[:, 2 + d:3 + d])
            lir = gr[d:d + 1, :]
            lfr = _log_sigmoid(gr[2 + d:3 + d, :])
            return rows, (q_s[rows, :], k_s[rows, :], v_s[rows, :], lfr, lir, lfc, lic, c_r, n_r, m_st, rev)

        def finish(rows, h_sum):
            if y_s is not None:
                y = _head_rms(h_sum, norm_g) * jax.nn.sigmoid(o_s[rows, :].astype(F32))
                y_s[rows, :] = y.astype(y_s.dtype)

        def step(ci, m_states, phase):
            rows_f, chain_f = chain(ci, False, cf_ref, nf_ref, m_states[0])
            rows_b, chain_b = chain(n_chunks - 1 - ci, True, cb_ref, nb_ref, m_states[1])
            (h_f, h_b), m_new = _ml_chunks([chain_f, chain_b])
            if phase == 0:
                h_s[rows_f, :] = h_f
                h_s[rows_b, :] = h_b
            elif phase == 1:
                finish(rows_f, h_f + h_b)
            else:
                finish(rows_f, h_s[rows_f, :] + h_f)
                finish(rows_b, h_s[rows_b, :] + h_b)
            return tuple(m_new)

        half = n_chunks // 2
        m_states = lax.fori_loop(0, half, lambda ci, m: step(ci, m, 0), m_states)
        if n_chunks % 2:
            m_states = step(half, m_states, 1)
        return lax.fori_loop(n_chunks - half, n_chunks, lambda ci, m: step(ci, m, 2), m_states)

    for ref in (cf_ref, nf_ref, cb_ref, nb_ref):
        ref[...] = jnp.zeros_like(ref)
    m0 = jnp.full((1, 1), NEG_INIT, F32)
    m1 = run(qsc_ref, ksc_ref, vc_ref, oc_ref, gcc_ref, grc_ref, hc_ref, yc_ref, ctx_len // chunk, (m0, m0))
    run(qsl_ref, ksl_ref, vl_ref, ol_ref, gcl_ref, grl_ref, hl_ref, yl_ref, seq // chunk, m1)


def _mlstm(proj_h, gates_c, gates_r, conv_w, gate_b, norm_g, n_batch, seq, ctx_len, need_ctx):
    w = ML_DK
    cb0 = n_batch * seq // ctx_len
    col = lambda off: off // w
    lat = lambda off: pl.BlockSpec((seq, w), lambda b, h: (b, col(off) + h))
    ctx = lambda off: pl.BlockSpec((ctx_len, w), lambda b, h: (cb0 + b, col(off) + h))
    conv = conv_w.reshape(3, 2 * ML_HEADS, w).transpose(1, 0, 2)
    gb = gate_b.reshape(4, ML_HEADS).T
    in_specs = [lat(H_ML_Q), lat(H_ML_K), lat(H_ML_V), lat(H_ML_O),
                ctx(H_ML_Q), ctx(H_ML_K), ctx(H_ML_V), ctx(H_ML_O),
                pl.BlockSpec((1, 3, w), lambda b, h: (h, 0, 0)),
                pl.BlockSpec((1, 3, w), lambda b, h: (ML_HEADS + h, 0, 0)),
                pl.BlockSpec((1, seq, 4), lambda b, h: (h, b, 0)),
                pl.BlockSpec((1, 4, seq), lambda b, h: (h, 0, b)),
                pl.BlockSpec((1, ctx_len, 4), lambda b, h: (h, cb0 + b, 0)),
                pl.BlockSpec((1, 4, ctx_len), lambda b, h: (h, 0, cb0 + b)),
                pl.BlockSpec((1, 1, 4), lambda b, h: (h, 0, 0)),
                pl.BlockSpec((1, 4, 1), lambda b, h: (h, 0, 0)),
                pl.BlockSpec((1, 1, w), lambda b, h: (h, 0, 0))]
    out_specs = [pl.BlockSpec((seq, w), lambda b, h: (b, h))]
    out_shape = [jax.ShapeDtypeStruct((n_batch * seq, ML_W), BF16)]
    if need_ctx:
        out_specs.append(pl.BlockSpec((ctx_len, w), lambda b, h: (b, h)))
        out_shape.append(jax.ShapeDtypeStruct((n_batch * ctx_len, ML_W), BF16))
    outs = pl.pallas_call(
        functools.partial(_ml_kernel, seq=seq, ctx_len=ctx_len, need_ctx=need_ctx),
        grid=(n_batch, ML_HEADS),
        in_specs=in_specs,
        out_specs=out_specs,
        out_shape=out_shape,
        scratch_shapes=[pltpu.VMEM((seq, w), BF16), pltpu.VMEM((seq, w), BF16),
                        pltpu.VMEM((ctx_len, w), BF16), pltpu.VMEM((ctx_len, w), BF16),
                        pltpu.VMEM((seq, w), F32), pltpu.VMEM((ctx_len, w), F32),
                        pltpu.VMEM((w, w), F32), pltpu.VMEM((1, w), F32),
                        pltpu.VMEM((w, w), F32), pltpu.VMEM((1, w), F32)],
        compiler_params=_params(2),
        name="mlstm",
    )(proj_h, proj_h, proj_h, proj_h, proj_h, proj_h, proj_h, proj_h,
      conv, conv, gates_c, gates_r, gates_c, gates_r,
      gb.reshape(ML_HEADS, 1, 4), gb.reshape(ML_HEADS, 4, 1), norm_g.reshape(ML_HEADS, 1, w))
    return outs if need_ctx else (outs[0], None)


def _hg_level_tables(chunk, rev):
    levels = []
    n = HG_LEAF
    while 2 * n <= chunk:
        refs = []
        for start in range(0, chunk, 2 * n):
            refs.append((start + n - 1) if rev else (start + n))
        levels.append((n, refs))
        n *= 2
    return levels


def _tri_cumsum(tri, x):
    hi = x.astype(BF16)
    rest = x - hi.astype(F32)
    mid = rest.astype(BF16)
    lo = (rest - mid.astype(F32)).astype(BF16)
    out = jnp.dot(tri, hi, preferred_element_type=F32)
    out += jnp.dot(tri, mid, preferred_element_type=F32)
    return out + jnp.dot(tri, lo, preferred_element_type=F32)


def _hg_chunks(chains, loglb, log1mlb, one_m_lb):
    chunk = chains[0][1].shape[0]
    nb = chunk // HG_LEAF

    qs, ks, lks, gcums = [], [], [], []
    for qraw, z, v, st_ref, masks, tri, rev in chains:
        e = jnp.exp(-jnp.abs(z))
        one_pe = 1.0 + e
        lsig = jnp.minimum(z, 0.0) - jnp.log(one_pe)
        a = log1mlb + lsig
        lf = jnp.maximum(loglb, a) + jnp.log(1.0 + jnp.exp(-jnp.abs(loglb - a)))
        ks.append(one_m_lb * (jnp.where(z >= 0.0, e, 1.0) / one_pe))
        lks.append((a - z) * LOG2E)
        gcums.append(_tri_cumsum(tri, lf * LOG2E))
        x = qraw.astype(F32)
        qs.append(x / (1.0 + jnp.exp(-x)))

    atts = [jnp.zeros((chunk, chunk), F32) for _ in chains]
    for lvl, (n, _) in enumerate(_hg_level_tables(chunk, False)):
        for c, (qraw, z, v, st_ref, masks, tri, rev) in enumerate(chains):
            refs = _hg_level_tables(chunk, rev)[lvl][1]
            gref = jnp.concatenate(
                [jnp.broadcast_to(gcums[c][r:r + 1, :], (2 * n, HG_DK)) for r in refs], axis=0)
            dec = jnp.exp2(-jnp.abs(gcums[c] - gref))
            mixed = jnp.concatenate(
                [(qs[c] if ((r0 % (2 * n)) >= n) != rev else ks[c])[r0:r0 + HG_LEAF, :]
                 for r0 in range(0, chunk, HG_LEAF)], axis=0)
            md = (mixed * dec).astype(BF16)
            p = lax.dot_general(md, md, _NT, preferred_element_type=F32)
            atts[c] += jnp.where(masks[0][lvl], p, 0.0)

    inters = []
    for c, (qraw, z, v, st_ref, masks, tri, rev) in enumerate(chains):
        g_tot = gcums[c][0:1, :] if rev else gcums[c][chunk - 1:chunk, :]
        q_dec = (qs[c] * jnp.exp2(gcums[c])).astype(BF16)
        k_dec = (ks[c] * jnp.exp2(g_tot - gcums[c])).astype(BF16)
        st = st_ref[...]
        inters.append(lax.dot_general(q_dec, st.astype(BF16), _NT, preferred_element_type=F32))
        st_ref[...] = st * jnp.exp2(g_tot) + lax.dot_general(v, k_dec, _TN, preferred_element_type=F32)

    outs = []
    for c, (qraw, z, v, st_ref, masks, tri, rev) in enumerate(chains):
        g3 = gcums[c].reshape(nb, HG_LEAF, HG_DK)
        h3 = (lks[c] - gcums[c]).reshape(nb, HG_LEAF, HG_DK)
        q3 = qs[c].reshape(nb, HG_LEAF, HG_DK)
        att = atts[c]
        for s in range(HG_LEAF):
            col = jnp.sum(q3 * jnp.exp2(g3 + h3[:, s:s + 1, :]), axis=2, keepdims=True)
            att = jnp.where(masks[1][s], col.reshape(chunk, 1), att)
        outs.append(inters[c] + jnp.dot(att.astype(BF16), v, preferred_element_type=F32))
    return outs


def _hg_kernel(*refs, seq, ctx_len, need_ctx):
    (ql_ref, il_ref, gl_ref, ffl_ref, fbl_ref, qc_ref, ic_ref, gc_ref, ffc_ref, fbc_ref,
     llb_ref, l1m_ref, oml_ref, ng_ref) = refs[:14]
    if need_ctx:
        yl_ref, yc_ref = refs[14:16]
        scratch = refs[16:]
    else:
        yl_ref = refs[14]
        yc_ref = None
        scratch = refs[15:]
    ol_ref, oc_ref, stf_ref, stb_ref = scratch
    chunk = HG_CHUNK
    loglb = llb_ref[0]
    log1mlb = l1m_ref[0]
    one_m_lb = oml_ref[0]
    norm_g = ng_ref[0]
    ti = lax.broadcasted_iota(jnp.int32, (chunk, chunk), 0)
    si = lax.broadcasted_iota(jnp.int32, (chunk, chunk), 1)

    def direction_consts(rev):
        tri = jnp.where((si >= ti) if rev else (si <= ti), 1.0, 0.0).astype(BF16)
        masks = []
        for n, _ in _hg_level_tables(chunk, rev):
            shift = (2 * n).bit_length() - 1
            same = jnp.right_shift(ti, shift) == jnp.right_shift(si, shift)
            t_late = jnp.bitwise_and(ti, 2 * n - 1) >= n
            s_late = jnp.bitwise_and(si, 2 * n - 1) >= n
            if rev:
                masks.append(same & jnp.logical_not(t_late) & s_late)
            else:
                masks.append(same & t_late & jnp.logical_not(s_late))
        t_in = jnp.bitwise_and(ti, HG_LEAF - 1)
        leaf_masks = []
        for s in range(HG_LEAF):
            at_col = si == (ti - t_in + s)
            leaf_masks.append(at_col & ((t_in <= s) if rev else (t_in >= s)))
        return tri, (masks, leaf_masks)

    tri_f, masks_f = direction_consts(False)
    tri_b, masks_b = direction_consts(True)

    def run(q_s, i_s, g_s, zf_s, zb_s, o_s, y_s, n_chunks):
        def step(pi, second):
            rows = []
            chains = []
            for u in range(HG_UNROLL):
                c = pi * HG_UNROLL + u
                r_f = pl.ds(pl.multiple_of(c * chunk, chunk), chunk)
                rows.append(r_f)
                chains.append((q_s[r_f, :], zf_s[r_f, :], i_s[r_f, :], stf_ref, masks_f, tri_f, False))
            for u in range(HG_UNROLL):
                c = n_chunks - 1 - (pi * HG_UNROLL + u)
                r_b = pl.ds(pl.multiple_of(c * chunk, chunk), chunk)
                rows.append(r_b)
                chains.append((q_s[r_b, :], zb_s[r_b, :], i_s[r_b, :], stb_ref, masks_b, tri_b, True))
            outs = _hg_chunks(chains, loglb, log1mlb, one_m_lb)
            for r, o in zip(rows, outs):
                if not second:
                    o_s[r, :] = o
                elif y_s is not None:
                    y = _head_rms(o_s[r, :] + o, norm_g) * _silu(g_s[r, :].astype(F32))
                    y_s[r, :] = y.astype(y_s.dtype)

        def first(pi, carry):
            step(pi, False)
            return carry

        def second(pi, carry):
            step(pi, True)
            return carry

        n_steps = n_chunks // HG_UNROLL
        lax.fori_loop(0, n_steps // 2, first, 0)
        lax.fori_loop(n_steps // 2, n_steps, second, 0)

    stf_ref[...] = jnp.zeros_like(stf_ref)
    stb_ref[...] = jnp.zeros_like(stb_ref)
    run(qc_ref, ic_ref, gc_ref, ffc_ref, fbc_ref, oc_ref, yc_ref, ctx_len // chunk)
    run(ql_ref, il_ref, gl_ref, ffl_ref, fbl_ref, ol_ref, yl_ref, seq // chunk)


def _hgrn2(proj_h, proj_f, lb, norm_g, n_batch, seq, ctx_len, need_ctx):
    w = HG_DK
    cb0 = n_batch * seq // ctx_len
    lat = lambda off: pl.BlockSpec((seq, w), lambda b, h: (b, off // w + h))
    ctx = lambda off: pl.BlockSpec((ctx_len, w), lambda b, h: (cb0 + b, off // w + h))
    vec = pl.BlockSpec((1, 1, w), lambda b, h: (h, 0, 0))
    lb3 = lb.reshape(HG_HEADS, 1, w)
    in_specs = [lat(H_HG_Q), lat(H_HG_I), lat(H_HG_G), lat(F_HG_FF), lat(F_HG_FB),
                ctx(H_HG_Q), ctx(H_HG_I), ctx(H_HG_G), ctx(F_HG_FF), ctx(F_HG_FB),
                vec, vec, vec, vec]
    out_specs = [pl.BlockSpec((seq, w), lambda b, h: (b, h))]
    out_shape = [jax.ShapeDtypeStruct((n_batch * seq, HG_W), BF16)]
    if need_ctx:
        out_specs.append(pl.BlockSpec((ctx_len, w), lambda b, h: (b, h)))
        out_shape.append(jax.ShapeDtypeStruct((n_batch * ctx_len, HG_W), BF16))
    outs = pl.pallas_call(
        functools.partial(_hg_kernel, seq=seq, ctx_len=ctx_len, need_ctx=need_ctx),
        grid=(n_batch, HG_HEADS),
        in_specs=in_specs,
        out_specs=out_specs,
        out_shape=out_shape,
        scratch_shapes=[pltpu.VMEM((seq, w), F32), pltpu.VMEM((ctx_len, w), F32),
                        pltpu.VMEM((w, w), F32), pltpu.VMEM((w, w), F32)],
        compiler_params=_params(2),
        name="hgrn2",
    )(proj_h, proj_h, proj_h, proj_f, proj_f, proj_h, proj_h, proj_h, proj_f, proj_f,
      jnp.log(lb3), jnp.log1p(-lb3), 1.0 - lb3, norm_g.reshape(HG_HEADS, 1, w))
    return outs if need_ctx else (outs[0], None)


def _router_kernel(x_ref, g_ref, sh_ref, sc_ref, wr_ref, h_ref, idx_ref, wt_ref, cnt_ref, carry_ref):
    @pl.when(pl.program_id(0) == 0)
    def _():
        carry_ref[...] = jnp.zeros_like(carry_ref)

    x = x_ref[...]
    y = x * lax.rsqrt(jnp.mean(x * x, axis=-1, keepdims=True) + EPS) * g_ref[...]
    h = y * (1.0 + sc_ref[0]) + sh_ref[0]
    h_ref[...] = h.astype(h_ref.dtype)
    logits = jnp.dot(h, wr_ref[...], precision=lax.Precision.HIGHEST, preferred_element_type=F32)
    lane = lax.broadcasted_iota(jnp.int32, logits.shape, 1).astype(F32)
    lg = jnp.where(lane < N_EXPERTS, logits, -jnp.inf)
    m1 = jnp.max(lg, axis=-1, keepdims=True)
    i1 = jnp.min(jnp.where(lg == m1, lane, float(LANE)), axis=-1, keepdims=True)
    lg2 = jnp.where(lane == i1, -jnp.inf, lg)
    m2 = jnp.max(lg2, axis=-1, keepdims=True)
    i2 = jnp.min(jnp.where(lg2 == m2, lane, float(LANE)), axis=-1, keepdims=True)
    e = jnp.exp(m2 - m1)
    w1 = 1.0 / (1.0 + e)
    w2 = e / (1.0 + e)
    wt_ref[...] = jnp.where(lane == 0, w1, jnp.where(lane == 1, w2, 0.0))
    n_tok = x.shape[0]
    routed = jnp.where(jnp.logical_or(lane == i1, lane == i2), 1.0, 0.0)
    ti = lax.broadcasted_iota(jnp.int32, (n_tok, n_tok), 0)
    ui = lax.broadcasted_iota(jnp.int32, (n_tok, n_tok), 1)
    earlier = jnp.where(ui < ti, 1.0, 0.0).astype(BF16)
    before = carry_ref[...] + jnp.dot(earlier, routed.astype(BF16), preferred_element_type=F32)
    r1 = jnp.sum(jnp.where(lane == i1, before, 0.0), axis=-1, keepdims=True)
    r2 = jnp.sum(jnp.where(lane == i2, before, 0.0), axis=-1, keepdims=True)
    carry_ref[...] += jnp.sum(routed, axis=0, keepdims=True)
    cnt_ref[...] = jnp.broadcast_to(carry_ref[...], cnt_ref.shape).astype(jnp.int32)
    idx_ref[...] = jnp.where(lane == 0, i1, jnp.where(lane == 1, i2, jnp.where(
        lane == 2, r1, jnp.where(lane == 3, r2, 0.0)))).astype(jnp.int32)


def _router(x, g, shift, scale, w_router, n_rows, seq, n_batch):
    d = x.shape[1]
    tm = 256
    tpb = seq // tm
    grp = lambda i: (_group_of(i, tpb, n_batch), 0, 0)
    wr = jnp.pad(w_router, ((0, 0), (0, LANE - N_EXPERTS)))
    return pl.pallas_call(
        _router_kernel,
        grid=(n_rows // tm,),
        in_specs=[pl.BlockSpec((tm, d), lambda i: (i, 0)),
                  pl.BlockSpec((1, d), lambda i: (0, 0)),
                  pl.BlockSpec((1, 1, d), grp),
                  pl.BlockSpec((1, 1, d), grp),
                  pl.BlockSpec((d, LANE), lambda i: (0, 0))],
        out_specs=[pl.BlockSpec((tm, d), lambda i: (i, 0)),
                   pl.BlockSpec((tm, LANE), lambda i: (i, 0)),
                   pl.BlockSpec((tm, LANE), lambda i: (i, 0)),
                   pl.BlockSpec((8, LANE), lambda i: (0, 0))],
        out_shape=[jax.ShapeDtypeStruct((n_rows, d), F32),
                   jax.ShapeDtypeStruct((n_rows, LANE), jnp.int32),
                   jax.ShapeDtypeStruct((n_rows, LANE), F32),
                   jax.ShapeDtypeStruct((8, LANE), jnp.int32)],
        scratch_shapes=[pltpu.VMEM((1, LANE), F32)],
        compiler_params=_params(1),
        name="router",
    )(x, g.reshape(1, d), shift, scale, wr)


def _row_copy(src_hbm, buf_ref, sem, src_row, r):
    return pltpu.make_async_copy(src_hbm.at[pl.ds(src_row, 1)], buf_ref.at[pl.ds(r, 1)], sem)


def _dispatch_kernel(src_ref, nu_ref, h_hbm, o_ref, buf_ref, sem):
    t = pl.program_id(0)
    tm = buf_ref.shape[1]
    slot = t % 2

    def tile_copies(tile, slot, fn):
        def body(r, carry):
            fn(_row_copy(h_hbm, buf_ref.at[slot], sem.at[slot], src_ref[tile * tm + r], r))
            return carry
        lax.fori_loop(0, tm, body, 0, unroll=8)

    @pl.when(jnp.logical_and(t == 0, nu_ref[0] > 0))
    def _():
        tile_copies(0, 0, lambda cp: cp.start())

    @pl.when(t + 1 < nu_ref[0])
    def _():
        tile_copies(t + 1, 1 - slot, lambda cp: cp.start())

    @pl.when(t < nu_ref[0])
    def _():
        tile_copies(t, slot, lambda cp: cp.wait())
        o_ref[...] = buf_ref[slot].astype(o_ref.dtype)

    @pl.when(t >= nu_ref[0])
    def _():
        o_ref[...] = jnp.zeros_like(o_ref)


def _dispatch(h, src_token, n_used):
    d = h.shape[1]
    tm = MOE_TILE
    n_pad = src_token.shape[0]
    return pl.pallas_call(
        _dispatch_kernel,
        grid_spec=pltpu.PrefetchScalarGridSpec(
            num_scalar_prefetch=2,
            grid=(n_pad // tm,),
            in_specs=[pl.BlockSpec(memory_space=pl.ANY)],
            out_specs=pl.BlockSpec((tm, d), lambda t, src, nu: (t, 0)),
            scratch_shapes=[pltpu.VMEM((2, tm, d), F32), pltpu.SemaphoreType.DMA((2,))]),
        out_shape=jax.ShapeDtypeStruct((n_pad, d), BF16),
        compiler_params=_params(1),
        name="moe_dispatch",
    )(src_token, n_used, h)


def _new_expert(te_ref, i):
    return jnp.logical_or(i == 0, te_ref[i] != te_ref[jnp.maximum(i - 1, 0)])


def _moe_kernel(te_ref, nu_ref, nxt_ref, valid_ref, a_ref, *refs, n_w, glu):
    w_hbm = refs[:n_w]
    o_ref = refs[n_w]
    wbuf, wbf, sem, slot_ref = refs[n_w + 1:]
    j, i = pl.program_id(0), pl.program_id(1)
    n_j, n_i = pl.num_programs(0), pl.num_programs(1)
    tn = o_ref.shape[1]

    def weight_copies(slot, expert, col_tile):
        cols = pl.ds(pl.multiple_of(col_tile * tn, tn), tn)
        return [pltpu.make_async_copy(w_hbm[x].at[expert, :, cols], wbuf.at[slot, x], sem.at[slot, x])
                for x in range(n_w)]

    @pl.when(jnp.logical_and(j == 0, i == 0))
    def _():
        slot_ref[0] = 0
        for cp in weight_copies(0, te_ref[0], 0):
            cp.start()

    @pl.when(_new_expert(te_ref, i))
    def _():
        slot = slot_ref[0]
        for cp in weight_copies(slot, te_ref[i], j):
            cp.wait()
        nxt = nxt_ref[i]
        same_col = nxt < n_i

        @pl.when(jnp.logical_or(same_col, j + 1 < n_j))
        def _():
            nxt_expert = te_ref[jnp.where(same_col, nxt, 0)]
            for cp in weight_copies(1 - slot, nxt_expert, jnp.where(same_col, j, j + 1)):
                cp.start()

        for x in range(n_w):
            wbf[x] = wbuf[slot, x].astype(BF16)
        slot_ref[0] = 1 - slot

    def compute(rows):
        a = a_ref[rows, :]
        if glu:
            g = jnp.dot(a, wbf[0], preferred_element_type=F32)
            u = jnp.dot(a, wbf[1], preferred_element_type=F32)
            o_ref[rows, :] = (_silu(g) * u).astype(o_ref.dtype)
        else:
            o_ref[rows, :] = jnp.dot(a, wbf[0], preferred_element_type=F32).astype(o_ref.dtype)

    tm = o_ref.shape[0]
    valid = valid_ref[i]

    @pl.when(valid == tm)
    def _():
        compute(slice(None))

    @pl.when(valid < tm)
    def _():
        for r in range(tm // MOE_SUB_ROWS):
            rows = slice(r * MOE_SUB_ROWS, (r + 1) * MOE_SUB_ROWS)

            @pl.when(r * MOE_SUB_ROWS < valid)
            def _():
                compute(rows)

            @pl.when(r * MOE_SUB_ROWS >= valid)
            def _():
                o_ref[rows, :] = jnp.zeros((MOE_SUB_ROWS, o_ref.shape[1]), o_ref.dtype)


def _moe_grouped(a, weights, tile_expert, n_used, next_run, tile_valid, glu, out_dtype, name):
    n_rows, k = a.shape
    n = weights[0].shape[2]
    n_w = len(weights)
    tm, tn = MOE_TILE, COL_TILE
    a_map = lambda j, i, te, nu, nxt, tv: (jnp.maximum(jnp.minimum(i, nu[0] - 1), 0), 0)
    return pl.pallas_call(
        functools.partial(_moe_kernel, n_w=n_w, glu=glu),
        grid_spec=pltpu.PrefetchScalarGridSpec(
            num_scalar_prefetch=4,
            grid=(n // tn, n_rows // tm),
            in_specs=[pl.BlockSpec((tm, k), a_map)] + [pl.BlockSpec(memory_space=pl.ANY)] * n_w,
            out_specs=pl.BlockSpec((tm, tn), lambda j, i, te, nu, nxt, tv: (i, j)),
            scratch_shapes=[pltpu.VMEM((2, n_w, k, tn), F32), pltpu.VMEM((n_w, k, tn), BF16),
                            pltpu.SemaphoreType.DMA((2, n_w)), pltpu.SMEM((1,), jnp.int32)]),
        out_shape=jax.ShapeDtypeStruct((n_rows, n), out_dtype),
        compiler_params=_params(2),
        name=name,
    )(tile_expert, n_used, next_run, tile_valid, a, *weights)


def _combine_kernel(x_ref, ya_ref, yb_ref, wt_ref, g_ref, o_ref):
    wt = wt_ref[...]
    y = wt[:, 0:1] * ya_ref[...].astype(F32) + wt[:, 1:2] * yb_ref[...].astype(F32)
    o_ref[...] = x_ref[...] + g_ref[0] * y


def _moe_combine(x, ya, yb, wt, gate, n_rows, seq, n_batch):
    d = x.shape[1]
    tm = 256
    tpb = seq // tm
    row = pl.BlockSpec((tm, d), lambda i: (i, 0))
    return pl.pallas_call(
        _combine_kernel,
        grid=(n_rows // tm,),
        in_specs=[row, row, row,
                  pl.BlockSpec((tm, LANE), lambda i: (i, 0)),
                  pl.BlockSpec((1, 1, d), lambda i: (_group_of(i, tpb, n_batch), 0, 0))],
        out_specs=row,
        out_shape=jax.ShapeDtypeStruct((n_rows, d), F32),
        compiler_params=_params(1),
        name="moe_combine",
    )(x, ya, yb, wt, gate)


def _moe_ffn(x, g, shift, scale, gate, w_router, wg, wu, wd, n_rows, seq, n_batch):
    h, idx, wt, cnt = _router(x, g, shift, scale, w_router, n_rows, seq, n_batch)
    tm = MOE_TILE
    n_tiles = 2 * n_rows // tm + N_EXPERTS
    counts = cnt[0, :N_EXPERTS]
    tiles = (counts + tm - 1) // tm
    tile_end = jnp.cumsum(tiles)
    tile_start = tile_end - tiles
    expert = jnp.concatenate([idx[:, 0], idx[:, 1]])
    rank = jnp.concatenate([idx[:, 2], idx[:, 3]])
    token = jnp.concatenate([jnp.arange(n_rows, dtype=jnp.int32)] * 2)
    one_hot = (expert[:, None] == jnp.arange(N_EXPERTS, dtype=jnp.int32)[None, :]).astype(jnp.int32)
    dest = jnp.sum(tile_start[None, :] * one_hot, axis=1) * tm + rank
    src_token = jnp.zeros((n_tiles * tm,), jnp.int32).at[dest].set(token)
    n_used = tile_end[-1:].astype(jnp.int32)
    tile_id = jnp.minimum(jnp.arange(n_tiles, dtype=jnp.int32), n_used[0] - 1)
    tile_expert = jnp.sum((tile_end[None, :] <= tile_id[:, None]).astype(jnp.int32), axis=1)

    a_sorted = _dispatch(h, src_token, n_used)
    later = jnp.arange(n_tiles, dtype=jnp.int32)
    other = jnp.logical_and(later[None, :] > later[:, None], tile_expert[None, :] != tile_expert[:, None])
    next_run = jnp.min(jnp.where(other, later[None, :], n_tiles), axis=1).astype(jnp.int32)
    sel = (tile_expert[:, None] == jnp.arange(N_EXPERTS, dtype=jnp.int32)[None, :]).astype(jnp.int32)
    rows_left = jnp.sum(sel * counts[None, :], axis=1) - (later - jnp.sum(sel * tile_start[None, :], axis=1)) * tm
    tile_valid = jnp.where(later < n_used[0], jnp.clip(rows_left, 0, tm), 0).astype(jnp.int32)
    u = _moe_grouped(a_sorted, (wg, wu), tile_expert, n_used, next_run, tile_valid, True, BF16, "moe_up")
    y = _moe_grouped(u, (wd,), tile_expert, n_used, next_run, tile_valid, False, BF16, "moe_down")
    ya = jnp.take(y, dest[:n_rows], axis=0, mode="clip")
    yb = jnp.take(y, dest[n_rows:], axis=0, mode="clip")
    return _moe_combine(x, ya, yb, wt, gate, n_rows, seq, n_batch)


def _rope_tables(n_batch, seq, n_ctx_rows):
    t = jnp.arange(seq)
    pos = jnp.stack([t // GRID_W, t % GRID_W], axis=-1).astype(F32)
    inv_freq = jnp.exp(-jnp.log(ROPE_BASE) * jnp.arange(0, ROPE_AXIS_DIM, 2, dtype=F32) / ROPE_AXIS_DIM)
    ang = pos[..., None] * inv_freq
    cos, sin = jnp.cos(ang), jnp.sin(ang)
    cos = jnp.concatenate([cos[:, 0], cos[:, 0], cos[:, 1], cos[:, 1]], axis=-1)
    sin = jnp.concatenate([-sin[:, 0], sin[:, 0], -sin[:, 1], sin[:, 1]], axis=-1)
    cos = jnp.concatenate([jnp.tile(cos, (n_batch, 1)), jnp.ones((n_ctx_rows, HEAD_DIM), F32)], axis=0)
    sin = jnp.concatenate([jnp.tile(sin, (n_batch, 1)), jnp.zeros((n_ctx_rows, HEAD_DIM), F32)], axis=0)
    return cos, sin


def kernel(x, c, ctx, c_ctx, w_mod, b_mod, norm1_g, norm2_g, w_in, attn_q_norm_g, attn_k_norm_g, ml_conv_w,
           ml_gate_b, ml_norm_g, hg_lb_logits, hg_norm_g, w_branch, w_out, ffn_w_gate, ffn_w_up, ffn_w_down,
           moe_w_router, moe_w_gate, moe_w_up, moe_w_down):
    n_batch, seq, d = x.shape
    ctx_len = ctx.shape[1]
    depth = w_mod.shape[0]
    m_lat = n_batch * seq
    m_ctx = n_batch * ctx_len
    m_all = m_lat + m_ctx
    assert d == D_MODEL and w_in.shape[2] == _N_IN
    assert seq % ROW_TILE == 0 and m_ctx % ROW_TILE == 0 and seq % ctx_len == 0
    assert seq % ML_CHUNK == 0 and ctx_len % ML_CHUNK == 0 and ctx_len % HG_CHUNK == 0

    xa = jnp.concatenate([x.reshape(m_lat, d), ctx.reshape(m_ctx, d)], axis=0)
    cs = jnp.concatenate([c, c_ctx[None], jnp.zeros((8 - n_batch - 1, d), F32)], axis=0)
    mod = _modulation(cs, w_mod, b_mod)
    cos, sin = _rope_tables(n_batch, seq, m_ctx)
    lb_all = jnp.cumsum(jax.nn.softmax(hg_lb_logits.astype(F32), axis=0), axis=0)
    lb_all = lb_all - lb_all[:1]
    w_in_t = jnp.swapaxes(w_in, 1, 2)

    for l in range(depth):
        need_ctx = l < depth - 1
        n_rows = m_all if need_ctx else m_lat
        mods = [mod[l, :n_batch + 1, i * d:(i + 1) * d].reshape(n_batch + 1, 1, d) for i in range(N_MOD)]

        h = _norm_mod(xa, norm1_g[l], mods[0], mods[1], m_all, seq, n_batch)
        proj_h = _proj_in(h, w_in_t, l, *H_TILES, BF16)
        proj_f = _proj_in(h, w_in_t, l, *F_TILES, F32)

        q_rot, k_rot = _att_prep(proj_h, cos, sin, attn_q_norm_g[l], attn_k_norm_g[l])
        ya = _attention(q_rot, k_rot, proj_h, n_batch, seq, ctx_len, True)

        gates = proj_f[:, F_ML_G:F_ML_G + 4 * ML_HEADS].reshape(m_all, 4, ML_HEADS)
        gates_c = gates.transpose(2, 0, 1)
        gates_r = gates.transpose(2, 1, 0)
        ym, ym_c = _mlstm(proj_h, gates_c, gates_r, ml_conv_w[l], ml_gate_b[l], ml_norm_g[l],
                          n_batch, seq, ctx_len, need_ctx)
        yh, yh_c = _hgrn2(proj_h, proj_f, lb_all[l], hg_norm_g[l], n_batch, seq, ctx_len, need_ctx)
        ys_ctx = None
        if need_ctx:
            ya_c = _attention(q_rot, k_rot, proj_h, n_batch, seq, ctx_len, False)
            ys_ctx = (ya_c, ym_c, yh_c)

        merged = _branch_merge((ya, ym, yh), ys_ctx, w_branch, l, proj_h, n_rows)
        xa = _matmul_residual(merged, w_out, l, xa, mods[2], n_rows, seq, n_batch, "proj_out")

        if l % 2 == 0:
            h2 = _norm_mod(xa, norm2_g[l], mods[3], mods[4], n_rows, seq, n_batch)
            u = _glu(h2, ffn_w_gate, ffn_w_up, l // 2)
            xa = _matmul_residual(u, ffn_w_down, l // 2, xa, mods[5], n_rows, seq, n_batch, "ffn_down")
        else:
            xa = _moe_ffn(xa, norm2_g[l], mods[3], mods[4], mods[5], moe_w_router[l // 2],
                          moe_w_gate[l // 2], moe_w_up[l // 2], moe_w_down[l // 2], n_rows, seq, n_batch)
    return xa[:m_lat].reshape(n_batch, seq, d)
```

```python
import functools

import jax
import jax.numpy as jnp
from jax import lax
from jax.experimental import pallas as pl
from jax.experimental.pallas import tpu as pltpu

F32 = jnp.float32
BF16 = jnp.bfloat16

D_MODEL = 2048
GRID_W = 64
N_MOD = 6
ATT_HEADS = 8
ATT_KV_HEADS = 2
HEAD_DIM = 128
ATT_W = ATT_HEADS * HEAD_DIM
KV_W = ATT_KV_HEADS * HEAD_DIM
ROPE_AXIS_DIM = HEAD_DIM // 2
ROPE_BASE = 10000.0
ML_HEADS = 4
ML_DK = 256
ML_W = ML_HEADS * ML_DK
HG_HEADS = 8
HG_DK = 128
HG_W = HG_HEADS * HG_DK
D_FF = 5632
N_EXPERTS = 8
EPS = 1e-6
NEG_INIT = -1e30
LOG2E = 1.4426950408889634

_O_ATT_Q = 0
_O_ML_GATES = ATT_W + 2 * KV_W + 4 * ML_W
_O_HG_Q = _O_ML_GATES + 4 * ML_HEADS
_O_HG_FF = _O_HG_Q + HG_W
_O_HG_I = _O_HG_FF + 2 * HG_W
_O_MERGE = _O_HG_I + 2 * HG_W
_N_IN = _O_MERGE + 3 * D_MODEL

H_ATT_Q = 0
H_ATT_K = 1024
H_ATT_V = 1280
H_ML_Q = 1536
H_ML_K = 2560
H_ML_V = 3584
H_ML_O = 4608
H_HG_Q = 5632
H_HG_I = 6656
H_HG_G = 7680
H_MERGE = 8704
H_WIDTH = 14848
F_HG_FF = 0
F_HG_FB = 1024
F_ML_G = 2048

LANE = 128
ROW_TILE = 512
COL_TILE = 512
PROJ_ROW_TILES = 4
GLU_ROW_TILES = 8
ELEMENTWISE_ROWS = 512
W_MISALIGN = _O_HG_Q % LANE


def _source_tiles(segments):
    blocks, shifted = [], []
    for off, width in segments:
        for start in range(off, off + width, COL_TILE):
            assert start % COL_TILE in (0, W_MISALIGN)
            blocks.append(start // COL_TILE)
            shifted.append(int(start % COL_TILE != 0))
    return blocks, shifted


H_TILES = _source_tiles([(0, _O_ML_GATES), (_O_HG_Q, HG_W), (_O_HG_I, 2 * HG_W + 3 * D_MODEL)])
F_TILES = _source_tiles([(_O_HG_FF, 2 * HG_W), (_O_ML_GATES, COL_TILE)])
VMEM_LIMIT = 56 * 1024 * 1024

ML_CHUNK = 256
HG_CHUNK = 64
HG_LEAF = 8
HG_UNROLL = 2
ATT_TQ = 512
MOE_TILE = 512
MOE_SUB_ROWS = 128

_NT = (((1,), (1,)), ((), ()))
_TN = (((0,), (0,)), ((), ()))


def _params(n_axes):
    return pltpu.CompilerParams(dimension_semantics=("arbitrary",) * n_axes,
                                vmem_limit_bytes=VMEM_LIMIT)


def _silu(x):
    return x * jax.nn.sigmoid(x)


def _log_sigmoid(x):
    return jnp.minimum(x, 0.0) - jnp.log1p(jnp.exp(-jnp.abs(x)))


def _mod_kernel(c_ref, w_ref, b_ref, o_ref):
    a = _silu(c_ref[...])
    o_ref[...] = jnp.dot(a.astype(BF16), w_ref[...].astype(BF16), preferred_element_type=F32) + b_ref[...]


def _modulation(cs, w_mod, b_mod):
    depth, d, n = w_mod.shape
    tn = 1024
    return pl.pallas_call(
        _mod_kernel,
        grid=(depth, n // tn),
        in_specs=[pl.BlockSpec((8, d), lambda l, j: (0, 0)),
                  pl.BlockSpec((None, d, tn), lambda l, j: (l, 0, j)),
                  pl.BlockSpec((None, 1, tn), lambda l, j: (l, 0, j))],
        out_specs=pl.BlockSpec((None, 8, tn), lambda l, j: (l, 0, j)),
        out_shape=jax.ShapeDtypeStruct((depth, 8, n), F32),
        compiler_params=_params(2),
        name="modulation",
    )(cs, w_mod, b_mod.reshape(depth, 1, n))


def _norm_kernel(x_ref, g_ref, sh_ref, sc_ref, o_ref):
    x = x_ref[...]
    y = x * lax.rsqrt(jnp.mean(x * x, axis=-1, keepdims=True) + EPS)
    y = y * g_ref[...]
    o_ref[...] = (y * (1.0 + sc_ref[0]) + sh_ref[0]).astype(o_ref.dtype)


def _group_of(i, tiles_per_batch, n_batch):
    return jnp.minimum(i // tiles_per_batch, n_batch)


def _norm_mod(x, g, shift, scale, n_rows, seq, n_batch):
    d = x.shape[1]
    tm = ELEMENTWISE_ROWS
    tpb = seq // tm
    grp = lambda i: (_group_of(i, tpb, n_batch), 0, 0)
    return pl.pallas_call(
        _norm_kernel,
        grid=(n_rows // tm,),
        in_specs=[pl.BlockSpec((tm, d), lambda i: (i, 0)),
                  pl.BlockSpec((1, d), lambda i: (0, 0)),
                  pl.BlockSpec((1, 1, d), grp),
                  pl.BlockSpec((1, 1, d), grp)],
        out_specs=pl.BlockSpec((tm, d), lambda i: (i, 0)),
        out_shape=jax.ShapeDtypeStruct((n_rows, d), BF16),
        compiler_params=_params(1),
        name="norm_mod",
    )(x, g.reshape(1, d), shift, scale)


def _proj_in_kernel(blk_ref, shift_ref, a_ref, wa_ref, wb_ref, o_ref, w_scr):
    j = pl.program_id(0)

    @pl.when(pl.program_id(1) == 0)
    def _():
        @pl.when(shift_ref[j] == 0)
        def _():
            w_scr[...] = wa_ref[...].T.astype(BF16)

        @pl.when(shift_ref[j] != 0)
        def _():
            w = jnp.concatenate([wa_ref[W_MISALIGN:, :], wb_ref[...]], axis=0)
            w_scr[...] = w.T.astype(BF16)

    o_ref[...] = jnp.dot(a_ref[...], w_scr[...], preferred_element_type=F32).astype(o_ref.dtype)


def _proj_in(a, w_in_t, layer, src_blocks, shifted, out_dtype):
    m, k = a.shape
    n_src = w_in_t.shape[1]
    tm, tn = m // PROJ_ROW_TILES, COL_TILE
    n_tiles = len(src_blocks)
    last_blk = n_src // W_MISALIGN - 1
    per = tn // W_MISALIGN
    return pl.pallas_call(
        _proj_in_kernel,
        grid_spec=pltpu.PrefetchScalarGridSpec(
            num_scalar_prefetch=2,
            grid=(n_tiles, m // tm),
            in_specs=[pl.BlockSpec((tm, k), lambda j, i, blk, sh: (i, 0)),
                      pl.BlockSpec((None, tn, k), lambda j, i, blk, sh: (layer, blk[j], 0)),
                      pl.BlockSpec((None, W_MISALIGN, k), lambda j, i, blk, sh: (
                          layer, jnp.minimum(per * (blk[j] + 1), last_blk), 0))],
            out_specs=pl.BlockSpec((tm, tn), lambda j, i, blk, sh: (i, j)),
            scratch_shapes=[pltpu.VMEM((k, tn), BF16)]),
        out_shape=jax.ShapeDtypeStruct((m, n_tiles * tn), out_dtype),
        compiler_params=_params(2),
        name="proj_in",
    )(jnp.asarray(src_blocks, jnp.int32), jnp.asarray(shifted, jnp.int32), a, w_in_t, w_in_t)


def _mm_res_kernel(a_ref, w_ref, r_ref, g_ref, o_ref, wb_ref):
    @pl.when(pl.program_id(1) == 0)
    def _():
        wb_ref[...] = w_ref[...].astype(BF16)

    y = jnp.dot(a_ref[...], wb_ref[...], preferred_element_type=F32)
    o_ref[...] = r_ref[...] + g_ref[0] * y


def _matmul_residual(a, w, layer, res, gate, n_rows, seq, n_batch, name):
    k = a.shape[1]
    n = w.shape[2]
    tm, tn = ROW_TILE, COL_TILE
    tpb = seq // tm
    return pl.pallas_call(
        _mm_res_kernel,
        grid=(n // tn, n_rows // tm),
        in_specs=[pl.BlockSpec((tm, k), lambda j, i: (i, 0)),
                  pl.BlockSpec((None, k, tn), lambda j, i: (layer, 0, j)),
                  pl.BlockSpec((tm, tn), lambda j, i: (i, j)),
                  pl.BlockSpec((1, 1, tn), lambda j, i: (_group_of(i, tpb, n_batch), 0, j))],
        out_specs=pl.BlockSpec((tm, tn), lambda j, i: (i, j)),
        out_shape=jax.ShapeDtypeStruct((n_rows, n), F32),
        scratch_shapes=[pltpu.VMEM((k, tn), BF16)],
        compiler_params=_params(2),
        name=name,
    )(a, w, res, gate)


def _glu_kernel(a_ref, wg_ref, wu_ref, o_ref, wgb_ref, wub_ref):
    @pl.when(pl.program_id(1) == 0)
    def _():
        wgb_ref[...] = wg_ref[...].astype(BF16)
        wub_ref[...] = wu_ref[...].astype(BF16)

    a = a_ref[...]
    g = jnp.dot(a, wgb_ref[...], preferred_element_type=F32)
    u = jnp.dot(a, wub_ref[...], preferred_element_type=F32)
    o_ref[...] = (_silu(g) * u).astype(o_ref.dtype)


def _glu(a, wg, wu, layer):
    m, k = a.shape
    n = wg.shape[2]
    tm, tn = m // GLU_ROW_TILES, COL_TILE
    return pl.pallas_call(
        _glu_kernel,
        grid=(n // tn, m // tm),
        in_specs=[pl.BlockSpec((tm, k), lambda j, i: (i, 0)),
                  pl.BlockSpec((None, k, tn), lambda j, i: (layer, 0, j)),
                  pl.BlockSpec((None, k, tn), lambda j, i: (layer, 0, j))],
        out_specs=pl.BlockSpec((tm, tn), lambda j, i: (i, j)),
        out_shape=jax.ShapeDtypeStruct((m, n), BF16),
        scratch_shapes=[pltpu.VMEM((k, tn), BF16), pltpu.VMEM((k, tn), BF16)],
        compiler_params=_params(2),
        name="ffn_up",
    )(a, wg, wu)


def _merge_kernel(*refs, n_lat_tiles, with_ctx):
    if with_ctx:
        lat_refs, ctx_refs, rest = refs[0:3], refs[3:6], refs[6:]
    else:
        lat_refs, ctx_refs, rest = refs[0:3], None, refs[3:]
    w_ref, g0_ref, g1_ref, g2_ref, o_ref, wb_ref = rest

    @pl.when(pl.program_id(1) == 0)
    def _():
        wb_ref[...] = w_ref[...].astype(BF16)

    is_ctx = pl.program_id(1) >= n_lat_tiles
    acc = None
    for b, g_ref in enumerate((g0_ref, g1_ref, g2_ref)):
        y = lat_refs[b][...]
        if with_ctx:
            y = jnp.where(is_ctx, ctx_refs[b][...], y)
        term = jax.nn.sigmoid(g_ref[...].astype(F32)) * jnp.dot(y, wb_ref[b], preferred_element_type=F32)
        acc = term if acc is None else acc + term
    o_ref[...] = acc.astype(o_ref.dtype)


def _branch_merge(ys_lat, ys_ctx, w_branch, layer, proj_h, n_rows):
    k = ys_lat[0].shape[1]
    n = w_branch.shape[3]
    tm, tn = ROW_TILE, COL_TILE
    n_lat_tiles = ys_lat[0].shape[0] // tm
    with_ctx = ys_ctx is not None
    gate_spec = lambda b: pl.BlockSpec((tm, tn), lambda j, i: (i, (H_MERGE + b * n) // tn + j))
    lat_spec = pl.BlockSpec((tm, k), lambda j, i: (jnp.minimum(i, n_lat_tiles - 1), 0))
    ctx_spec = pl.BlockSpec((tm, k), lambda j, i: (jnp.maximum(i - n_lat_tiles, 0), 0))
    in_specs = [lat_spec] * 3 + ([ctx_spec] * 3 if with_ctx else [])
    in_specs += [pl.BlockSpec((None, 3, k, tn), lambda j, i: (layer, 0, 0, j)),
                 gate_spec(0), gate_spec(1), gate_spec(2)]
    args = list(ys_lat) + (list(ys_ctx) if with_ctx else [])
    return pl.pallas_call(
        functools.partial(_merge_kernel, n_lat_tiles=n_lat_tiles, with_ctx=with_ctx),
        grid=(n // tn, n_rows // tm),
        in_specs=in_specs,
        out_specs=pl.BlockSpec((tm, tn), lambda j, i: (i, j)),
        out_shape=jax.ShapeDtypeStruct((n_rows, n), BF16),
        scratch_shapes=[pltpu.VMEM((3, k, tn), BF16)],
        compiler_params=_params(2),
        name="branch_merge",
    )(*args, w_branch, proj_h, proj_h, proj_h)


def _head_rms(x, g):
    return x * lax.rsqrt(jnp.mean(x * x, axis=-1, keepdims=True) + EPS) * g


def _rope(x, cos, sin, low_half):
    partner = jnp.where(low_half, pltpu.roll(x, 96, axis=1), pltpu.roll(x, 32, axis=1))
    return x * cos + partner * sin


def _att_prep_kernel(q_ref, k_ref, cos_ref, sin_ref, qg_ref, kg_ref, qo_ref, ko_ref):
    cos = cos_ref[...]
    sin = sin_ref[...]
    lane = lax.broadcasted_iota(jnp.int32, cos.shape, 1)
    low_half = jnp.bitwise_and(lane, 63) < 32
    scale = HEAD_DIM ** -0.5 * LOG2E
    for h in range(ATT_HEADS):
        sl = slice(h * HEAD_DIM, (h + 1) * HEAD_DIM)
        x = _head_rms(q_ref[:, sl].astype(F32), qg_ref[...])
        qo_ref[:, sl] = (_rope(x, cos, sin, low_half) * scale).astype(qo_ref.dtype)
    for h in range(ATT_KV_HEADS):
        sl = slice(h * HEAD_DIM, (h + 1) * HEAD_DIM)
        x = _head_rms(k_ref[:, sl].astype(F32), kg_ref[...])
        ko_ref[:, sl] = _rope(x, cos, sin, low_half).astype(ko_ref.dtype)


def _att_prep(proj_h, cos, sin, qg, kg):
    m = proj_h.shape[0]
    tm = ROW_TILE
    return pl.pallas_call(
        _att_prep_kernel,
        grid=(m // tm,),
        in_specs=[pl.BlockSpec((tm, ATT_W), lambda i: (i, H_ATT_Q // ATT_W)),
                  pl.BlockSpec((tm, KV_W), lambda i: (i, H_ATT_K // KV_W)),
                  pl.BlockSpec((tm, HEAD_DIM), lambda i: (i, 0)),
                  pl.BlockSpec((tm, HEAD_DIM), lambda i: (i, 0)),
                  pl.BlockSpec((1, HEAD_DIM), lambda i: (0, 0)),
                  pl.BlockSpec((1, HEAD_DIM), lambda i: (0, 0))],
        out_specs=[pl.BlockSpec((tm, ATT_W), lambda i: (i, 0)),
                   pl.BlockSpec((tm, KV_W), lambda i: (i, 0))],
        out_shape=[jax.ShapeDtypeStruct((m, ATT_W), BF16),
                   jax.ShapeDtypeStruct((m, KV_W), BF16)],
        compiler_params=_params(1),
        name="att_prep",
    )(proj_h, proj_h, cos, sin, qg.reshape(1, HEAD_DIM), kg.reshape(1, HEAD_DIM))


def _att_kernel(*refs, with_latent):
    if with_latent:
        q_ref, kc_ref, vc_ref, kl_ref, vl_ref, o_ref = refs
    else:
        q_ref, kc_ref, vc_ref, o_ref = refs
    grp = ATT_HEADS // ATT_KV_HEADS
    for g in range(grp):
        sl = slice(g * HEAD_DIM, (g + 1) * HEAD_DIM)
        q = q_ref[:, sl]
        sc = lax.dot_general(q, kc_ref[...], _NT, preferred_element_type=F32)
        m = jnp.max(sc, axis=-1, keepdims=True)
        if with_latent:
            sl_ = lax.dot_general(q, kl_ref[...], _NT, preferred_element_type=F32)
            m = jnp.maximum(m, jnp.max(sl_, axis=-1, keepdims=True))
        pc = jnp.exp2(sc - m)
        den = jnp.sum(pc, axis=-1, keepdims=True)
        acc = jnp.dot(pc.astype(BF16), vc_ref[...], preferred_element_type=F32)
        if with_latent:
            pl_ = jnp.exp2(sl_ - m)
            den += jnp.sum(pl_, axis=-1, keepdims=True)
            acc += jnp.dot(pl_.astype(BF16), vl_ref[...], preferred_element_type=F32)
        o_ref[:, sl] = (acc / den).astype(o_ref.dtype)


def _attention(q_rot, k_rot, proj_h, n_batch, seq, ctx_len, latent_queries):
    grp_w = ATT_W // ATT_KV_HEADS
    ctx_blk0 = n_batch * seq // ctx_len
    v_col = H_ATT_V // HEAD_DIM
    if latent_queries:
        tq = ATT_TQ
        nq = seq // tq
        q_map = lambda b, k, i: (b * nq + i, k)
        n_out = n_batch * seq
    else:
        tq = ctx_len
        nq = 1
        q_map = lambda b, k, i: (ctx_blk0 + b, k)
        n_out = n_batch * ctx_len
    in_specs = [pl.BlockSpec((tq, grp_w), q_map),
                pl.BlockSpec((ctx_len, HEAD_DIM), lambda b, k, i: (ctx_blk0 + b, k)),
                pl.BlockSpec((ctx_len, HEAD_DIM), lambda b, k, i: (ctx_blk0 + b, v_col + k))]
    args = [q_rot, k_rot, proj_h]
    if latent_queries:
        in_specs += [pl.BlockSpec((seq, HEAD_DIM), lambda b, k, i: (b, k)),
                     pl.BlockSpec((seq, HEAD_DIM), lambda b, k, i: (b, v_col + k))]
        args += [k_rot, proj_h]
    return pl.pallas_call(
        functools.partial(_att_kernel, with_latent=latent_queries),
        grid=(n_batch, ATT_KV_HEADS, nq),
        in_specs=in_specs,
        out_specs=pl.BlockSpec((tq, grp_w), lambda b, k, i: (b * nq + i, k)),
        out_shape=jax.ShapeDtypeStruct((n_out, ATT_W), BF16),
        compiler_params=_params(3),
        name="attention_lat" if latent_queries else "attention_ctx",
    )(*args)


def _conv3(x_ref, w, c, n_chunks, chunk):
    r0 = pl.multiple_of(c * chunk, chunk)
    x = x_ref[pl.ds(r0, chunk), :].astype(F32)
    prev_blk = x_ref[pl.ds(pl.multiple_of(jnp.maximum(r0 - 16, 0), 16), 16), :].astype(F32)
    next_blk = x_ref[pl.ds(pl.multiple_of(jnp.minimum(r0 + chunk, (n_chunks - 1) * chunk), 16), 16), :].astype(F32)
    prev_row = jnp.where(c > 0, prev_blk[15:16, :], 0.0)
    next_row = jnp.where(c < n_chunks - 1, next_blk[0:1, :], 0.0)
    row = lax.broadcasted_iota(jnp.int32, x.shape, 0)
    x_m1 = jnp.where(row == 0, prev_row, pltpu.roll(x, 1, axis=0))
    x_p1 = jnp.where(row == chunk - 1, next_row, pltpu.roll(x, chunk - 1, axis=0))
    return x_m1 * w[0:1, :] + x * w[1:2, :] + x_p1 * w[2:3, :]


def _ml_chunks(chains):
    n_tok = chains[0][0].shape[0]
    ti = lax.broadcasted_iota(jnp.int32, (n_tok, n_tok), 0)
    si = lax.broadcasted_iota(jnp.int32, (n_tok, n_tok), 1)
    gate = []
    for q, k, v, lfr, lir, lfc, lic, c_ref, n_ref, m_st, rev in chains:
        causal = (si >= ti) if rev else (si <= ti)
        causal_t = (ti >= si) if rev else (ti <= si)
        fc_col = jnp.sum(jnp.where(causal, lfr, 0.0), axis=1, keepdims=True)
        fc_row = jnp.sum(jnp.where(causal_t, lfc, 0.0), axis=0, keepdims=True)
        f_tot = jnp.sum(lfr, axis=1, keepdims=True)
        dmat = jnp.where(causal, fc_col - fc_row + lir, -jnp.inf)
        inter = fc_col + m_st
        m_t = jnp.maximum(inter, jnp.max(dmat, axis=1, keepdims=True))
        w = jnp.exp(dmat - m_t)
        a = jnp.exp(inter - m_t)
        g_col = f_tot - fc_col + lic
        g_row = f_tot - fc_row + lir
        m_new = jnp.maximum(f_tot + m_st, jnp.max(g_row, axis=1, keepdims=True))
        decay = jnp.exp(f_tot + m_st - m_new)
        kw = k.astype(F32) * jnp.exp(g_col - m_new)
        gate.append((w, a, m_t, m_new, decay, kw))
    scs = [lax.dot_general(ch[0], ch[1], _NT, preferred_element_type=F32) * g[0]
           for ch, g in zip(chains, gate)]
    nums = [jnp.dot(sc.astype(BF16), ch[2], preferred_element_type=F32) for ch, sc in zip(chains, scs)]
    carried = [jnp.dot(ch[0], ch[7][...].astype(BF16), preferred_element_type=F32) for ch in chains]
    hs = []
    for ch, g, sc, num, car in zip(chains, gate, scs, nums, carried):
        q, n_ref = ch[0], ch[8]
        w, a, m_t, m_new, decay, kw = g
        den = jnp.sum(sc, axis=1, keepdims=True)
        den += a * jnp.sum(q.astype(F32) * n_ref[...], axis=1, keepdims=True)
        hs.append((num + a * car) / jnp.maximum(jnp.abs(den), jnp.exp(-m_t)))
    for ch, g in zip(chains, gate):
        v, c_ref, n_ref = ch[2], ch[7], ch[8]
        w, a, m_t, m_new, decay, kw = g
        c_ref[...] = decay * c_ref[...] + lax.dot_general(kw.astype(BF16), v, _TN, preferred_element_type=F32)
        n_ref[...] = decay * n_ref[...] + jnp.sum(kw, axis=0, keepdims=True)
    return hs, [g[3] for g in gate]


def _ml_kernel(*refs, seq, ctx_len, need_ctx):
    (ql_ref, kl_ref, vl_ref, ol_ref, qc_ref, kc_ref, vc_ref, oc_ref,
     wq_ref, wk_ref, gcl_ref, grl_ref, gcc_ref, grc_ref, bc_ref, br_ref, ng_ref) = refs[:17]
    if need_ctx:
        yl_ref, yc_ref = refs[17:19]
        scratch = refs[19:]
    else:
        yl_ref = refs[17]
        yc_ref = None
        scratch = refs[18:]
    qsl_ref, ksl_ref, qsc_ref, ksc_ref, hl_ref, hc_ref, cf_ref, nf_ref, cb_ref, nb_ref = scratch
    chunk = ML_CHUNK
    k_scale = ML_DK ** -0.5
    wq = wq_ref[0]
    wk = wk_ref[0]

    def conv_pass(src_q, src_k, dst_q, dst_k, n_chunks):
        def body(c, carry):
            r0 = pl.multiple_of(c * chunk, chunk)
            dst_q[pl.ds(r0, chunk), :] = _conv3(src_q, wq, c, n_chunks, chunk).astype(BF16)
            dst_k[pl.ds(r0, chunk), :] = (_conv3(src_k, wk, c, n_chunks, chunk) * k_scale).astype(BF16)
            return carry
        lax.fori_loop(0, n_chunks, body, 0)

    conv_pass(qc_ref, kc_ref, qsc_ref, ksc_ref, ctx_len // chunk)
    conv_pass(ql_ref, kl_ref, qsl_ref, ksl_ref, seq // chunk)

    bias_c = bc_ref[0]
    bias_r = br_ref[0]
    norm_g = ng_ref[0]

    def run(q_s, k_s, v_s, o_s, gc_s, gr_s, h_s, y_s, n_chunks, m_states):
        def chain(c, rev, c_r, n_r, m_st):
            d = 1 if rev else 0
            rows = pl.ds(pl.multiple_of(c * chunk, chunk), chunk)
            gc = gc_s[0, rows, :] + bias_c
            gr = gr_s[0, :, rows] + bias_r
            lic = gc[:, d:d + 1]
            lfc = _log_sigmoid(gc[:, 2 + d:3 + d])
            lir = gr[d:d + 1, :]
            lfr = _log_sigmoid(gr[2 + d:3 + d, :])
            return rows, (q_s[rows, :], k_s[rows, :], v_s[rows, :], lfr, lir, lfc, lic, c_r, n_r, m_st, rev)

        def finish(rows, h_sum):
            if y_s is not None:
                y = _head_rms(h_sum, norm_g) * jax.nn.sigmoid(o_s[rows, :].astype(F32))
                y_s[rows, :] = y.astype(y_s.dtype)

        def step(ci, m_states, phase):
            rows_f, chain_f = chain(ci, False, cf_ref, nf_ref, m_states[0])
            rows_b, chain_b = chain(n_chunks - 1 - ci, True, cb_ref, nb_ref, m_states[1])
            (h_f, h_b), m_new = _ml_chunks([chain_f, chain_b])
            if phase == 0:
                h_s[rows_f, :] = h_f
                h_s[rows_b, :] = h_b
            elif phase == 1:
                finish(rows_f, h_f + h_b)
            else:
                finish(rows_f, h_s[rows_f, :] + h_f)
                finish(rows_b, h_s[rows_b, :] + h_b)
            return tuple(m_new)

        half = n_chunks // 2
        m_states = lax.fori_loop(0, half, lambda ci, m: step(ci, m, 0), m_states)
        if n_chunks % 2:
            m_states = step(half, m_states, 1)
        return lax.fori_loop(n_chunks - half, n_chunks, lambda ci, m: step(ci, m, 2), m_states)

    for ref in (cf_ref, nf_ref, cb_ref, nb_ref):
        ref[...] = jnp.zeros_like(ref)
    m0 = jnp.full((1, 1), NEG_INIT, F32)
    m1 = run(qsc_ref, ksc_ref, vc_ref, oc_ref, gcc_ref, grc_ref, hc_ref, yc_ref, ctx_len // chunk, (m0, m0))
    run(qsl_ref, ksl_ref, vl_ref, ol_ref, gcl_ref, grl_ref, hl_ref, yl_ref, seq // chunk, m1)


def _mlstm(proj_h, gates_c, gates_r, conv_w, gate_b, norm_g, n_batch, seq, ctx_len, need_ctx):
    w = ML_DK
    cb0 = n_batch * seq // ctx_len
    col = lambda off: off // w
    lat = lambda off: pl.BlockSpec((seq, w), lambda b, h: (b, col(off) + h))
    ctx = lambda off: pl.BlockSpec((ctx_len, w), lambda b, h: (cb0 + b, col(off) + h))
    conv = conv_w.reshape(3, 2 * ML_HEADS, w).transpose(1, 0, 2)
    gb = gate_b.reshape(4, ML_HEADS).T
    in_specs = [lat(H_ML_Q), lat(H_ML_K), lat(H_ML_V), lat(H_ML_O),
                ctx(H_ML_Q), ctx(H_ML_K), ctx(H_ML_V), ctx(H_ML_O),
                pl.BlockSpec((1, 3, w), lambda b, h: (h, 0, 0)),
                pl.BlockSpec((1, 3, w), lambda b, h: (ML_HEADS + h, 0, 0)),
                pl.BlockSpec((1, seq, 4), lambda b, h: (h, b, 0)),
                pl.BlockSpec((1, 4, seq), lambda b, h: (h, 0, b)),
                pl.BlockSpec((1, ctx_len, 4), lambda b, h: (h, cb0 + b, 0)),
                pl.BlockSpec((1, 4, ctx_len), lambda b, h: (h, 0, cb0 + b)),
                pl.BlockSpec((1, 1, 4), lambda b, h: (h, 0, 0)),
                pl.BlockSpec((1, 4, 1), lambda b, h: (h, 0, 0)),
                pl.BlockSpec((1, 1, w), lambda b, h: (h, 0, 0))]
    out_specs = [pl.BlockSpec((seq, w), lambda b, h: (b, h))]
    out_shape = [jax.ShapeDtypeStruct((n_batch * seq, ML_W), BF16)]
    if need_ctx:
        out_specs.append(pl.BlockSpec((ctx_len, w), lambda b, h: (b, h)))
        out_shape.append(jax.ShapeDtypeStruct((n_batch * ctx_len, ML_W), BF16))
    outs = pl.pallas_call(
        functools.partial(_ml_kernel, seq=seq, ctx_len=ctx_len, need_ctx=need_ctx),
        grid=(n_batch, ML_HEADS),
        in_specs=in_specs,
        out_specs=out_specs,
        out_shape=out_shape,
        scratch_shapes=[pltpu.VMEM((seq, w), BF16), pltpu.VMEM((seq, w), BF16),
                        pltpu.VMEM((ctx_len, w), BF16), pltpu.VMEM((ctx_len, w), BF16),
                        pltpu.VMEM((seq, w), F32), pltpu.VMEM((ctx_len, w), F32),
                        pltpu.VMEM((w, w), F32), pltpu.VMEM((1, w), F32),
                        pltpu.VMEM((w, w), F32), pltpu.VMEM((1, w), F32)],
        compiler_params=_params(2),
        name="mlstm",
    )(proj_h, proj_h, proj_h, proj_h, proj_h, proj_h, proj_h, proj_h,
      conv, conv, gates_c, gates_r, gates_c, gates_r,
      gb.reshape(ML_HEADS, 1, 4), gb.reshape(ML_HEADS, 4, 1), norm_g.reshape(ML_HEADS, 1, w))
    return outs if need_ctx else (outs[0], None)


def _hg_level_tables(chunk, rev):
    levels = []
    n = HG_LEAF
    while 2 * n <= chunk:
        refs = []
        for start in range(0, chunk, 2 * n):
            refs.append((start + n - 1) if rev else (start + n))
        levels.append((n, refs))
        n *= 2
    return levels


def _tri_cumsum(tri, x):
    hi = x.astype(BF16)
    rest = x - hi.astype(F32)
    mid = rest.astype(BF16)
    lo = (rest - mid.astype(F32)).astype(BF16)
    out = jnp.dot(tri, hi, preferred_element_type=F32)
    out += jnp.dot(tri, mid, preferred_element_type=F32)
    return out + jnp.dot(tri, lo, preferred_element_type=F32)


def _hg_chunks(chains, loglb, log1mlb, one_m_lb):
    chunk = chains[0][1].shape[0]
    nb = chunk // HG_LEAF

    qs, ks, lks, gcums = [], [], [], []
    for qraw, z, v, st_ref, masks, tri, rev in chains:
        e = jnp.exp(-jnp.abs(z))
        one_pe = 1.0 + e
        lsig = jnp.minimum(z, 0.0) - jnp.log(one_pe)
        a = log1mlb + lsig
        lf = jnp.maximum(loglb, a) + jnp.log(1.0 + jnp.exp(-jnp.abs(loglb - a)))
        ks.append(one_m_lb * (jnp.where(z >= 0.0, e, 1.0) / one_pe))
        lks.append((a - z) * LOG2E)
        gcums.append(_tri_cumsum(tri, lf * LOG2E))
        x = qraw.astype(F32)
        qs.append(x / (1.0 + jnp.exp(-x)))

    atts = [jnp.zeros((chunk, chunk), F32) for _ in chains]
    for lvl, (n, _) in enumerate(_hg_level_tables(chunk, False)):
        for c, (qraw, z, v, st_ref, masks, tri, rev) in enumerate(chains):
            refs = _hg_level_tables(chunk, rev)[lvl][1]
            gref = jnp.concatenate(
                [jnp.broadcast_to(gcums[c][r:r + 1, :], (2 * n, HG_DK)) for r in refs], axis=0)
            dec = jnp.exp2(-jnp.abs(gcums[c] - gref))
            mixed = jnp.concatenate(
                [(qs[c] if ((r0 % (2 * n)) >= n) != rev else ks[c])[r0:r0 + HG_LEAF, :]
                 for r0 in range(0, chunk, HG_LEAF)], axis=0)
            md = (mixed * dec).astype(BF16)
            p = lax.dot_general(md, md, _NT, preferred_element_type=F32)
            atts[c] += jnp.where(masks[0][lvl], p, 0.0)

    inters = []
    for c, (qraw, z, v, st_ref, masks, tri, rev) in enumerate(chains):
        g_tot = gcums[c][0:1, :] if rev else gcums[c][chunk - 1:chunk, :]
        q_dec = (qs[c] * jnp.exp2(gcums[c])).astype(BF16)
        k_dec = (ks[c] * jnp.exp2(g_tot - gcums[c])).astype(BF16)
        st = st_ref[...]
        inters.append(lax.dot_general(q_dec, st.astype(BF16), _NT, preferred_element_type=F32))
        st_ref[...] = st * jnp.exp2(g_tot) + lax.dot_general(v, k_dec, _TN, preferred_element_type=F32)

    outs = []
    for c, (qraw, z, v, st_ref, masks, tri, rev) in enumerate(chains):
        g3 = gcums[c].reshape(nb, HG_LEAF, HG_DK)
        h3 = (lks[c] - gcums[c]).reshape(nb, HG_LEAF, HG_DK)
        q3 = qs[c].reshape(nb, HG_LEAF, HG_DK)
        att = atts[c]
        for s in range(HG_LEAF):
            col = jnp.sum(q3 * jnp.exp2(g3 + h3[:, s:s + 1, :]), axis=2, keepdims=True)
            att = jnp.where(masks[1][s], col.reshape(chunk, 1), att)
        outs.append(inters[c] + jnp.dot(att.astype(BF16), v, preferred_element_type=F32))
    return outs


def _hg_kernel(*refs, seq, ctx_len, need_ctx):
    (ql_ref, il_ref, gl_ref, ffl_ref, fbl_ref, qc_ref, ic_ref, gc_ref, ffc_ref, fbc_ref,
     llb_ref, l1m_ref, oml_ref, ng_ref) = refs[:14]
    if need_ctx:
        yl_ref, yc_ref = refs[14:16]
        scratch = refs[16:]
    else:
        yl_ref = refs[14]
        yc_ref = None
        scratch = refs[15:]
    ol_ref, oc_ref, stf_ref, stb_ref = scratch
    chunk = HG_CHUNK
    loglb = llb_ref[0]
    log1mlb = l1m_ref[0]
    one_m_lb = oml_ref[0]
    norm_g = ng_ref[0]
    ti = lax.broadcasted_iota(jnp.int32, (chunk, chunk), 0)
    si = lax.broadcasted_iota(jnp.int32, (chunk, chunk), 1)

    def direction_consts(rev):
        tri = jnp.where((si >= ti) if rev else (si <= ti), 1.0, 0.0).astype(BF16)
        masks = []
        for n, _ in _hg_level_tables(chunk, rev):
            shift = (2 * n).bit_length() - 1
            same = jnp.right_shift(ti, shift) == jnp.right_shift(si, shift)
            t_late = jnp.bitwise_and(ti, 2 * n - 1) >= n
            s_late = jnp.bitwise_and(si, 2 * n - 1) >= n
            if rev:
                masks.append(same & jnp.logical_not(t_late) & s_late)
            else:
                masks.append(same & t_late & jnp.logical_not(s_late))
        t_in = jnp.bitwise_and(ti, HG_LEAF - 1)
        leaf_masks = []
        for s in range(HG_LEAF):
            at_col = si == (ti - t_in + s)
            leaf_masks.append(at_col & ((t_in <= s) if rev else (t_in >= s)))
        return tri, (masks, leaf_masks)

    tri_f, masks_f = direction_consts(False)
    tri_b, masks_b = direction_consts(True)

    def run(q_s, i_s, g_s, zf_s, zb_s, o_s, y_s, n_chunks):
        def step(pi, second):
            rows = []
            chains = []
            for u in range(HG_UNROLL):
                c = pi * HG_UNROLL + u
                r_f = pl.ds(pl.multiple_of(c * chunk, chunk), chunk)
                rows.append(r_f)
                chains.append((q_s[r_f, :], zf_s[r_f, :], i_s[r_f, :], stf_ref, masks_f, tri_f, False))
            for u in range(HG_UNROLL):
                c = n_chunks - 1 - (pi * HG_UNROLL + u)
                r_b = pl.ds(pl.multiple_of(c * chunk, chunk), chunk)
                rows.append(r_b)
                chains.append((q_s[r_b, :], zb_s[r_b, :], i_s[r_b, :], stb_ref, masks_b, tri_b, True))
            outs = _hg_chunks(chains, loglb, log1mlb, one_m_lb)
            for r, o in zip(rows, outs):
                if not second:
                    o_s[r, :] = o
                elif y_s is not None:
                    y = _head_rms(o_s[r, :] + o, norm_g) * _silu(g_s[r, :].astype(F32))
                    y_s[r, :] = y.astype(y_s.dtype)

        def first(pi, carry):
            step(pi, False)
            return carry

        def second(pi, carry):
            step(pi, True)
            return carry

        n_steps = n_chunks // HG_UNROLL
        lax.fori_loop(0, n_steps // 2, first, 0)
        lax.fori_loop(n_steps // 2, n_steps, second, 0)

    stf_ref[...] = jnp.zeros_like(stf_ref)
    stb_ref[...] = jnp.zeros_like(stb_ref)
    run(qc_ref, ic_ref, gc_ref, ffc_ref, fbc_ref, oc_ref, yc_ref, ctx_len // chunk)
    run(ql_ref, il_ref, gl_ref, ffl_ref, fbl_ref, ol_ref, yl_ref, seq // chunk)


def _hgrn2(proj_h, proj_f, lb, norm_g, n_batch, seq, ctx_len, need_ctx):
    w = HG_DK
    cb0 = n_batch * seq // ctx_len
    lat = lambda off: pl.BlockSpec((seq, w), lambda b, h: (b, off // w + h))
    ctx = lambda off: pl.BlockSpec((ctx_len, w), lambda b, h: (cb0 + b, off // w + h))
    vec = pl.BlockSpec((1, 1, w), lambda b, h: (h, 0, 0))
    lb3 = lb.reshape(HG_HEADS, 1, w)
    in_specs = [lat(H_HG_Q), lat(H_HG_I), lat(H_HG_G), lat(F_HG_FF), lat(F_HG_FB),
                ctx(H_HG_Q), ctx(H_HG_I), ctx(H_HG_G), ctx(F_HG_FF), ctx(F_HG_FB),
                vec, vec, vec, vec]
    out_specs = [pl.BlockSpec((seq, w), lambda b, h: (b, h))]
    out_shape = [jax.ShapeDtypeStruct((n_batch * seq, HG_W), BF16)]
    if need_ctx:
        out_specs.append(pl.BlockSpec((ctx_len, w), lambda b, h: (b, h)))
        out_shape.append(jax.ShapeDtypeStruct((n_batch * ctx_len, HG_W), BF16))
    outs = pl.pallas_call(
        functools.partial(_hg_kernel, seq=seq, ctx_len=ctx_len, need_ctx=need_ctx),
        grid=(n_batch, HG_HEADS),
        in_specs=in_specs,
        out_specs=out_specs,
        out_shape=out_shape,
        scratch_shapes=[pltpu.VMEM((seq, w), F32), pltpu.VMEM((ctx_len, w), F32),
                        pltpu.VMEM((w, w), F32), pltpu.VMEM((w, w), F32)],
        compiler_params=_params(2),
        name="hgrn2",
    )(proj_h, proj_h, proj_h, proj_f, proj_f, proj_h, proj_h, proj_h, proj_f, proj_f,
      jnp.log(lb3), jnp.log1p(-lb3), 1.0 - lb3, norm_g.reshape(HG_HEADS, 1, w))
    return outs if need_ctx else (outs[0], None)


def _router_kernel(x_ref, g_ref, sh_ref, sc_ref, wr_ref, h_ref, idx_ref, wt_ref, cnt_ref, carry_ref):
    @pl.when(pl.program_id(0) == 0)
    def _():
        carry_ref[...] = jnp.zeros_like(carry_ref)

    x = x_ref[...]
    y = x * lax.rsqrt(jnp.mean(x * x, axis=-1, keepdims=True) + EPS) * g_ref[...]
    h = y * (1.0 + sc_ref[0]) + sh_ref[0]
    h_ref[...] = h.astype(h_ref.dtype)
    logits = jnp.dot(h, wr_ref[...], precision=lax.Precision.HIGHEST, preferred_element_type=F32)
    lane = lax.broadcasted_iota(jnp.int32, logits.shape, 1).astype(F32)
    lg = jnp.where(lane < N_EXPERTS, logits, -jnp.inf)
    m1 = jnp.max(lg, axis=-1, keepdims=True)
    i1 = jnp.min(jnp.where(lg == m1, lane, float(LANE)), axis=-1, keepdims=True)
    lg2 = jnp.where(lane == i1, -jnp.inf, lg)
    m2 = jnp.max(lg2, axis=-1, keepdims=True)
    i2 = jnp.min(jnp.where(lg2 == m2, lane, float(LANE)), axis=-1, keepdims=True)
    e = jnp.exp(m2 - m1)
    w1 = 1.0 / (1.0 + e)
    w2 = e / (1.0 + e)
    wt_ref[...] = jnp.where(lane == 0, w1, jnp.where(lane == 1, w2, 0.0))
    n_tok = x.shape[0]
    routed = jnp.where(jnp.logical_or(lane == i1, lane == i2), 1.0, 0.0)
    ti = lax.broadcasted_iota(jnp.int32, (n_tok, n_tok), 0)
    ui = lax.broadcasted_iota(jnp.int32, (n_tok, n_tok), 1)
    earlier = jnp.where(ui < ti, 1.0, 0.0).astype(BF16)
    before = carry_ref[...] + jnp.dot(earlier, routed.astype(BF16), preferred_element_type=F32)
    r1 = jnp.sum(jnp.where(lane == i1, before, 0.0), axis=-1, keepdims=True)
    r2 = jnp.sum(jnp.where(lane == i2, before, 0.0), axis=-1, keepdims=True)
    carry_ref[...] += jnp.sum(routed, axis=0, keepdims=True)
    cnt_ref[...] = jnp.broadcast_to(carry_ref[...], cnt_ref.shape).astype(jnp.int32)
    idx_ref[...] = jnp.where(lane == 0, i1, jnp.where(lane == 1, i2, jnp.where(
        lane == 2, r1, jnp.where(lane == 3, r2, 0.0)))).astype(jnp.int32)


def _router(x, g, shift, scale, w_router, n_rows, seq, n_batch):
    d = x.shape[1]
    tm = ELEMENTWISE_ROWS
    tpb = seq // tm
    grp = lambda i: (_group_of(i, tpb, n_batch), 0, 0)
    wr = jnp.pad(w_router, ((0, 0), (0, LANE - N_EXPERTS)))
    return pl.pallas_call(
        _router_kernel,
        grid=(n_rows // tm,),
        in_specs=[pl.BlockSpec((tm, d), lambda i: (i, 0)),
                  pl.BlockSpec((1, d), lambda i: (0, 0)),
                  pl.BlockSpec((1, 1, d), grp),
                  pl.BlockSpec((1, 1, d), grp),
                  pl.BlockSpec((d, LANE), lambda i: (0, 0))],
        out_specs=[pl.BlockSpec((tm, d), lambda i: (i, 0)),
                   pl.BlockSpec((tm, LANE), lambda i: (i, 0)),
                   pl.BlockSpec((tm, LANE), lambda i: (i, 0)),
                   pl.BlockSpec((8, LANE), lambda i: (0, 0))],
        out_shape=[jax.ShapeDtypeStruct((n_rows, d), F32),
                   jax.ShapeDtypeStruct((n_rows, LANE), jnp.int32),
                   jax.ShapeDtypeStruct((n_rows, LANE), F32),
                   jax.ShapeDtypeStruct((8, LANE), jnp.int32)],
        scratch_shapes=[pltpu.VMEM((1, LANE), F32)],
        compiler_params=_params(1),
        name="router",
    )(x, g.reshape(1, d), shift, scale, wr)


def _row_copy(src_hbm, buf_ref, sem, src_row, r):
    return pltpu.make_async_copy(src_hbm.at[pl.ds(src_row, 1)], buf_ref.at[pl.ds(r, 1)], sem)


def _dispatch_kernel(src_ref, nu_ref, h_hbm, o_ref, buf_ref, sem):
    t = pl.program_id(0)
    tm = buf_ref.shape[1]
    slot = t % 2

    def tile_copies(tile, slot, fn):
        def body(r, carry):
            fn(_row_copy(h_hbm, buf_ref.at[slot], sem.at[slot], src_ref[tile * tm + r], r))
            return carry
        lax.fori_loop(0, tm, body, 0, unroll=8)

    @pl.when(jnp.logical_and(t == 0, nu_ref[0] > 0))
    def _():
        tile_copies(0, 0, lambda cp: cp.start())

    @pl.when(t + 1 < nu_ref[0])
    def _():
        tile_copies(t + 1, 1 - slot, lambda cp: cp.start())

    @pl.when(t < nu_ref[0])
    def _():
        tile_copies(t, slot, lambda cp: cp.wait())
        o_ref[...] = buf_ref[slot].astype(o_ref.dtype)

    @pl.when(t >= nu_ref[0])
    def _():
        o_ref[...] = jnp.zeros_like(o_ref)


def _dispatch(h, src_token, n_used):
    d = h.shape[1]
    tm = MOE_TILE
    n_pad = src_token.shape[0]
    return pl.pallas_call(
        _dispatch_kernel,
        grid_spec=pltpu.PrefetchScalarGridSpec(
            num_scalar_prefetch=2,
            grid=(n_pad // tm,),
            in_specs=[pl.BlockSpec(memory_space=pl.ANY)],
            out_specs=pl.BlockSpec((tm, d), lambda t, src, nu: (t, 0)),
            scratch_shapes=[pltpu.VMEM((2, tm, d), F32), pltpu.SemaphoreType.DMA((2,))]),
        out_shape=jax.ShapeDtypeStruct((n_pad, d), BF16),
        compiler_params=_params(1),
        name="moe_dispatch",
    )(src_token, n_used, h)


def _new_expert(te_ref, i):
    return jnp.logical_or(i == 0, te_ref[i] != te_ref[jnp.maximum(i - 1, 0)])


def _moe_kernel(te_ref, nu_ref, nxt_ref, valid_ref, a_ref, *refs, n_w, glu):
    w_hbm = refs[:n_w]
    o_ref = refs[n_w]
    wbuf, wbf, sem, slot_ref = refs[n_w + 1:]
    j, i = pl.program_id(0), pl.program_id(1)
    n_j, n_i = pl.num_programs(0), pl.num_programs(1)
    tn = o_ref.shape[1]

    def weight_copies(slot, expert, col_tile):
        cols = pl.ds(pl.multiple_of(col_tile * tn, tn), tn)
        return [pltpu.make_async_copy(w_hbm[x].at[expert, :, cols], wbuf.at[slot, x], sem.at[slot, x])
                for x in range(n_w)]

    @pl.when(jnp.logical_and(j == 0, i == 0))
    def _():
        slot_ref[0] = 0
        for cp in weight_copies(0, te_ref[0], 0):
            cp.start()

    @pl.when(_new_expert(te_ref, i))
    def _():
        slot = slot_ref[0]
        for cp in weight_copies(slot, te_ref[i], j):
            cp.wait()
        nxt = nxt_ref[i]
        same_col = nxt < n_i

        @pl.when(jnp.logical_or(same_col, j + 1 < n_j))
        def _():
            nxt_expert = te_ref[jnp.where(same_col, nxt, 0)]
            for cp in weight_copies(1 - slot, nxt_expert, jnp.where(same_col, j, j + 1)):
                cp.start()

        for x in range(n_w):
            wbf[x] = wbuf[slot, x].astype(BF16)
        slot_ref[0] = 1 - slot

    def compute(rows):
        a = a_ref[rows, :]
        if glu:
            g = jnp.dot(a, wbf[0], preferred_element_type=F32)
            u = jnp.dot(a, wbf[1], preferred_element_type=F32)
            o_ref[rows, :] = (_silu(g) * u).astype(o_ref.dtype)
        else:
            o_ref[rows, :] = jnp.dot(a, wbf[0], preferred_element_type=F32).astype(o_ref.dtype)

    tm = o_ref.shape[0]
    valid = valid_ref[i]

    @pl.when(valid == tm)
    def _():
        compute(slice(None))

    @pl.when(valid < tm)
    def _():
        for r in range(tm // MOE_SUB_ROWS):
            rows = slice(r * MOE_SUB_ROWS, (r + 1) * MOE_SUB_ROWS)

            @pl.when(r * MOE_SUB_ROWS < valid)
            def _():
                compute(rows)

            @pl.when(r * MOE_SUB_ROWS >= valid)
            def _():
                o_ref[rows, :] = jnp.zeros((MOE_SUB_ROWS, o_ref.shape[1]), o_ref.dtype)


def _moe_grouped(a, weights, tile_expert, n_used, next_run, tile_valid, glu, out_dtype, name):
    n_rows, k = a.shape
    n = weights[0].shape[2]
    n_w = len(weights)
    tm, tn = MOE_TILE, COL_TILE
    a_map = lambda j, i, te, nu, nxt, tv: (jnp.maximum(jnp.minimum(i, nu[0] - 1), 0), 0)
    return pl.pallas_call(
        functools.partial(_moe_kernel, n_w=n_w, glu=glu),
        grid_spec=pltpu.PrefetchScalarGridSpec(
            num_scalar_prefetch=4,
            grid=(n // tn, n_rows // tm),
            in_specs=[pl.BlockSpec((tm, k), a_map)] + [pl.BlockSpec(memory_space=pl.ANY)] * n_w,
            out_specs=pl.BlockSpec((tm, tn), lambda j, i, te, nu, nxt, tv: (i, j)),
            scratch_shapes=[pltpu.VMEM((2, n_w, k, tn), F32), pltpu.VMEM((n_w, k, tn), BF16),
                            pltpu.SemaphoreType.DMA((2, n_w)), pltpu.SMEM((1,), jnp.int32)]),
        out_shape=jax.ShapeDtypeStruct((n_rows, n), out_dtype),
        compiler_params=_params(2),
        name=name,
    )(tile_expert, n_used, next_run, tile_valid, a, *weights)


def _combine_kernel(x_ref, ya_ref, yb_ref, wt_ref, g_ref, o_ref):
    wt = wt_ref[...]
    y = wt[:, 0:1] * ya_ref[...].astype(F32) + wt[:, 1:2] * yb_ref[...].astype(F32)
    o_ref[...] = x_ref[...] + g_ref[0] * y


def _moe_combine(x, ya, yb, wt, gate, n_rows, seq, n_batch):
    d = x.shape[1]
    tm = ELEMENTWISE_ROWS
    tpb = seq // tm
    row = pl.BlockSpec((tm, d), lambda i: (i, 0))
    return pl.pallas_call(
        _combine_kernel,
        grid=(n_rows // tm,),
        in_specs=[row, row, row,
                  pl.BlockSpec((tm, LANE), lambda i: (i, 0)),
                  pl.BlockSpec((1, 1, d), lambda i: (_group_of(i, tpb, n_batch), 0, 0))],
        out_specs=row,
        out_shape=jax.ShapeDtypeStruct((n_rows, d), F32),
        compiler_params=_params(1),
        name="moe_combine",
    )(x, ya, yb, wt, gate)


def _moe_ffn(x, g, shift, scale, gate, w_router, wg, wu, wd, n_rows, seq, n_batch):
    h, idx, wt, cnt = _router(x, g, shift, scale, w_router, n_rows, seq, n_batch)
    tm = MOE_TILE
    n_tiles = 2 * n_rows // tm + N_EXPERTS
    counts = cnt[0, :N_EXPERTS]
    tiles = (counts + tm - 1) // tm
    tile_end = jnp.cumsum(tiles)
    tile_start = tile_end - tiles
    expert = jnp.concatenate([idx[:, 0], idx[:, 1]])
    rank = jnp.concatenate([idx[:, 2], idx[:, 3]])
    token = jnp.concatenate([jnp.arange(n_rows, dtype=jnp.int32)] * 2)
    one_hot = (expert[:, None] == jnp.arange(N_EXPERTS, dtype=jnp.int32)[None, :]).astype(jnp.int32)
    dest = jnp.sum(tile_start[None, :] * one_hot, axis=1) * tm + rank
    src_token = jnp.zeros((n_tiles * tm,), jnp.int32).at[dest].set(token)
    n_used = tile_end[-1:].astype(jnp.int32)
    tile_id = jnp.minimum(jnp.arange(n_tiles, dtype=jnp.int32), n_used[0] - 1)
    tile_expert = jnp.sum((tile_end[None, :] <= tile_id[:, None]).astype(jnp.int32), axis=1)

    a_sorted = _dispatch(h, src_token, n_used)
    later = jnp.arange(n_tiles, dtype=jnp.int32)
    other = jnp.logical_and(later[None, :] > later[:, None], tile_expert[None, :] != tile_expert[:, None])
    next_run = jnp.min(jnp.where(other, later[None, :], n_tiles), axis=1).astype(jnp.int32)
    sel = (tile_expert[:, None] == jnp.arange(N_EXPERTS, dtype=jnp.int32)[None, :]).astype(jnp.int32)
    rows_left = jnp.sum(sel * counts[None, :], axis=1) - (later - jnp.sum(sel * tile_start[None, :], axis=1)) * tm
    tile_valid = jnp.where(later < n_used[0], jnp.clip(rows_left, 0, tm), 0).astype(jnp.int32)
    u = _moe_grouped(a_sorted, (wg, wu), tile_expert, n_used, next_run, tile_valid, True, BF16, "moe_up")
    y = _moe_grouped(u, (wd,), tile_expert, n_used, next_run, tile_valid, False, BF16, "moe_down")
    ya = jnp.take(y, dest[:n_rows], axis=0, mode="clip")
    yb = jnp.take(y, dest[n_rows:], axis=0, mode="clip")
    return _moe_combine(x, ya, yb, wt, gate, n_rows, seq, n_batch)


def _rope_tables(n_batch, seq, n_ctx_rows):
    t = jnp.arange(seq)
    pos = jnp.stack([t // GRID_W, t % GRID_W], axis=-1).astype(F32)
    inv_freq = jnp.exp(-jnp.log(ROPE_BASE) * jnp.arange(0, ROPE_AXIS_DIM, 2, dtype=F32) / ROPE_AXIS_DIM)
    ang = pos[..., None] * inv_freq
    cos, sin = jnp.cos(ang), jnp.sin(ang)
    cos = jnp.concatenate([cos[:, 0], cos[:, 0], cos[:, 1], cos[:, 1]], axis=-1)
    sin = jnp.concatenate([-sin[:, 0], sin[:, 0], -sin[:, 1], sin[:, 1]], axis=-1)
    cos = jnp.concatenate([jnp.tile(cos, (n_batch, 1)), jnp.ones((n_ctx_rows, HEAD_DIM), F32)], axis=0)
    sin = jnp.concatenate([jnp.tile(sin, (n_batch, 1)), jnp.zeros((n_ctx_rows, HEAD_DIM), F32)], axis=0)
    return cos, sin


def kernel(x, c, ctx, c_ctx, w_mod, b_mod, norm1_g, norm2_g, w_in, attn_q_norm_g, attn_k_norm_g, ml_conv_w,
           ml_gate_b, ml_norm_g, hg_lb_logits, hg_norm_g, w_branch, w_out, ffn_w_gate, ffn_w_up, ffn_w_down,
           moe_w_router, moe_w_gate, moe_w_up, moe_w_down):
    n_batch, seq, d = x.shape
    ctx_len = ctx.shape[1]
    depth = w_mod.shape[0]
    m_lat = n_batch * seq
    m_ctx = n_batch * ctx_len
    m_all = m_lat + m_ctx
    assert d == D_MODEL and w_in.shape[2] == _N_IN
    assert seq % ROW_TILE == 0 and m_ctx % ROW_TILE == 0 and seq % ctx_len == 0
    assert seq % ML_CHUNK == 0 and ctx_len % ML_CHUNK == 0 and ctx_len % HG_CHUNK == 0

    xa = jnp.concatenate([x.reshape(m_lat, d), ctx.reshape(m_ctx, d)], axis=0)
    cs = jnp.concatenate([c, c_ctx[None], jnp.zeros((8 - n_batch - 1, d), F32)], axis=0)
    mod = _modulation(cs, w_mod, b_mod)
    cos, sin = _rope_tables(n_batch, seq, m_ctx)
    lb_all = jnp.cumsum(jax.nn.softmax(hg_lb_logits.astype(F32), axis=0), axis=0)
    lb_all = lb_all - lb_all[:1]
    w_in_t = jnp.swapaxes(w_in, 1, 2)

    for l in range(depth):
        need_ctx = l < depth - 1
        n_rows = m_all if need_ctx else m_lat
        mods = [mod[l, :n_batch + 1, i * d:(i + 1) * d].reshape(n_batch + 1, 1, d) for i in range(N_MOD)]

        h = _norm_mod(xa, norm1_g[l], mods[0], mods[1], m_all, seq, n_batch)
        proj_h = _proj_in(h, w_in_t, l, *H_TILES, BF16)
        proj_f = _proj_in(h, w_in_t, l, *F_TILES, F32)

        q_rot, k_rot = _att_prep(proj_h, cos, sin, attn_q_norm_g[l], attn_k_norm_g[l])
        ya = _attention(q_rot, k_rot, proj_h, n_batch, seq, ctx_len, True)

        gates = proj_f[:, F_ML_G:F_ML_G + 4 * ML_HEADS].reshape(m_all, 4, ML_HEADS)
        gates_c = gates.transpose(2, 0, 1)
        gates_r = gates.transpose(2, 1, 0)
        ym, ym_c = _mlstm(proj_h, gates_c, gates_r, ml_conv_w[l], ml_gate_b[l], ml_norm_g[l],
                          n_batch, seq, ctx_len, need_ctx)
        yh, yh_c = _hgrn2(proj_h, proj_f, lb_all[l], hg_norm_g[l], n_batch, seq, ctx_len, need_ctx)
        ys_ctx = None
        if need_ctx:
            ya_c = _attention(q_rot, k_rot, proj_h, n_batch, seq, ctx_len, False)
            ys_ctx = (ya_c, ym_c, yh_c)

        merged = _branch_merge((ya, ym, yh), ys_ctx, w_branch, l, proj_h, n_rows)
        xa = _matmul_residual(merged, w_out, l, xa, mods[2], n_rows, seq, n_batch, "proj_out")

        if l % 2 == 0:
            h2 = _norm_mod(xa, norm2_g[l], mods[3], mods[4], n_rows, seq, n_batch)
            u = _glu(h2, ffn_w_gate, ffn_w_up, l // 2)
            xa = _matmul_residual(u, ffn_w_down, l // 2, xa, mods[5], n_rows, seq, n_batch, "ffn_down")
        else:
            xa = _moe_ffn(xa, norm2_g[l], mods[3], mods[4], mods[5], moe_w_router[l // 2],
                          moe_w_gate[l // 2], moe_w_up[l // 2], moe_w_down[l // 2], n_rows, seq, n_batch)
    return xa[:m_lat].reshape(n_batch, seq, d)
```

```python
import functools

import jax
import jax.numpy as jnp
from jax import lax
from jax.experimental import pallas as pl
from jax.experimental.pallas import tpu as pltpu

F32 = jnp.float32
BF16 = jnp.bfloat16

D_MODEL = 2048
GRID_W = 64
N_MOD = 6
ATT_HEADS = 8
ATT_KV_HEADS = 2
HEAD_DIM = 128
ATT_W = ATT_HEADS * HEAD_DIM
KV_W = ATT_KV_HEADS * HEAD_DIM
ROPE_AXIS_DIM = HEAD_DIM // 2
ROPE_BASE = 10000.0
ML_HEADS = 4
ML_DK = 256
ML_W = ML_HEADS * ML_DK
HG_HEADS = 8
HG_DK = 128
HG_W = HG_HEADS * HG_DK
D_FF = 5632
N_EXPERTS = 8
EPS = 1e-6
NEG_INIT = -1e30
LOG2E = 1.4426950408889634

_O_ATT_Q = 0
_O_ML_GATES = ATT_W + 2 * KV_W + 4 * ML_W
_O_HG_Q = _O_ML_GATES + 4 * ML_HEADS
_O_HG_FF = _O_HG_Q + HG_W
_O_HG_I = _O_HG_FF + 2 * HG_W
_O_MERGE = _O_HG_I + 2 * HG_W
_N_IN = _O_MERGE + 3 * D_MODEL

H_ATT_Q = 0
H_ATT_K = 1024
H_ATT_V = 1280
H_ML_Q = 1536
H_ML_K = 2560
H_ML_V = 3584
H_ML_O = 4608
H_HG_Q = 5632
H_HG_I = 6656
H_HG_G = 7680
H_MERGE = 8704
H_WIDTH = 14848
F_HG_FF = 0
F_HG_FB = 1024
F_ML_G = 2048

LANE = 128
ROW_TILE = 512
COL_TILE = 512
PROJ_ROW_TILES = 4
GLU_ROW_TILES = 8
ELEMENTWISE_ROWS = 512
W_MISALIGN = _O_HG_Q % LANE


def _source_tiles(segments):
    blocks, shifted = [], []
    for off, width in segments:
        for start in range(off, off + width, COL_TILE):
            assert start % COL_TILE in (0, W_MISALIGN)
            blocks.append(start // COL_TILE)
            shifted.append(int(start % COL_TILE != 0))
    return blocks, shifted


H_TILES = _source_tiles([(0, _O_ML_GATES), (_O_HG_Q, HG_W), (_O_HG_I, 2 * HG_W + 3 * D_MODEL)])
F_TILES = _source_tiles([(_O_HG_FF, 2 * HG_W), (_O_ML_GATES, COL_TILE)])
VMEM_LIMIT = 56 * 1024 * 1024

ML_CHUNK = 256
HG_CHUNK = 64
HG_LEAF = 8
HG_UNROLL = 2
ATT_TQ = 512
MOE_TILE = 512
MOE_SUB_ROWS = 128

_NT = (((1,), (1,)), ((), ()))
_TN = (((0,), (0,)), ((), ()))


def _params(n_axes):
    return pltpu.CompilerParams(dimension_semantics=("arbitrary",) * n_axes,
                                vmem_limit_bytes=VMEM_LIMIT)


def _silu(x):
    return x * jax.nn.sigmoid(x)


def _log_sigmoid(x):
    return jnp.minimum(x, 0.0) - jnp.log1p(jnp.exp(-jnp.abs(x)))


def _mod_kernel(c_ref, w_ref, b_ref, o_ref):
    a = _silu(c_ref[...])
    o_ref[...] = jnp.dot(a.astype(BF16), w_ref[...].astype(BF16), preferred_element_type=F32) + b_ref[...]


def _modulation(cs, w_mod, b_mod):
    depth, d, n = w_mod.shape
    tn = 1024
    return pl.pallas_call(
        _mod_kernel,
        grid=(depth, n // tn),
        in_specs=[pl.BlockSpec((8, d), lambda l, j: (0, 0)),
                  pl.BlockSpec((None, d, tn), lambda l, j: (l, 0, j)),
                  pl.BlockSpec((None, 1, tn), lambda l, j: (l, 0, j))],
        out_specs=pl.BlockSpec((None, 8, tn), lambda l, j: (l, 0, j)),
        out_shape=jax.ShapeDtypeStruct((depth, 8, n), F32),
        compiler_params=_params(2),
        name="modulation",
    )(cs, w_mod, b_mod.reshape(depth, 1, n))


def _norm_kernel(x_ref, g_ref, sh_ref, sc_ref, o_ref):
    x = x_ref[...]
    y = x * lax.rsqrt(jnp.mean(x * x, axis=-1, keepdims=True) + EPS)
    y = y * g_ref[...]
    o_ref[...] = (y * (1.0 + sc_ref[0]) + sh_ref[0]).astype(o_ref.dtype)


def _group_of(i, tiles_per_batch, n_batch):
    return jnp.minimum(i // tiles_per_batch, n_batch)


def _norm_mod(x, g, shift, scale, n_rows, seq, n_batch):
    d = x.shape[1]
    tm = ELEMENTWISE_ROWS
    tpb = seq // tm
    grp = lambda i: (_group_of(i, tpb, n_batch), 0, 0)
    return pl.pallas_call(
        _norm_kernel,
        grid=(n_rows // tm,),
        in_specs=[pl.BlockSpec((tm, d), lambda i: (i, 0)),
                  pl.BlockSpec((1, d), lambda i: (0, 0)),
                  pl.BlockSpec((1, 1, d), grp),
                  pl.BlockSpec((1, 1, d), grp)],
        out_specs=pl.BlockSpec((tm, d), lambda i: (i, 0)),
        out_shape=jax.ShapeDtypeStruct((n_rows, d), BF16),
        compiler_params=_params(1),
        name="norm_mod",
    )(x, g.reshape(1, d), shift, scale)


def _proj_in_kernel(blk_ref, shift_ref, a_ref, wa_ref, wb_ref, o_ref, w_scr):
    j = pl.program_id(0)

    @pl.when(pl.program_id(1) == 0)
    def _():
        @pl.when(shift_ref[j] == 0)
        def _():
            w_scr[...] = wa_ref[...].T.astype(BF16)

        @pl.when(shift_ref[j] != 0)
        def _():
            w = jnp.concatenate([wa_ref[W_MISALIGN:, :], wb_ref[...]], axis=0)
            w_scr[...] = w.T.astype(BF16)

    o_ref[...] = jnp.dot(a_ref[...], w_scr[...], preferred_element_type=F32).astype(o_ref.dtype)


def _proj_in(a, w_in_t, layer, src_blocks, shifted, out_dtype):
    m, k = a.shape
    n_src = w_in_t.shape[1]
    tm, tn = m // PROJ_ROW_TILES, COL_TILE
    n_tiles = len(src_blocks)
    last_blk = n_src // W_MISALIGN - 1
    per = tn // W_MISALIGN
    return pl.pallas_call(
        _proj_in_kernel,
        grid_spec=pltpu.PrefetchScalarGridSpec(
            num_scalar_prefetch=2,
            grid=(n_tiles, m // tm),
            in_specs=[pl.BlockSpec((tm, k), lambda j, i, blk, sh: (i, 0)),
                      pl.BlockSpec((None, tn, k), lambda j, i, blk, sh: (layer, blk[j], 0)),
                      pl.BlockSpec((None, W_MISALIGN, k), lambda j, i, blk, sh: (
                          layer, jnp.minimum(per * (blk[j] + 1), last_blk), 0))],
            out_specs=pl.BlockSpec((tm, tn), lambda j, i, blk, sh: (i, j)),
            scratch_shapes=[pltpu.VMEM((k, tn), BF16)]),
        out_shape=jax.ShapeDtypeStruct((m, n_tiles * tn), out_dtype),
        compiler_params=_params(2),
        name="proj_in",
    )(jnp.asarray(src_blocks, jnp.int32), jnp.asarray(shifted, jnp.int32), a, w_in_t, w_in_t)


def _mm_res_kernel(a_ref, w_ref, r_ref, g_ref, o_ref, wb_ref):
    @pl.when(pl.program_id(1) == 0)
    def _():
        wb_ref[...] = w_ref[...].astype(BF16)

    y = jnp.dot(a_ref[...], wb_ref[...], preferred_element_type=F32)
    o_ref[...] = r_ref[...] + g_ref[0] * y


def _matmul_residual(a, w, layer, res, gate, n_rows, seq, n_batch, name):
    k = a.shape[1]
    n = w.shape[2]
    tm, tn = ROW_TILE, COL_TILE
    tpb = seq // tm
    return pl.pallas_call(
        _mm_res_kernel,
        grid=(n // tn, n_rows // tm),
        in_specs=[pl.BlockSpec((tm, k), lambda j, i: (i, 0)),
                  pl.BlockSpec((None, k, tn), lambda j, i: (layer, 0, j)),
                  pl.BlockSpec((tm, tn), lambda j, i: (i, j)),
                  pl.BlockSpec((1, 1, tn), lambda j, i: (_group_of(i, tpb, n_batch), 0, j))],
        out_specs=pl.BlockSpec((tm, tn), lambda j, i: (i, j)),
        out_shape=jax.ShapeDtypeStruct((n_rows, n), F32),
        scratch_shapes=[pltpu.VMEM((k, tn), BF16)],
        compiler_params=_params(2),
        name=name,
    )(a, w, res, gate)


def _glu_kernel(a_ref, wg_ref, wu_ref, o_ref, wgb_ref, wub_ref):
    @pl.when(pl.program_id(1) == 0)
    def _():
        wgb_ref[...] = wg_ref[...].astype(BF16)
        wub_ref[...] = wu_ref[...].astype(BF16)

    a = a_ref[...]
    g = jnp.dot(a, wgb_ref[...], preferred_element_type=F32)
    u = jnp.dot(a, wub_ref[...], preferred_element_type=F32)
    o_ref[...] = (_silu(g) * u).astype(o_ref.dtype)


def _glu(a, wg, wu, layer):
    m, k = a.shape
    n = wg.shape[2]
    tm, tn = m // GLU_ROW_TILES, COL_TILE
    return pl.pallas_call(
        _glu_kernel,
        grid=(n // tn, m // tm),
        in_specs=[pl.BlockSpec((tm, k), lambda j, i: (i, 0)),
                  pl.BlockSpec((None, k, tn), lambda j, i: (layer, 0, j)),
                  pl.BlockSpec((None, k, tn), lambda j, i: (layer, 0, j))],
        out_specs=pl.BlockSpec((tm, tn), lambda j, i: (i, j)),
        out_shape=jax.ShapeDtypeStruct((m, n), BF16),
        scratch_shapes=[pltpu.VMEM((k, tn), BF16), pltpu.VMEM((k, tn), BF16)],
        compiler_params=_params(2),
        name="ffn_up",
    )(a, wg, wu)


def _merge_kernel(*refs, n_lat_tiles, with_ctx):
    if with_ctx:
        lat_refs, ctx_refs, rest = refs[0:3], refs[3:6], refs[6:]
    else:
        lat_refs, ctx_refs, rest = refs[0:3], None, refs[3:]
    w_ref, g0_ref, g1_ref, g2_ref, o_ref, wb_ref = rest

    @pl.when(pl.program_id(1) == 0)
    def _():
        wb_ref[...] = w_ref[...].astype(BF16)

    is_ctx = pl.program_id(1) >= n_lat_tiles
    acc = None
    for b, g_ref in enumerate((g0_ref, g1_ref, g2_ref)):
        y = lat_refs[b][...]
        if with_ctx:
            y = jnp.where(is_ctx, ctx_refs[b][...], y)
        term = jax.nn.sigmoid(g_ref[...].astype(F32)) * jnp.dot(y, wb_ref[b], preferred_element_type=F32)
        acc = term if acc is None else acc + term
    o_ref[...] = acc.astype(o_ref.dtype)


def _branch_merge(ys_lat, ys_ctx, w_branch, layer, proj_h, n_rows):
    k = ys_lat[0].shape[1]
    n = w_branch.shape[3]
    tm, tn = ROW_TILE, COL_TILE
    n_lat_tiles = ys_lat[0].shape[0] // tm
    with_ctx = ys_ctx is not None
    gate_spec = lambda b: pl.BlockSpec((tm, tn), lambda j, i: (i, (H_MERGE + b * n) // tn + j))
    lat_spec = pl.BlockSpec((tm, k), lambda j, i: (jnp.minimum(i, n_lat_tiles - 1), 0))
    ctx_spec = pl.BlockSpec((tm, k), lambda j, i: (jnp.maximum(i - n_lat_tiles, 0), 0))
    in_specs = [lat_spec] * 3 + ([ctx_spec] * 3 if with_ctx else [])
    in_specs += [pl.BlockSpec((None, 3, k, tn), lambda j, i: (layer, 0, 0, j)),
                 gate_spec(0), gate_spec(1), gate_spec(2)]
    args = list(ys_lat) + (list(ys_ctx) if with_ctx else [])
    return pl.pallas_call(
        functools.partial(_merge_kernel, n_lat_tiles=n_lat_tiles, with_ctx=with_ctx),
        grid=(n // tn, n_rows // tm),
        in_specs=in_specs,
        out_specs=pl.BlockSpec((tm, tn), lambda j, i: (i, j)),
        out_shape=jax.ShapeDtypeStruct((n_rows, n), BF16),
        scratch_shapes=[pltpu.VMEM((3, k, tn), BF16)],
        compiler_params=_params(2),
        name="branch_merge",
    )(*args, w_branch, proj_h, proj_h, proj_h)


def _head_rms(x, g):
    return x * lax.rsqrt(jnp.mean(x * x, axis=-1, keepdims=True) + EPS) * g


def _rope(x, cos, sin, low_half):
    partner = jnp.where(low_half, pltpu.roll(x, 96, axis=1), pltpu.roll(x, 32, axis=1))
    return x * cos + partner * sin


def _att_prep_kernel(q_ref, k_ref, cos_ref, sin_ref, qg_ref, kg_ref, qo_ref, ko_ref):
    cos = cos_ref[...]
    sin = sin_ref[...]
    lane = lax.broadcasted_iota(jnp.int32, cos.shape, 1)
    low_half = jnp.bitwise_and(lane, 63) < 32
    scale = HEAD_DIM ** -0.5 * LOG2E
    for h in range(ATT_HEADS):
        sl = slice(h * HEAD_DIM, (h + 1) * HEAD_DIM)
        x = _head_rms(q_ref[:, sl].astype(F32), qg_ref[...])
        qo_ref[:, sl] = (_rope(x, cos, sin, low_half) * scale).astype(qo_ref.dtype)
    for h in range(ATT_KV_HEADS):
        sl = slice(h * HEAD_DIM, (h + 1) * HEAD_DIM)
        x = _head_rms(k_ref[:, sl].astype(F32), kg_ref[...])
        ko_ref[:, sl] = _rope(x, cos, sin, low_half).astype(ko_ref.dtype)


def _att_prep(proj_h, cos, sin, qg, kg):
    m = proj_h.shape[0]
    tm = ROW_TILE
    return pl.pallas_call(
        _att_prep_kernel,
        grid=(m // tm,),
        in_specs=[pl.BlockSpec((tm, ATT_W), lambda i: (i, H_ATT_Q // ATT_W)),
                  pl.BlockSpec((tm, KV_W), lambda i: (i, H_ATT_K // KV_W)),
                  pl.BlockSpec((tm, HEAD_DIM), lambda i: (i, 0)),
                  pl.BlockSpec((tm, HEAD_DIM), lambda i: (i, 0)),
                  pl.BlockSpec((1, HEAD_DIM), lambda i: (0, 0)),
                  pl.BlockSpec((1, HEAD_DIM), lambda i: (0, 0))],
        out_specs=[pl.BlockSpec((tm, ATT_W), lambda i: (i, 0)),
                   pl.BlockSpec((tm, KV_W), lambda i: (i, 0))],
        out_shape=[jax.ShapeDtypeStruct((m, ATT_W), BF16),
                   jax.ShapeDtypeStruct((m, KV_W), BF16)],
        compiler_params=_params(1),
        name="att_prep",
    )(proj_h, proj_h, cos, sin, qg.reshape(1, HEAD_DIM), kg.reshape(1, HEAD_DIM))


def _att_kernel(*refs, with_latent):
    if with_latent:
        q_ref, kc_ref, vc_ref, kl_ref, vl_ref, o_ref = refs
    else:
        q_ref, kc_ref, vc_ref, o_ref = refs
    grp = ATT_HEADS // ATT_KV_HEADS
    for g in range(grp):
        sl = slice(g * HEAD_DIM, (g + 1) * HEAD_DIM)
        q = q_ref[:, sl]
        sc = lax.dot_general(q, kc_ref[...], _NT, preferred_element_type=F32)
        m = jnp.max(sc, axis=-1, keepdims=True)
        if with_latent:
            sl_ = lax.dot_general(q, kl_ref[...], _NT, preferred_element_type=F32)
            m = jnp.maximum(m, jnp.max(sl_, axis=-1, keepdims=True))
        pc = jnp.exp2(sc - m)
        den = jnp.sum(pc, axis=-1, keepdims=True)
        acc = jnp.dot(pc.astype(BF16), vc_ref[...], preferred_element_type=F32)
        if with_latent:
            pl_ = jnp.exp2(sl_ - m)
            den += jnp.sum(pl_, axis=-1, keepdims=True)
            acc += jnp.dot(pl_.astype(BF16), vl_ref[...], preferred_element_type=F32)
        o_ref[:, sl] = (acc / den).astype(o_ref.dtype)


def _attention(q_rot, k_rot, proj_h, n_batch, seq, ctx_len, latent_queries):
    grp_w = ATT_W // ATT_KV_HEADS
    ctx_blk0 = n_batch * seq // ctx_len
    v_col = H_ATT_V // HEAD_DIM
    if latent_queries:
        tq = ATT_TQ
        nq = seq // tq
        q_map = lambda b, k, i: (b * nq + i, k)
        n_out = n_batch * seq
    else:
        tq = ctx_len
        nq = 1
        q_map = lambda b, k, i: (ctx_blk0 + b, k)
        n_out = n_batch * ctx_len
    in_specs = [pl.BlockSpec((tq, grp_w), q_map),
                pl.BlockSpec((ctx_len, HEAD_DIM), lambda b, k, i: (ctx_blk0 + b, k)),
                pl.BlockSpec((ctx_len, HEAD_DIM), lambda b, k, i: (ctx_blk0 + b, v_col + k))]
    args = [q_rot, k_rot, proj_h]
    if latent_queries:
        in_specs += [pl.BlockSpec((seq, HEAD_DIM), lambda b, k, i: (b, k)),
                     pl.BlockSpec((seq, HEAD_DIM), lambda b, k, i: (b, v_col + k))]
        args += [k_rot, proj_h]
    return pl.pallas_call(
        functools.partial(_att_kernel, with_latent=latent_queries),
        grid=(n_batch, ATT_KV_HEADS, nq),
        in_specs=in_specs,
        out_specs=pl.BlockSpec((tq, grp_w), lambda b, k, i: (b * nq + i, k)),
        out_shape=jax.ShapeDtypeStruct((n_out, ATT_W), BF16),
        compiler_params=_params(3),
        name="attention_lat" if latent_queries else "attention_ctx",
    )(*args)


def _conv3(x_ref, w, c, n_chunks, chunk):
    r0 = pl.multiple_of(c * chunk, chunk)
    x = x_ref[pl.ds(r0, chunk), :].astype(F32)
    prev_blk = x_ref[pl.ds(pl.multiple_of(jnp.maximum(r0 - 16, 0), 16), 16), :].astype(F32)
    next_blk = x_ref[pl.ds(pl.multiple_of(jnp.minimum(r0 + chunk, (n_chunks - 1) * chunk), 16), 16), :].astype(F32)
    prev_row = jnp.where(c > 0, prev_blk[15:16, :], 0.0)
    next_row = jnp.where(c < n_chunks - 1, next_blk[0:1, :], 0.0)
    row = lax.broadcasted_iota(jnp.int32, x.shape, 0)
    x_m1 = jnp.where(row == 0, prev_row, pltpu.roll(x, 1, axis=0))
    x_p1 = jnp.where(row == chunk - 1, next_row, pltpu.roll(x, chunk - 1, axis=0))
    return x_m1 * w[0:1, :] + x * w[1:2, :] + x_p1 * w[2:3, :]


def _ml_chunks(chains):
    n_tok = chains[0][0].shape[0]
    ti = lax.broadcasted_iota(jnp.int32, (n_tok, n_tok), 0)
    si = lax.broadcasted_iota(jnp.int32, (n_tok, n_tok), 1)
    gate = []
    for q, k, v, lfr, lir, lfc, lic, c_ref, n_ref, m_st, rev in chains:
        causal = (si >= ti) if rev else (si <= ti)
        causal_t = (ti >= si) if rev else (ti <= si)
        fc_col = jnp.sum(jnp.where(causal, lfr, 0.0), axis=1, keepdims=True)
        fc_row = jnp.sum(jnp.where(causal_t, lfc, 0.0), axis=0, keepdims=True)
        f_tot = jnp.sum(lfr, axis=1, keepdims=True)
        dmat = jnp.where(causal, fc_col - fc_row + lir, -jnp.inf)
        inter = fc_col + m_st
        m_t = jnp.maximum(inter, jnp.max(dmat, axis=1, keepdims=True))
        w = jnp.exp(dmat - m_t)
        a = jnp.exp(inter - m_t)
        g_col = f_tot - fc_col + lic
        g_row = f_tot - fc_row + lir
        m_new = jnp.maximum(f_tot + m_st, jnp.max(g_row, axis=1, keepdims=True))
        decay = jnp.exp(f_tot + m_st - m_new)
        kw = k.astype(F32) * jnp.exp(g_col - m_new)
        gate.append((w, a, m_t, m_new, decay, kw))
    scs = [lax.dot_general(ch[0], ch[1], _NT, preferred_element_type=F32) * g[0]
           for ch, g in zip(chains, gate)]
    nums = [jnp.dot(sc.astype(BF16), ch[2], preferred_element_type=F32) for ch, sc in zip(chains, scs)]
    carried = [jnp.dot(ch[0], ch[7][...].astype(BF16), preferred_element_type=F32) for ch in chains]
    hs = []
    for ch, g, sc, num, car in zip(chains, gate, scs, nums, carried):
        q, n_ref = ch[0], ch[8]
        w, a, m_t, m_new, decay, kw = g
        den = jnp.sum(sc, axis=1, keepdims=True)
        den += a * jnp.sum(q.astype(F32) * n_ref[...], axis=1, keepdims=True)
        hs.append((num + a * car) / jnp.maximum(jnp.abs(den), jnp.exp(-m_t)))
    for ch, g in zip(chains, gate):
        v, c_ref, n_ref = ch[2], ch[7], ch[8]
        w, a, m_t, m_new, decay, kw = g
        c_ref[...] = decay * c_ref[...] + lax.dot_general(kw.astype(BF16), v, _TN, preferred_element_type=F32)
        n_ref[...] = decay * n_ref[...] + jnp.sum(kw, axis=0, keepdims=True)
    return hs, [g[3] for g in gate]


def _ml_kernel(*refs, seq, ctx_len, need_ctx):
    (ql_ref, kl_ref, vl_ref, ol_ref, qc_ref, kc_ref, vc_ref, oc_ref,
     wq_ref, wk_ref, gcl_ref, grl_ref, gcc_ref, grc_ref, bc_ref, br_ref, ng_ref) = refs[:17]
    if need_ctx:
        yl_ref, yc_ref = refs[17:19]
        scratch = refs[19:]
    else:
        yl_ref = refs[17]
        yc_ref = None
        scratch = refs[18:]
    qsl_ref, ksl_ref, qsc_ref, ksc_ref, hl_ref, hc_ref, cf_ref, nf_ref, cb_ref, nb_ref = scratch
    chunk = ML_CHUNK
    k_scale = ML_DK ** -0.5
    wq = wq_ref[0]
    wk = wk_ref[0]

    def conv_pass(src_q, src_k, dst_q, dst_k, n_chunks):
        def body(c, carry):
            r0 = pl.multiple_of(c * chunk, chunk)
            dst_q[pl.ds(r0, chunk), :] = _conv3(src_q, wq, c, n_chunks, chunk).astype(BF16)
            dst_k[pl.ds(r0, chunk), :] = (_conv3(src_k, wk, c, n_chunks, chunk) * k_scale).astype(BF16)
            return carry
        lax.fori_loop(0, n_chunks, body, 0)

    conv_pass(qc_ref, kc_ref, qsc_ref, ksc_ref, ctx_len // chunk)
    conv_pass(ql_ref, kl_ref, qsl_ref, ksl_ref, seq // chunk)

    bias_c = bc_ref[0]
    bias_r = br_ref[0]
    norm_g = ng_ref[0]

    def run(q_s, k_s, v_s, o_s, gc_s, gr_s, h_s, y_s, n_chunks, m_states):
        def chain(c, rev, c_r, n_r, m_st):
            d = 1 if rev else 0
            rows = pl.ds(pl.multiple_of(c * chunk, chunk), chunk)
            gc = gc_s[0, rows, :] + bias_c
            gr = gr_s[0, :, rows] + bias_r
            lic = gc[:, d:d + 1]
            lfc = _log_sigmoid(gc[:, 2 + d:3 + d])
            lir = gr[d:d + 1, :]
            lfr = _log_sigmoid(gr[2 + d:3 + d, :])
            return rows, (q_s[rows, :], k_s[rows, :], v_s[rows, :], lfr, lir, lfc, lic, c_r, n_r, m_st, rev)

        def finish(rows, h_sum):
            if y_s is not None:
                y = _head_rms(h_sum, norm_g) * jax.nn.sigmoid(o_s[rows, :].astype(F32))
                y_s[rows, :] = y.astype(y_s.dtype)

        def step(ci, m_states, phase):
            rows_f, chain_f = chain(ci, False, cf_ref, nf_ref, m_states[0])
            rows_b, chain_b = chain(n_chunks - 1 - ci, True, cb_ref, nb_ref, m_states[1])
            (h_f, h_b), m_new = _ml_chunks([chain_f, chain_b])
            if phase == 0:
                h_s[rows_f, :] = h_f
                h_s[rows_b, :] = h_b
            elif phase == 1:
                finish(rows_f, h_f + h_b)
            else:
                finish(rows_f, h_s[rows_f, :] + h_f)
                finish(rows_b, h_s[rows_b, :] + h_b)
            return tuple(m_new)

        half = n_chunks // 2
        m_states = lax.fori_loop(0, half, lambda ci, m: step(ci, m, 0), m_states)
        if n_chunks % 2:
            m_states = step(half, m_states, 1)
        return lax.fori_loop(n_chunks - half, n_chunks, lambda ci, m: step(ci, m, 2), m_states)

    for ref in (cf_ref, nf_ref, cb_ref, nb_ref):
        ref[...] = jnp.zeros_like(ref)
    m0 = jnp.full((1, 1), NEG_INIT, F32)
    m1 = run(qsc_ref, ksc_ref, vc_ref, oc_ref, gcc_ref, grc_ref, hc_ref, yc_ref, ctx_len // chunk, (m0, m0))
    run(qsl_ref, ksl_ref, vl_ref, ol_ref, gcl_ref, grl_ref, hl_ref, yl_ref, seq // chunk, m1)


def _mlstm(proj_h, gates_c, gates_r, conv_w, gate_b, norm_g, n_batch, seq, ctx_len, need_ctx):
    w = ML_DK
    cb0 = n_batch * seq // ctx_len
    col = lambda off: off // w
    lat = lambda off: pl.BlockSpec((seq, w), lambda b, h: (b, col(off) + h))
    ctx = lambda off: pl.BlockSpec((ctx_len, w), lambda b, h: (cb0 + b, col(off) + h))
    conv = conv_w.reshape(3, 2 * ML_HEADS, w).transpose(1, 0, 2)
    gb = gate_b.reshape(4, ML_HEADS).T
    in_specs = [lat(H_ML_Q), lat(H_ML_K), lat(H_ML_V), lat(H_ML_O),
                ctx(H_ML_Q), ctx(H_ML_K), ctx(H_ML_V), ctx(H_ML_O),
                pl.BlockSpec((1, 3, w), lambda b, h: (h, 0, 0)),
                pl.BlockSpec((1, 3, w), lambda b, h: (ML_HEADS + h, 0, 0)),
                pl.BlockSpec((1, seq, 4), lambda b, h: (h, b, 0)),
                pl.BlockSpec((1, 4, seq), lambda b, h: (h, 0, b)),
                pl.BlockSpec((1, ctx_len, 4), lambda b, h: (h, cb0 + b, 0)),
                pl.BlockSpec((1, 4, ctx_len), lambda b, h: (h, 0, cb0 + b)),
                pl.BlockSpec((1, 1, 4), lambda b, h: (h, 0, 0)),
                pl.BlockSpec((1, 4, 1), lambda b, h: (h, 0, 0)),
                pl.BlockSpec((1, 1, w), lambda b, h: (h, 0, 0))]
    out_specs = [pl.BlockSpec((seq, w), lambda b, h: (b, h))]
    out_shape = [jax.ShapeDtypeStruct((n_batch * seq, ML_W), BF16)]
    if need_ctx:
        out_specs.append(pl.BlockSpec((ctx_len, w), lambda b, h: (b, h)))
        out_shape.append(jax.ShapeDtypeStruct((n_batch * ctx_len, ML_W), BF16))
    outs = pl.pallas_call(
        functools.partial(_ml_kernel, seq=seq, ctx_len=ctx_len, need_ctx=need_ctx),
        grid=(n_batch, ML_HEADS),
        in_specs=in_specs,
        out_specs=out_specs,
        out_shape=out_shape,
        scratch_shapes=[pltpu.VMEM((seq, w), BF16), pltpu.VMEM((seq, w), BF16),
                        pltpu.VMEM((ctx_len, w), BF16), pltpu.VMEM((ctx_len, w), BF16),
                        pltpu.VMEM((seq, w), F32), pltpu.VMEM((ctx_len, w), F32),
                        pltpu.VMEM((w, w), F32), pltpu.VMEM((1, w), F32),
                        pltpu.VMEM((w, w), F32), pltpu.VMEM((1, w), F32)],
        compiler_params=_params(2),
        name="mlstm",
    )(proj_h, proj_h, proj_h, proj_h, proj_h, proj_h, proj_h, proj_h,
      conv, conv, gates_c, gates_r, gates_c, gates_r,
      gb.reshape(ML_HEADS, 1, 4), gb.reshape(ML_HEADS, 4, 1), norm_g.reshape(ML_HEADS, 1, w))
    return outs if need_ctx else (outs[0], None)


def _hg_level_tables(chunk, rev):
    levels = []
    n = HG_LEAF
    while 2 * n <= chunk:
        refs = []
        for start in range(0, chunk, 2 * n):
            refs.append((start + n - 1) if rev else (start + n))
        levels.append((n, refs))
        n *= 2
    return levels


def _tri_cumsum(tri, x):
    hi = x.astype(BF16)
    rest = x - hi.astype(F32)
    mid = rest.astype(BF16)
    lo = (rest - mid.astype(F32)).astype(BF16)
    out = jnp.dot(tri, hi, preferred_element_type=F32)
    out += jnp.dot(tri, mid, preferred_element_type=F32)
    return out + jnp.dot(tri, lo, preferred_element_type=F32)


def _hg_chunks(chains, loglb, log1mlb, one_m_lb):
    chunk = chains[0][1].shape[0]
    nb = chunk // HG_LEAF

    qs, ks, lks, gcums = [], [], [], []
    for qraw, z, v, st_ref, masks, tri, rev in chains:
        e = jnp.exp(-jnp.abs(z))
        one_pe = 1.0 + e
        lsig = jnp.minimum(z, 0.0) - jnp.log(one_pe)
        a = log1mlb + lsig
        lf = jnp.maximum(loglb, a) + jnp.log(1.0 + jnp.exp(-jnp.abs(loglb - a)))
        ks.append(one_m_lb * (jnp.where(z >= 0.0, e, 1.0) / one_pe))
        lks.append((a - z) * LOG2E)
        gcums.append(_tri_cumsum(tri, lf * LOG2E))
        x = qraw.astype(F32)
        qs.append(x / (1.0 + jnp.exp(-x)))

    atts = [jnp.zeros((chunk, chunk), F32) for _ in chains]
    for lvl, (n, _) in enumerate(_hg_level_tables(chunk, False)):
        for c, (qraw, z, v, st_ref, masks, tri, rev) in enumerate(chains):
            refs = _hg_level_tables(chunk, rev)[lvl][1]
            gref = jnp.concatenate(
                [jnp.broadcast_to(gcums[c][r:r + 1, :], (2 * n, HG_DK)) for r in refs], axis=0)
            dec = jnp.exp2(-jnp.abs(gcums[c] - gref))
            mixed = jnp.concatenate(
                [(qs[c] if ((r0 % (2 * n)) >= n) != rev else ks[c])[r0:r0 + HG_LEAF, :]
                 for r0 in range(0, chunk, HG_LEAF)], axis=0)
            md = (mixed * dec).astype(BF16)
            p = lax.dot_general(md, md, _NT, preferred_element_type=F32)
            atts[c] += jnp.where(masks[0][lvl], p, 0.0)

    inters = []
    for c, (qraw, z, v, st_ref, masks, tri, rev) in enumerate(chains):
        g_tot = gcums[c][0:1, :] if rev else gcums[c][chunk - 1:chunk, :]
        q_dec = (qs[c] * jnp.exp2(gcums[c])).astype(BF16)
        k_dec = (ks[c] * jnp.exp2(g_tot - gcums[c])).astype(BF16)
        st = st_ref[...]
        inters.append(lax.dot_general(q_dec, st.astype(BF16), _NT, preferred_element_type=F32))
        st_ref[...] = st * jnp.exp2(g_tot) + lax.dot_general(v, k_dec, _TN, preferred_element_type=F32)

    outs = []
    for c, (qraw, z, v, st_ref, masks, tri, rev) in enumerate(chains):
        g3 = gcums[c].reshape(nb, HG_LEAF, HG_DK)
        h3 = (lks[c] - gcums[c]).reshape(nb, HG_LEAF, HG_DK)
        q3 = qs[c].reshape(nb, HG_LEAF, HG_DK)
        att = atts[c]
        for s in range(HG_LEAF):
            col = jnp.sum(q3 * jnp.exp2(g3 + h3[:, s:s + 1, :]), axis=2, keepdims=True)
            att = jnp.where(masks[1][s], col.reshape(chunk, 1), att)
        outs.append(inters[c] + jnp.dot(att.astype(BF16), v, preferred_element_type=F32))
    return outs


def _hg_kernel(*refs, seq, ctx_len, need_ctx):
    (ql_ref, il_ref, gl_ref, ffl_ref, fbl_ref, qc_ref, ic_ref, gc_ref, ffc_ref, fbc_ref,
     llb_ref, l1m_ref, oml_ref, ng_ref) = refs[:14]
    if need_ctx:
        yl_ref, yc_ref = refs[14:16]
        scratch = refs[16:]
    else:
        yl_ref = refs[14]
        yc_ref = None
        scratch = refs[15:]
    ol_ref, oc_ref, stf_ref, stb_ref = scratch
    chunk = HG_CHUNK
    loglb = llb_ref[0]
    log1mlb = l1m_ref[0]
    one_m_lb = oml_ref[0]
    norm_g = ng_ref[0]
    ti = lax.broadcasted_iota(jnp.int32, (chunk, chunk), 0)
    si = lax.broadcasted_iota(jnp.int32, (chunk, chunk), 1)

    def direction_consts(rev):
        tri = jnp.where((si >= ti) if rev else (si <= ti), 1.0, 0.0).astype(BF16)
        masks = []
        for n, _ in _hg_level_tables(chunk, rev):
            shift = (2 * n).bit_length() - 1
            same = jnp.right_shift(ti, shift) == jnp.right_shift(si, shift)
            t_late = jnp.bitwise_and(ti, 2 * n - 1) >= n
            s_late = jnp.bitwise_and(si, 2 * n - 1) >= n
            if rev:
                masks.append(same & jnp.logical_not(t_late) & s_late)
            else:
                masks.append(same & t_late & jnp.logical_not(s_late))
        t_in = jnp.bitwise_and(ti, HG_LEAF - 1)
        leaf_masks = []
        for s in range(HG_LEAF):
            at_col = si == (ti - t_in + s)
            leaf_masks.append(at_col & ((t_in <= s) if rev else (t_in >= s)))
        return tri, (masks, leaf_masks)

    tri_f, masks_f = direction_consts(False)
    tri_b, masks_b = direction_consts(True)

    def run(q_s, i_s, g_s, zf_s, zb_s, o_s, y_s, n_chunks):
        def step(pi, second):
            rows = []
            chains = []
            for u in range(HG_UNROLL):
                c = pi * HG_UNROLL + u
                r_f = pl.ds(pl.multiple_of(c * chunk, chunk), chunk)
                rows.append(r_f)
                chains.append((q_s[r_f, :], zf_s[r_f, :], i_s[r_f, :], stf_ref, masks_f, tri_f, False))
            for u in range(HG_UNROLL):
                c = n_chunks - 1 - (pi * HG_UNROLL + u)
                r_b = pl.ds(pl.multiple_of(c * chunk, chunk), chunk)
                rows.append(r_b)
                chains.append((q_s[r_b, :], zb_s[r_b, :], i_s[r_b, :], stb_ref, masks_b, tri_b, True))
            outs = _hg_chunks(chains, loglb, log1mlb, one_m_lb)
            for r, o in zip(rows, outs):
                if not second:
                    o_s[r, :] = o
                elif y_s is not None:
                    y = _head_rms(o_s[r, :] + o, norm_g) * _silu(g_s[r, :].astype(F32))
                    y_s[r, :] = y.astype(y_s.dtype)

        def first(pi, carry):
            step(pi, False)
            return carry

        def second(pi, carry):
            step(pi, True)
            return carry

        n_steps = n_chunks // HG_UNROLL
        lax.fori_loop(0, n_steps // 2, first, 0)
        lax.fori_loop(n_steps // 2, n_steps, second, 0)

    stf_ref[...] = jnp.zeros_like(stf_ref)
    stb_ref[...] = jnp.zeros_like(stb_ref)
    run(qc_ref, ic_ref, gc_ref, ffc_ref, fbc_ref, oc_ref, yc_ref, ctx_len // chunk)
    run(ql_ref, il_ref, gl_ref, ffl_ref, fbl_ref, ol_ref, yl_ref, seq // chunk)


def _hgrn2(proj_h, proj_f, lb, norm_g, n_batch, seq, ctx_len, need_ctx):
    w = HG_DK
    cb0 = n_batch * seq // ctx_len
    lat = lambda off: pl.BlockSpec((seq, w), lambda b, h: (b, off // w + h))
    ctx = lambda off: pl.BlockSpec((ctx_len, w), lambda b, h: (cb0 + b, off // w + h))
    vec = pl.BlockSpec((1, 1, w), lambda b, h: (h, 0, 0))
    lb3 = lb.reshape(HG_HEADS, 1, w)
    in_specs = [lat(H_HG_Q), lat(H_HG_I), lat(H_HG_G), lat(F_HG_FF), lat(F_HG_FB),
                ctx(H_HG_Q), ctx(H_HG_I), ctx(H_HG_G), ctx(F_HG_FF), ctx(F_HG_FB),
                vec, vec, vec, vec]
    out_specs = [pl.BlockSpec((seq, w), lambda b, h: (b, h))]
    out_shape = [jax.ShapeDtypeStruct((n_batch * seq, HG_W), BF16)]
    if need_ctx:
        out_specs.append(pl.BlockSpec((ctx_len, w), lambda b, h: (b, h)))
        out_shape.append(jax.ShapeDtypeStruct((n_batch * ctx_len, HG_W), BF16))
    outs = pl.pallas_call(
        functools.partial(_hg_kernel, seq=seq, ctx_len=ctx_len, need_ctx=need_ctx),
        grid=(n_batch, HG_HEADS),
        in_specs=in_specs,
        out_specs=out_specs,
        out_shape=out_shape,
        scratch_shapes=[pltpu.VMEM((seq, w), F32), pltpu.VMEM((ctx_len, w), F32),
                        pltpu.VMEM((w, w), F32), pltpu.VMEM((w, w), F32)],
        compiler_params=_params(2),
        name="hgrn2",
    )(proj_h, proj_h, proj_h, proj_f, proj_f, proj_h, proj_h, proj_h, proj_f, proj_f,
      jnp.log(lb3), jnp.log1p(-lb3), 1.0 - lb3, norm_g.reshape(HG_HEADS, 1, w))
    return outs if need_ctx else (outs[0], None)


def _router_kernel(x_ref, g_ref, sh_ref, sc_ref, wr_ref, h_ref, idx_ref, wt_ref, cnt_ref, carry_ref):
    @pl.when(pl.program_id(0) == 0)
    def _():
        carry_ref[...] = jnp.zeros_like(carry_ref)

    x = x_ref[...]
    y = x * lax.rsqrt(jnp.mean(x * x, axis=-1, keepdims=True) + EPS) * g_ref[...]
    h = y * (1.0 + sc_ref[0]) + sh_ref[0]
    h_ref[...] = h.astype(h_ref.dtype)
    logits = jnp.dot(h, wr_ref[...], precision=lax.Precision.HIGHEST, preferred_element_type=F32)
    lane = lax.broadcasted_iota(jnp.int32, logits.shape, 1).astype(F32)
    lg = jnp.where(lane < N_EXPERTS, logits, -jnp.inf)
    m1 = jnp.max(lg, axis=-1, keepdims=True)
    i1 = jnp.min(jnp.where(lg == m1, lane, float(LANE)), axis=-1, keepdims=True)
    lg2 = jnp.where(lane == i1, -jnp.inf, lg)
    m2 = jnp.max(lg2, axis=-1, keepdims=True)
    i2 = jnp.min(jnp.where(lg2 == m2, lane, float(LANE)), axis=-1, keepdims=True)
    e = jnp.exp(m2 - m1)
    w1 = 1.0 / (1.0 + e)
    w2 = e / (1.0 + e)
    wt_ref[...] = jnp.where(lane == 0, w1, jnp.where(lane == 1, w2, 0.0))
    n_tok = x.shape[0]
    routed = jnp.where(jnp.logical_or(lane == i1, lane == i2), 1.0, 0.0)
    ti = lax.broadcasted_iota(jnp.int32, (n_tok, n_tok), 0)
    ui = lax.broadcasted_iota(jnp.int32, (n_tok, n_tok), 1)
    earlier = jnp.where(ui < ti, 1.0, 0.0).astype(BF16)
    before = carry_ref[...] + jnp.dot(earlier, routed.astype(BF16), preferred_element_type=F32)
    r1 = jnp.sum(jnp.where(lane == i1, before, 0.0), axis=-1, keepdims=True)
    r2 = jnp.sum(jnp.where(lane == i2, before, 0.0), axis=-1, keepdims=True)
    carry_ref[...] += jnp.sum(routed, axis=0, keepdims=True)
    cnt_ref[...] = jnp.broadcast_to(carry_ref[...], cnt_ref.shape).astype(jnp.int32)
    idx_ref[...] = jnp.where(lane == 0, i1, jnp.where(lane == 1, i2, jnp.where(
        lane == 2, r1, jnp.where(lane == 3, r2, 0.0)))).astype(jnp.int32)


def _router(x, g, shift, scale, w_router, n_rows, seq, n_batch):
    d = x.shape[1]
    tm = ELEMENTWISE_ROWS
    tpb = seq // tm
    grp = lambda i: (_group_of(i, tpb, n_batch), 0, 0)
    wr = jnp.pad(w_router, ((0, 0), (0, LANE - N_EXPERTS)))
    return pl.pallas_call(
        _router_kernel,
        grid=(n_rows // tm,),
        in_specs=[pl.BlockSpec((tm, d), lambda i: (i, 0)),
                  pl.BlockSpec((1, d), lambda i: (0, 0)),
                  pl.BlockSpec((1, 1, d), grp),
                  pl.BlockSpec((1, 1, d), grp),
                  pl.BlockSpec((d, LANE), lambda i: (0, 0))],
        out_specs=[pl.BlockSpec((tm, d), lambda i: (i, 0)),
                   pl.BlockSpec((tm, LANE), lambda i: (i, 0)),
                   pl.BlockSpec((tm, LANE), lambda i: (i, 0)),
                   pl.BlockSpec((8, LANE), lambda i: (0, 0))],
        out_shape=[jax.ShapeDtypeStruct((n_rows, d), F32),
                   jax.ShapeDtypeStruct((n_rows, LANE), jnp.int32),
                   jax.ShapeDtypeStruct((n_rows, LANE), F32),
                   jax.ShapeDtypeStruct((8, LANE), jnp.int32)],
        scratch_shapes=[pltpu.VMEM((1, LANE), F32)],
        compiler_params=_params(1),
        name="router",
    )(x, g.reshape(1, d), shift, scale, wr)


def _row_copy(src_hbm, buf_ref, sem, src_row, r):
    return pltpu.make_async_copy(src_hbm.at[pl.ds(src_row, 1)], buf_ref.at[pl.ds(r, 1)], sem)


def _dispatch_kernel(src_ref, nu_ref, h_hbm, o_ref, buf_ref, sem):
    t = pl.program_id(0)
    tm = buf_ref.shape[1]
    slot = t % 2

    def tile_copies(tile, slot, fn):
        def body(r8, carry):
            for u in range(8):
                r = r8 * 8 + u
                fn(_row_copy(h_hbm, buf_ref.at[slot], sem.at[slot], src_ref[tile * tm + r], r), u)
            return carry
        lax.fori_loop(0, tm // 8, body, 0)

    start = lambda cp, u: cp.start(priority=u % 2)

    @pl.when(jnp.logical_and(t == 0, nu_ref[0] > 0))
    def _():
        tile_copies(0, 0, start)

    @pl.when(t + 1 < nu_ref[0])
    def _():
        tile_copies(t + 1, 1 - slot, start)

    @pl.when(t < nu_ref[0])
    def _():
        tile_copies(t, slot, lambda cp, u: cp.wait())
        o_ref[...] = buf_ref[slot].astype(o_ref.dtype)

    @pl.when(t >= nu_ref[0])
    def _():
        o_ref[...] = jnp.zeros_like(o_ref)


def _dispatch(h, src_token, n_used):
    d = h.shape[1]
    tm = MOE_TILE
    n_pad = src_token.shape[0]
    return pl.pallas_call(
        _dispatch_kernel,
        grid_spec=pltpu.PrefetchScalarGridSpec(
            num_scalar_prefetch=2,
            grid=(n_pad // tm,),
            in_specs=[pl.BlockSpec(memory_space=pl.ANY)],
            out_specs=pl.BlockSpec((tm, d), lambda t, src, nu: (t, 0)),
            scratch_shapes=[pltpu.VMEM((2, tm, d), F32), pltpu.SemaphoreType.DMA((2,))]),
        out_shape=jax.ShapeDtypeStruct((n_pad, d), BF16),
        compiler_params=_params(1),
        name="moe_dispatch",
    )(src_token, n_used, h)


def _new_expert(te_ref, i):
    return jnp.logical_or(i == 0, te_ref[i] != te_ref[jnp.maximum(i - 1, 0)])


def _moe_kernel(te_ref, nu_ref, nxt_ref, valid_ref, a_ref, *refs, n_w, glu):
    w_hbm = refs[:n_w]
    o_ref = refs[n_w]
    wbuf, wbf, sem, slot_ref = refs[n_w + 1:]
    j, i = pl.program_id(0), pl.program_id(1)
    n_j, n_i = pl.num_programs(0), pl.num_programs(1)
    tn = o_ref.shape[1]

    def weight_copies(slot, expert, col_tile):
        cols = pl.ds(pl.multiple_of(col_tile * tn, tn), tn)
        return [pltpu.make_async_copy(w_hbm[x].at[expert, :, cols], wbuf.at[slot, x], sem.at[slot, x])
                for x in range(n_w)]

    @pl.when(jnp.logical_and(j == 0, i == 0))
    def _():
        slot_ref[0] = 0
        for cp in weight_copies(0, te_ref[0], 0):
            cp.start()

    @pl.when(_new_expert(te_ref, i))
    def _():
        slot = slot_ref[0]
        for cp in weight_copies(slot, te_ref[i], j):
            cp.wait()
        nxt = nxt_ref[i]
        same_col = nxt < n_i

        @pl.when(jnp.logical_or(same_col, j + 1 < n_j))
        def _():
            nxt_expert = te_ref[jnp.where(same_col, nxt, 0)]
            for cp in weight_copies(1 - slot, nxt_expert, jnp.where(same_col, j, j + 1)):
                cp.start()

        for x in range(n_w):
            wbf[x] = wbuf[slot, x].astype(BF16)
        slot_ref[0] = 1 - slot

    def compute(rows):
        a = a_ref[rows, :]
        if glu:
            g = jnp.dot(a, wbf[0], preferred_element_type=F32)
            u = jnp.dot(a, wbf[1], preferred_element_type=F32)
            o_ref[rows, :] = (_silu(g) * u).astype(o_ref.dtype)
        else:
            o_ref[rows, :] = jnp.dot(a, wbf[0], preferred_element_type=F32).astype(o_ref.dtype)

    tm = o_ref.shape[0]
    valid = valid_ref[i]

    @pl.when(valid == tm)
    def _():
        compute(slice(None))

    @pl.when(valid < tm)
    def _():
        for r in range(tm // MOE_SUB_ROWS):
            rows = slice(r * MOE_SUB_ROWS, (r + 1) * MOE_SUB_ROWS)

            @pl.when(r * MOE_SUB_ROWS < valid)
            def _():
                compute(rows)

            @pl.when(r * MOE_SUB_ROWS >= valid)
            def _():
                o_ref[rows, :] = jnp.zeros((MOE_SUB_ROWS, o_ref.shape[1]), o_ref.dtype)


def _moe_grouped(a, weights, tile_expert, n_used, next_run, tile_valid, glu, out_dtype, name):
    n_rows, k = a.shape
    n = weights[0].shape[2]
    n_w = len(weights)
    tm, tn = MOE_TILE, COL_TILE
    a_map = lambda j, i, te, nu, nxt, tv: (jnp.maximum(jnp.minimum(i, nu[0] - 1), 0), 0)
    return pl.pallas_call(
        functools.partial(_moe_kernel, n_w=n_w, glu=glu),
        grid_spec=pltpu.PrefetchScalarGridSpec(
            num_scalar_prefetch=4,
            grid=(n // tn, n_rows // tm),
            in_specs=[pl.BlockSpec((tm, k), a_map)] + [pl.BlockSpec(memory_space=pl.ANY)] * n_w,
            out_specs=pl.BlockSpec((tm, tn), lambda j, i, te, nu, nxt, tv: (i, j)),
            scratch_shapes=[pltpu.VMEM((2, n_w, k, tn), F32), pltpu.VMEM((n_w, k, tn), BF16),
                            pltpu.SemaphoreType.DMA((2, n_w)), pltpu.SMEM((1,), jnp.int32)]),
        out_shape=jax.ShapeDtypeStruct((n_rows, n), out_dtype),
        compiler_params=_params(2),
        name=name,
    )(tile_expert, n_used, next_run, tile_valid, a, *weights)


def _combine_kernel(x_ref, ya_ref, yb_ref, wt_ref, g_ref, o_ref):
    wt = wt_ref[...]
    y = wt[:, 0:1] * ya_ref[...].astype(F32) + wt[:, 1:2] * yb_ref[...].astype(F32)
    o_ref[...] = x_ref[...] + g_ref[0] * y


def _moe_combine(x, ya, yb, wt, gate, n_rows, seq, n_batch):
    d = x.shape[1]
    tm = ELEMENTWISE_ROWS
    tpb = seq // tm
    row = pl.BlockSpec((tm, d), lambda i: (i, 0))
    return pl.pallas_call(
        _combine_kernel,
        grid=(n_rows // tm,),
        in_specs=[row, row, row,
                  pl.BlockSpec((tm, LANE), lambda i: (i, 0)),
                  pl.BlockSpec((1, 1, d), lambda i: (_group_of(i, tpb, n_batch), 0, 0))],
        out_specs=row,
        out_shape=jax.ShapeDtypeStruct((n_rows, d), F32),
        compiler_params=_params(1),
        name="moe_combine",
    )(x, ya, yb, wt, gate)


def _moe_ffn(x, g, shift, scale, gate, w_router, wg, wu, wd, n_rows, seq, n_batch):
    h, idx, wt, cnt = _router(x, g, shift, scale, w_router, n_rows, seq, n_batch)
    tm = MOE_TILE
    n_tiles = 2 * n_rows // tm + N_EXPERTS
    counts = cnt[0, :N_EXPERTS]
    tiles = (counts + tm - 1) // tm
    tile_end = jnp.cumsum(tiles)
    tile_start = tile_end - tiles
    expert = jnp.concatenate([idx[:, 0], idx[:, 1]])
    rank = jnp.concatenate([idx[:, 2], idx[:, 3]])
    token = jnp.concatenate([jnp.arange(n_rows, dtype=jnp.int32)] * 2)
    one_hot = (expert[:, None] == jnp.arange(N_EXPERTS, dtype=jnp.int32)[None, :]).astype(jnp.int32)
    dest = jnp.sum(tile_start[None, :] * one_hot, axis=1) * tm + rank
    src_token = jnp.zeros((n_tiles * tm,), jnp.int32).at[dest].set(token)
    n_used = tile_end[-1:].astype(jnp.int32)
    tile_id = jnp.minimum(jnp.arange(n_tiles, dtype=jnp.int32), n_used[0] - 1)
    tile_expert = jnp.sum((tile_end[None, :] <= tile_id[:, None]).astype(jnp.int32), axis=1)

    a_sorted = _dispatch(h, src_token, n_used)
    later = jnp.arange(n_tiles, dtype=jnp.int32)
    other = jnp.logical_and(later[None, :] > later[:, None], tile_expert[None, :] != tile_expert[:, None])
    next_run = jnp.min(jnp.where(other, later[None, :], n_tiles), axis=1).astype(jnp.int32)
    sel = (tile_expert[:, None] == jnp.arange(N_EXPERTS, dtype=jnp.int32)[None, :]).astype(jnp.int32)
    rows_left = jnp.sum(sel * counts[None, :], axis=1) - (later - jnp.sum(sel * tile_start[None, :], axis=1)) * tm
    tile_valid = jnp.where(later < n_used[0], jnp.clip(rows_left, 0, tm), 0).astype(jnp.int32)
    u = _moe_grouped(a_sorted, (wg, wu), tile_expert, n_used, next_run, tile_valid, True, BF16, "moe_up")
    y = _moe_grouped(u, (wd,), tile_expert, n_used, next_run, tile_valid, False, BF16, "moe_down")
    ya = jnp.take(y, dest[:n_rows], axis=0, mode="clip")
    yb = jnp.take(y, dest[n_rows:], axis=0, mode="clip")
    return _moe_combine(x, ya, yb, wt, gate, n_rows, seq, n_batch)


def _rope_tables(n_batch, seq, n_ctx_rows):
    t = jnp.arange(seq)
    pos = jnp.stack([t // GRID_W, t % GRID_W], axis=-1).astype(F32)
    inv_freq = jnp.exp(-jnp.log(ROPE_BASE) * jnp.arange(0, ROPE_AXIS_DIM, 2, dtype=F32) / ROPE_AXIS_DIM)
    ang = pos[..., None] * inv_freq
    cos, sin = jnp.cos(ang), jnp.sin(ang)
    cos = jnp.concatenate([cos[:, 0], cos[:, 0], cos[:, 1], cos[:, 1]], axis=-1)
    sin = jnp.concatenate([-sin[:, 0], sin[:, 0], -sin[:, 1], sin[:, 1]], axis=-1)
    cos = jnp.concatenate([jnp.tile(cos, (n_batch, 1)), jnp.ones((n_ctx_rows, HEAD_DIM), F32)], axis=0)
    sin = jnp.concatenate([jnp.tile(sin, (n_batch, 1)), jnp.zeros((n_ctx_rows, HEAD_DIM), F32)], axis=0)
    return cos, sin


def kernel(x, c, ctx, c_ctx, w_mod, b_mod, norm1_g, norm2_g, w_in, attn_q_norm_g, attn_k_norm_g, ml_conv_w,
           ml_gate_b, ml_norm_g, hg_lb_logits, hg_norm_g, w_branch, w_out, ffn_w_gate, ffn_w_up, ffn_w_down,
           moe_w_router, moe_w_gate, moe_w_up, moe_w_down):
    n_batch, seq, d = x.shape
    ctx_len = ctx.shape[1]
    depth = w_mod.shape[0]
    m_lat = n_batch * seq
    m_ctx = n_batch * ctx_len
    m_all = m_lat + m_ctx
    assert d == D_MODEL and w_in.shape[2] == _N_IN
    assert seq % ROW_TILE == 0 and m_ctx % ROW_TILE == 0 and seq % ctx_len == 0
    assert seq % ML_CHUNK == 0 and ctx_len % ML_CHUNK == 0 and ctx_len % HG_CHUNK == 0

    xa = jnp.concatenate([x.reshape(m_lat, d), ctx.reshape(m_ctx, d)], axis=0)
    cs = jnp.concatenate([c, c_ctx[None], jnp.zeros((8 - n_batch - 1, d), F32)], axis=0)
    mod = _modulation(cs, w_mod, b_mod)
    cos, sin = _rope_tables(n_batch, seq, m_ctx)
    lb_all = jnp.cumsum(jax.nn.softmax(hg_lb_logits.astype(F32), axis=0), axis=0)
    lb_all = lb_all - lb_all[:1]
    w_in_t = jnp.swapaxes(w_in, 1, 2)

    for l in range(depth):
        need_ctx = l < depth - 1
        n_rows = m_all if need_ctx else m_lat
        mods = [mod[l, :n_batch + 1, i * d:(i + 1) * d].reshape(n_batch + 1, 1, d) for i in range(N_MOD)]

        h = _norm_mod(xa, norm1_g[l], mods[0], mods[1], m_all, seq, n_batch)
        proj_h = _proj_in(h, w_in_t, l, *H_TILES, BF16)
        proj_f = _proj_in(h, w_in_t, l, *F_TILES, F32)

        q_rot, k_rot = _att_prep(proj_h, cos, sin, attn_q_norm_g[l], attn_k_norm_g[l])
        ya = _attention(q_rot, k_rot, proj_h, n_batch, seq, ctx_len, True)

        gates = proj_f[:, F_ML_G:F_ML_G + 4 * ML_HEADS].reshape(m_all, 4, ML_HEADS)
        gates_c = gates.transpose(2, 0, 1)
        gates_r = gates.transpose(2, 1, 0)
        ym, ym_c = _mlstm(proj_h, gates_c, gates_r, ml_conv_w[l], ml_gate_b[l], ml_norm_g[l],
                          n_batch, seq, ctx_len, need_ctx)
        yh, yh_c = _hgrn2(proj_h, proj_f, lb_all[l], hg_norm_g[l], n_batch, seq, ctx_len, need_ctx)
        ys_ctx = None
        if need_ctx:
            ya_c = _attention(q_rot, k_rot, proj_h, n_batch, seq, ctx_len, False)
            ys_ctx = (ya_c, ym_c, yh_c)

        merged = _branch_merge((ya, ym, yh), ys_ctx, w_branch, l, proj_h, n_rows)
        xa = _matmul_residual(merged, w_out, l, xa, mods[2], n_rows, seq, n_batch, "proj_out")

        if l % 2 == 0:
            h2 = _norm_mod(xa, norm2_g[l], mods[3], mods[4], n_rows, seq, n_batch)
            u = _glu(h2, ffn_w_gate, ffn_w_up, l // 2)
            xa = _matmul_residual(u, ffn_w_down, l // 2, xa, mods[5], n_rows, seq, n_batch, "ffn_down")
        else:
            xa = _moe_ffn(xa, norm2_g[l], mods[3], mods[4], mods[5], moe_w_router[l // 2],
                          moe_w_gate[l // 2], moe_w_up[l // 2], moe_w_down[l // 2], n_rows, seq, n_batch)
    return xa[:m_lat].reshape(n_batch, seq, d)
```
